```python
import jax, jax.numpy as jnp
from jax import lax
import numpy as np

D_MODEL = 1024
BATCH = 2
SEQ = 16384
DEPTH = 1
DEC_BATCH = 32
DEC_SEQ = 32
PAST_LEN = 4096

CHUNK = 64
MIX_WIDTH = D_MODEL
GMLP_WIDTH = MIX_WIDTH // 2
GMLP_GROUPS = 4
GMLP_HEAD = GMLP_WIDTH // GMLP_GROUPS
GMLP_CHUNK = 128
GLA_WIDTH = MIX_WIDTH - GMLP_WIDTH
GLA_HEADS = 4
GLA_DV = GLA_WIDTH // GLA_HEADS
GLA_DK = GLA_DV // 2
GLA_KEY_WIDTH = GLA_HEADS * GLA_DK
GLA_GATE_RANK = 16
GLA_GATE_NORMALIZER = 16.0
GLA_BLOCK = 16
N_EXPERTS = 64
TOP_K = 8
N_EXPERT_GROUPS = 8
TOPK_GROUPS = 4
EXPERT_FF = 256
SHARED_FF = 256
ROUTED_SCALE = 2.5
EPS = 1e-6
IN_WIDTH = 2 * GMLP_WIDTH + 2 * GLA_KEY_WIDTH + 2 * GLA_WIDTH + GLA_GATE_RANK

kernel_name = 'hymba_gmlp_gla_moe_adaln_stream_step'


def _rmsnorm(x, g):
    xf = x.astype(jnp.float32)
    y = xf * lax.rsqrt(jnp.mean(xf * xf, axis=-1, keepdims=True) + EPS)
    return (y * g.astype(jnp.float32)).astype(x.dtype)


def _swiglu(x, wg, wu, wd):
    return (jax.nn.silu(x @ wg) * (x @ wu)) @ wd


def _gmlp_spatial_mix(v, w_s, b_s):
    B, T, G, Dh = v.shape
    n_c = -(-T // GMLP_CHUNK)
    pad = n_c * GMLP_CHUNK - T
    vp = jnp.pad(v, ((0, 0), (0, pad), (0, 0), (0, 0))).reshape(B, n_c, GMLP_CHUNK, G, Dh)
    blk = jnp.arange(GMLP_CHUNK) // CHUNK
    mask = blk[:, None] >= blk[None, :]
    ws = jnp.where(mask[None], w_s, 0.0).astype(v.dtype)
    mixed = jnp.einsum('gij,bcjgd->bcigd', ws, vp) + b_s.T.astype(v.dtype)[None, None, :, :, None]
    return mixed.reshape(B, n_c * GMLP_CHUNK, G, Dh)[:, :T]


def _gla_recurrence(q, k, v, la, s0):
    B, T, H, _ = q.shape
    DV = v.shape[-1]
    n_blk = -(-T // GLA_BLOCK)
    pad = n_blk * GLA_BLOCK - T

    def blocks(a):
        a = jnp.pad(a.astype(jnp.float32), ((0, 0), (0, pad), (0, 0), (0, 0)))
        return a.reshape(B, n_blk, GLA_BLOCK, H, a.shape[-1]).transpose(1, 0, 3, 2, 4)

    tri = jnp.tril(jnp.ones((GLA_BLOCK, GLA_BLOCK), dtype=bool))

    def step(S, xs):
        qc, kc, vc, lc = xs
        b = jnp.cumsum(lc, axis=-2)
        b_last = b[..., -1:, :]
        qt = qc * jnp.exp(b)
        kt = kc * jnp.exp(-b)
        att = jnp.where(tri, jnp.einsum('bhid,bhjd->bhij', qt, kt), 0.0)
        o = jnp.einsum('bhij,bhjv->bhiv', att, vc) + jnp.einsum('bhid,bhdv->bhiv', qt, S)
        S = jnp.exp(b_last[..., 0, :])[..., None] * S + jnp.einsum('bhjd,bhjv->bhdv', kc * jnp.exp(b_last - b), vc)
        return S, o

    S, o = lax.scan(step, s0.astype(jnp.float32), (blocks(q), blocks(k), blocks(v), blocks(la)))
    o = o.transpose(1, 0, 3, 2, 4).reshape(B, n_blk * GLA_BLOCK, H, DV)[:, :T]
    return o, S


def _moe(h, w_router, b_router, w_e_gate, w_e_up, w_e_down, w_sh_gate, w_sh_up, w_sh_down):
    B, T, D = h.shape
    hf = h.reshape(B * T, D)
    N = hf.shape[0]
    scores = jax.nn.sigmoid(hf.astype(jnp.float32) @ w_router.astype(jnp.float32))
    sel = scores + b_router.astype(jnp.float32)
    grp = sel.reshape(N, N_EXPERT_GROUPS, N_EXPERTS // N_EXPERT_GROUPS)
    grp_score = jnp.sum(lax.top_k(grp, 2)[0], axis=-1)
    _, gidx = lax.top_k(grp_score, TOPK_GROUPS)
    gmask = jnp.sum(jax.nn.one_hot(gidx, N_EXPERT_GROUPS, dtype=jnp.float32), axis=1) > 0
    emask = jnp.repeat(gmask, N_EXPERTS // N_EXPERT_GROUPS, axis=1)
    _, eidx = lax.top_k(jnp.where(emask, sel, -jnp.inf), TOP_K)
    wk = jnp.take_along_axis(scores, eidx, axis=1)
    wk = wk / jnp.sum(wk, axis=-1, keepdims=True) * ROUTED_SCALE
    gates = jnp.sum(jax.nn.one_hot(eidx, N_EXPERTS, dtype=jnp.float32) * wk[..., None], axis=1)

    def expert_step(acc, xs):
        wg, wu, wd, g = xs
        out = _swiglu(hf, wg, wu, wd)
        return acc + g[:, None] * out.astype(jnp.float32), None

    y, _ = lax.scan(expert_step, jnp.zeros((N, D), jnp.float32), (w_e_gate, w_e_up, w_e_down, gates.T))
    y = y.astype(h.dtype) + _swiglu(hf, w_sh_gate, w_sh_up, w_sh_down)
    return y.reshape(B, T, D)


def _hybrid_layer(x, c, s0, w_ada, b_ada, g_pre1, g_post1, w_in, w_gk_up, b_gk, g_gmlp_v, w_s, b_s,
                  g_gla_o, w_out, g_pre2, g_post2, w_router, b_router, w_e_gate, w_e_up, w_e_down,
                  w_sh_gate, w_sh_up, w_sh_down):
    B, T, _ = x.shape
    mod = (jax.nn.silu(c) @ w_ada + b_ada)[:, None, :]
    shift1, scale1, gate1, shift2, scale2, gate2 = jnp.split(mod, 6, axis=-1)

    h = _rmsnorm(x, g_pre1) * (1.0 + scale1) + shift1
    proj = h @ w_in
    cuts = [GMLP_WIDTH, 2 * GMLP_WIDTH, 2 * GMLP_WIDTH + GLA_KEY_WIDTH, 2 * GMLP_WIDTH + 2 * GLA_KEY_WIDTH,
            2 * GMLP_WIDTH + 2 * GLA_KEY_WIDTH + GLA_WIDTH, 2 * GMLP_WIDTH + 2 * GLA_KEY_WIDTH + 2 * GLA_WIDTH]
    u, vg, q, k, vl, r, gk = jnp.split(proj, cuts, axis=-1)

    u = jax.nn.gelu(u).reshape(B, T, GMLP_GROUPS, GMLP_HEAD)
    vg = _rmsnorm(jax.nn.gelu(vg), g_gmlp_v).reshape(B, T, GMLP_GROUPS, GMLP_HEAD)
    gmlp_out = (u * _gmlp_spatial_mix(vg, w_s, b_s)).reshape(B, T, GMLP_WIDTH)

    q = q.reshape(B, T, GLA_HEADS, GLA_DK) * (GLA_DK ** -0.5)
    k = k.reshape(B, T, GLA_HEADS, GLA_DK)
    vl = vl.reshape(B, T, GLA_HEADS, GLA_DV)
    la = jax.nn.log_sigmoid((gk @ w_gk_up + b_gk).astype(jnp.float32)) / GLA_GATE_NORMALIZER
    la = la.reshape(B, T, GLA_HEADS, GLA_DK)
    o, s_new = _gla_recurrence(q, k, vl, la, s0)
    o = _rmsnorm(o, g_gla_o).astype(x.dtype).reshape(B, T, GLA_WIDTH)
    gla_out = o * jax.nn.silu(r)

    mix = jnp.concatenate([gmlp_out, gla_out], axis=-1) @ w_out
    x = x + gate1 * _rmsnorm(mix, g_post1)

    h2 = _rmsnorm(x, g_pre2) * (1.0 + scale2) + shift2
    f = _moe(h2, w_router, b_router, w_e_gate, w_e_up, w_e_down, w_sh_gate, w_sh_up, w_sh_down)
    x = x + gate2 * _rmsnorm(f, g_post2)
    return x, s_new, vg


def setup_inputs(seed: int = 0) -> dict:
    key = jax.random.key(seed)
    ks = jax.random.split(key, 32)
    n = lambda i, shape, s: jax.random.normal(ks[i], shape, jnp.float32) * s
    gain = lambda i, shape: 1.0 + 0.02 * jax.random.normal(ks[i], shape, jnp.float32)
    D = D_MODEL
    return {
        'x_prompt': n(0, (BATCH, SEQ, D), 1.0),
        'x_sample': n(1, (DEC_BATCH, DEC_SEQ, D), 1.0),
        'state_gla': n(2, (DEPTH, DEC_BATCH, GLA_HEADS, GLA_DK, GLA_DV), 0.5),
        'c_prompt': n(3, (BATCH, D), 1.0),
        'c_sample': n(4, (DEC_BATCH, D), 1.0),
        'w_ada': n(5, (DEPTH, D, 6 * D), 0.5 * D ** -0.5),
        'b_ada': n(6, (DEPTH, 6 * D), 0.02),
        'g_pre1': gain(7, (DEPTH, D)),
        'g_post1': gain(8, (DEPTH, D)),
        'w_in': n(9, (DEPTH, D, IN_WIDTH), D ** -0.5),
        'w_gk_up': n(10, (DEPTH, GLA_GATE_RANK, GLA_KEY_WIDTH), GLA_GATE_RANK ** -0.5),
        'b_gk': n(11, (DEPTH, GLA_KEY_WIDTH), 0.02),
        'g_gmlp_v': gain(12, (DEPTH, GMLP_WIDTH)),
        'w_s': n(13, (DEPTH, GMLP_GROUPS, GMLP_CHUNK, GMLP_CHUNK), GMLP_CHUNK ** -0.5),
        'b_s': gain(14, (DEPTH, GMLP_GROUPS, GMLP_CHUNK)),
        'g_gla_o': gain(15, (DEPTH, GLA_HEADS, GLA_DV)),
        'w_out': n(16, (DEPTH, MIX_WIDTH, D), MIX_WIDTH ** -0.5),
        'g_pre2': gain(17, (DEPTH, D)),
        'g_post2': gain(18, (DEPTH, D)),
        'w_router': n(19, (DEPTH, D, N_EXPERTS), D ** -0.5),
        'b_router': n(20, (DEPTH, N_EXPERTS), 0.01),
        'w_e_gate': n(21, (DEPTH, N_EXPERTS, D, EXPERT_FF), D ** -0.5),
        'w_e_up': n(22, (DEPTH, N_EXPERTS, D, EXPERT_FF), D ** -0.5),
        'w_e_down': n(23, (DEPTH, N_EXPERTS, EXPERT_FF, D), EXPERT_FF ** -0.5),
        'w_sh_gate': n(24, (DEPTH, D, SHARED_FF), D ** -0.5),
        'w_sh_up': n(25, (DEPTH, D, SHARED_FF), D ** -0.5),
        'w_sh_down': n(26, (DEPTH, SHARED_FF, D), SHARED_FF ** -0.5),
    }


def reference(x_prompt, x_sample, state_gla, c_prompt, c_sample, w_ada, b_ada, g_pre1, g_post1, w_in,
              w_gk_up, b_gk, g_gmlp_v, w_s, b_s, g_gla_o, w_out, g_pre2, g_post2, w_router, b_router,
              w_e_gate, w_e_up, w_e_down, w_sh_gate, w_sh_up, w_sh_down):
    x_p, x_s = x_prompt, x_sample
    sp_list, ss_list, vs_list = [], [], []
    for l in range(DEPTH):
        lw = (w_ada[l], b_ada[l], g_pre1[l], g_post1[l], w_in[l], w_gk_up[l], b_gk[l], g_gmlp_v[l], w_s[l],
              b_s[l], g_gla_o[l], w_out[l], g_pre2[l], g_post2[l], w_router[l], b_router[l], w_e_gate[l],
              w_e_up[l], w_e_down[l], w_sh_gate[l], w_sh_up[l], w_sh_down[l])
        s0 = jnp.zeros((x_p.shape[0], GLA_HEADS, GLA_DK, GLA_DV), jnp.float32)
        x_p, s_p, _ = _hybrid_layer(x_p, c_prompt, s0, *lw)
        x_s, s_s, v_s = _hybrid_layer(x_s, c_sample, state_gla[l], *lw)
        sp_list.append(s_p)
        ss_list.append(s_s)
        vs_list.append(v_s)
    new_state_gla_prompt = jnp.stack(sp_list)
    new_state_gla_sample = jnp.stack(ss_list)
    new_gmlp_v_sample = jnp.stack(vs_list)
    return (x_p, x_s, new_state_gla_prompt, new_state_gla_sample, new_gmlp_v_sample)
```

```python
import functools

import jax
import jax.numpy as jnp
from jax import lax
from jax.experimental import pallas as pl
from jax.experimental.pallas import tpu as pltpu

D_MODEL = 1024
GMLP_WIDTH = 512
GMLP_GROUPS = 4
GMLP_HEAD = 128
GMLP_CHUNK = 128
CAUSAL_BLOCK = 64
GLA_WIDTH = 512
GLA_HEADS = 4
GLA_DV = 128
GLA_DK = 64
GLA_KEY_WIDTH = 256
GLA_GATE_RANK = 16
GLA_GATE_NORMALIZER = 16.0
N_EXPERTS = 64
TOP_K = 8
N_EXPERT_GROUPS = 8
GROUP_SIZE = N_EXPERTS // N_EXPERT_GROUPS
TOPK_GROUPS = 4
EXPERT_FF = 256
SHARED_FF = 256
ROUTED_SCALE = 2.5
EPS = 1e-6

LANES = 128
GK_PAD = LANES
IN_WIDTH_PAD = 2 * GMLP_WIDTH + 2 * GLA_KEY_WIDTH + 2 * GLA_WIDTH + GK_PAD
OFF_U = 0
OFF_VG = GMLP_WIDTH
OFF_Q = 2 * GMLP_WIDTH
OFF_K = OFF_Q + GLA_KEY_WIDTH
OFF_VL = OFF_K + GLA_KEY_WIDTH
OFF_R = OFF_VL + GLA_WIDTH
OFF_GK = OFF_R + GLA_WIDTH
MOD_ROWS = 8
VMEM_LIMIT = 56 * 1024 * 1024

F32 = jnp.float32
BF16 = jnp.bfloat16
NT_DIMS = (((1,), (1,)), ((), ()))
TN_DIMS = (((0,), (0,)), ((), ()))


def _rms(x, g):
    return x * lax.rsqrt(jnp.mean(x * x, axis=-1, keepdims=True) + EPS) * g


def _gelu(x):
    return 0.5 * x * (1.0 + jnp.tanh(0.7978845608028654 * (x + 0.044715 * (x * x * x))))


def _sigmoid(x):
    return 1.0 / (1.0 + jnp.exp(-x))


def _silu(x):
    return x * _sigmoid(x)


def _ada_kernel(c_ref, w_ref, b_ref, o_ref):
    a = _silu(c_ref[...])
    o_ref[0] = jnp.dot(a, w_ref[...], precision=lax.Precision.HIGHEST,
                       preferred_element_type=F32) + b_ref[0]


def _ada(c_all, w_ada, b_ada):
    n = c_all.shape[0]
    return pl.pallas_call(
        _ada_kernel,
        grid=(6,),
        in_specs=[pl.BlockSpec((n, D_MODEL), lambda j: (0, 0)),
                  pl.BlockSpec((D_MODEL, D_MODEL), lambda j: (0, j)),
                  pl.BlockSpec((1, 1, D_MODEL), lambda j: (j, 0, 0))],
        out_specs=pl.BlockSpec((1, n, D_MODEL), lambda j: (j, 0, 0)),
        out_shape=jax.ShapeDtypeStruct((6, n, D_MODEL), F32),
        compiler_params=pltpu.CompilerParams(vmem_limit_bytes=VMEM_LIMIT),
        name="ada",
    )(c_all, w_ada, b_ada.reshape(6, 1, D_MODEL))


def _route(logits_t, bias_t):
    t = logits_t.shape[1]
    scores = _sigmoid(logits_t)
    sel = scores + bias_t
    sub = lax.broadcasted_iota(jnp.int32, (GROUP_SIZE, t), 0)
    gscore = []
    for g in range(N_EXPERT_GROUPS):
        blk = sel[g * GROUP_SIZE:(g + 1) * GROUP_SIZE]
        m1 = jnp.max(blk, axis=0, keepdims=True)
        first = jnp.min(jnp.where(blk == m1, sub, GROUP_SIZE), axis=0, keepdims=True)
        m2 = jnp.max(jnp.where(sub == first, -jnp.inf, blk), axis=0, keepdims=True)
        gscore.append(m1 + m2)
    neg = jnp.full((GROUP_SIZE, t), -jnp.inf, F32)
    masked = []
    for g in range(N_EXPERT_GROUPS):
        rank = jnp.zeros((1, t), jnp.int32)
        for o in range(N_EXPERT_GROUPS):
            if o == g:
                continue
            ahead = (gscore[o] >= gscore[g]) if o < g else (gscore[o] > gscore[g])
            rank = rank + jnp.where(ahead, 1, 0)
        keep = jnp.broadcast_to(rank < TOPK_GROUPS, (GROUP_SIZE, t))
        masked.append(jnp.where(keep, sel[g * GROUP_SIZE:(g + 1) * GROUP_SIZE], neg))
    selm = jnp.concatenate(masked, axis=0)
    eidx = lax.broadcasted_iota(jnp.int32, (N_EXPERTS, t), 0)
    rank = jnp.zeros((N_EXPERTS, t), jnp.int32)
    for o in range(N_EXPERTS):
        so = selm[o:o + 1]
        rank = rank + jnp.where(eidx > o, jnp.where(so >= selm, 1, 0), jnp.where(so > selm, 1, 0))
    chosen = jnp.where(rank < TOP_K, scores, 0.0)
    denom = jnp.sum(chosen, axis=0, keepdims=True)
    return chosen * (ROUTED_SCALE / denom)


def _mixer_kernel(x_ref, mod_ref, s0_ref, gpre1_ref, win_ref, wgk_ref, bgk_ref, ggv_ref, ws_ref,
                  bsf_ref, ggo_ref, wout_ref, gpost1_ref, gpre2_ref, wr_ref, br_ref,
                  x1_ref, h2_ref, gates_ref, st_ref, *rest,
                  chunk, n_chunks, chunk_is_seq, emit_v):
    if emit_v:
        v_ref, proj_scr, mix_scr = rest
    else:
        proj_scr, mix_scr = rest
    ns, ls, _ = x_ref.shape
    tm = ns * ls

    x = x_ref[...]
    mod = mod_ref[...]
    shift1, scale1, gate1 = mod[:, 0:1], mod[:, 1:2], mod[:, 2:3]
    shift2, scale2 = mod[:, 3:4], mod[:, 4:5]

    h = _rms(x, gpre1_ref[...]) * (1.0 + scale1) + shift1
    proj_scr[...] = jnp.dot(h.reshape(tm, D_MODEL).astype(BF16), win_ref[...],
                            preferred_element_type=F32)

    if chunk_is_seq:
        st_ref[...] = s0_ref[...]
    else:
        @pl.when(pl.program_id(1) == 0)
        def _():
            st_ref[...] = s0_ref[...]

    row = lax.broadcasted_iota(jnp.int32, (chunk, chunk), 0)
    col = lax.broadcasted_iota(jnp.int32, (chunk, chunk), 1)
    tri = (row >= col).astype(F32)
    causal = row >= col
    block_causal = (row // CAUSAL_BLOCK) >= (col // CAUSAL_BLOCK)
    wmix = [jnp.where(block_causal, ws_ref[g, 0:chunk, 0:chunk], 0.0).astype(BF16)
            for g in range(GMLP_GROUPS)]
    mid = chunk // 2

    def chunk_body(c, carry):
        rows = pl.ds(pl.multiple_of(c * chunk, chunk), chunk)
        sidx = c if chunk_is_seq else 0

        vg = _gelu(proj_scr[rows, OFF_VG:OFF_VG + GMLP_WIDTH])
        vn = _rms(vg, ggv_ref[...])
        if emit_v:
            v_ref[rows, :] = vn
        vnb = vn.astype(BF16)
        mixed = jnp.concatenate(
            [jnp.dot(wmix[g], vnb[:, g * GMLP_HEAD:(g + 1) * GMLP_HEAD], preferred_element_type=F32)
             for g in range(GMLP_GROUPS)], axis=-1) + bsf_ref[0:chunk, :]
        u = _gelu(proj_scr[rows, OFF_U:OFF_U + GMLP_WIDTH])
        mix_scr[rows, 0:GMLP_WIDTH] = (u * mixed).astype(BF16)

        q = proj_scr[rows, OFF_Q:OFF_Q + GLA_KEY_WIDTH] * (GLA_DK ** -0.5)
        k = proj_scr[rows, OFF_K:OFF_K + GLA_KEY_WIDTH]
        gk = proj_scr[rows, OFF_GK:OFF_GK + GK_PAD]
        z = jnp.dot(gk.astype(BF16), wgk_ref[...], preferred_element_type=F32) + bgk_ref[...]
        la = (jnp.minimum(z, 0.0) - jnp.log(1.0 + jnp.exp(-jnp.abs(z)))) * (1.0 / GLA_GATE_NORMALIZER)
        b = jnp.dot(tri, la, precision=lax.Precision.HIGHEST, preferred_element_type=F32)
        b_mid = b[mid - 1:mid]
        b_end = b[chunk - 1:chunk]
        qt = (q * jnp.exp(b - b_mid)).astype(BF16)
        kt = (k * jnp.exp(b_mid - b)).astype(BF16)
        qs = (q * jnp.exp(b)).astype(BF16)
        kd = (k * jnp.exp(b_end - b)).astype(BF16)
        decay = jnp.exp(b_end)
        for hd in range(GLA_HEADS):
            ksl = slice(hd * GLA_DK, (hd + 1) * GLA_DK)
            vsl = slice(OFF_VL + hd * GLA_DV, OFF_VL + (hd + 1) * GLA_DV)
            rsl = slice(OFF_R + hd * GLA_DV, OFF_R + (hd + 1) * GLA_DV)
            att = lax.dot_general(qt[:, ksl], kt[:, ksl], NT_DIMS, preferred_element_type=F32)
            att = jnp.where(causal, att, 0.0).astype(BF16)
            vh = proj_scr[rows, vsl].astype(BF16)
            st = st_ref[sidx, hd]
            o = jnp.dot(att, vh, preferred_element_type=F32) + lax.dot_general(
                qs[:, ksl], st.astype(BF16), NT_DIMS, preferred_element_type=F32)
            on = _rms(o, ggo_ref[hd:hd + 1, :])
            mix_scr[rows, GMLP_WIDTH + hd * GLA_DV:GMLP_WIDTH + (hd + 1) * GLA_DV] = (
                on * _silu(proj_scr[rows, rsl])).astype(BF16)
            st_ref[sidx, hd] = st * decay[:, ksl] + lax.dot_general(
                vh, kd[:, ksl], TN_DIMS, preferred_element_type=F32)
        return carry

    lax.fori_loop(0, n_chunks, chunk_body, 0)

    mixo = jnp.dot(mix_scr[...], wout_ref[...], preferred_element_type=F32).reshape(ns, ls, D_MODEL)
    x1 = x + gate1 * _rms(mixo, gpost1_ref[...])
    x1_ref[...] = x1
    h2 = (_rms(x1, gpre2_ref[...]) * (1.0 + scale2) + shift2).reshape(tm, D_MODEL)
    h2_ref[...] = h2.astype(BF16)

    logits = jnp.dot(h2, wr_ref[...], precision=lax.Precision.HIGHEST, preferred_element_type=F32)
    gates_t = _route(logits.T[0:N_EXPERTS], br_ref[...])
    gates_ref[...] = jnp.concatenate([gates_t, jnp.zeros_like(gates_t)], axis=0).T


def _mixer(x, mod, s0_t, w, *, seq_tile, chunk, emit_v):
    n_seq, seq_len, _ = x.shape
    chunk_is_seq = seq_len == chunk
    if chunk_is_seq:
        ns, ls = seq_tile // chunk, chunk
        grid = (1, n_seq // ns)
        x_map = lambda b, t: (t, 0, 0)
        seq_map3 = lambda b, t: (t, 0, 0)
        seq_map4 = lambda b, t: (t, 0, 0, 0)
    else:
        ns, ls = 1, seq_tile
        grid = (n_seq, seq_len // seq_tile)
        x_map = lambda b, t: (b, t, 0)
        seq_map3 = lambda b, t: (b, 0, 0)
        seq_map4 = lambda b, t: (b, 0, 0, 0)
    tm = ns * ls
    n_tok = n_seq * seq_len
    tok_map = lambda b, t: (b * grid[1] + t, 0)
    full2 = lambda b, t: (0, 0)
    full3 = lambda b, t: (0, 0, 0)

    in_specs = [
        pl.BlockSpec((ns, ls, D_MODEL), x_map),
        pl.BlockSpec((ns, MOD_ROWS, D_MODEL), seq_map3),
        pl.BlockSpec((ns, GLA_HEADS, GLA_DV, GLA_DK), seq_map4),
        pl.BlockSpec((1, D_MODEL), full2),
        pl.BlockSpec((D_MODEL, IN_WIDTH_PAD), full2),
        pl.BlockSpec((GK_PAD, GLA_KEY_WIDTH), full2),
        pl.BlockSpec((1, GLA_KEY_WIDTH), full2),
        pl.BlockSpec((1, GMLP_WIDTH), full2),
        pl.BlockSpec((GMLP_GROUPS, GMLP_CHUNK, GMLP_CHUNK), full3),
        pl.BlockSpec((GMLP_CHUNK, GMLP_WIDTH), full2),
        pl.BlockSpec((GLA_HEADS, GLA_DV), full2),
        pl.BlockSpec((D_MODEL, D_MODEL), full2),
        pl.BlockSpec((1, D_MODEL), full2),
        pl.BlockSpec((1, D_MODEL), full2),
        pl.BlockSpec((D_MODEL, LANES), full2),
        pl.BlockSpec((N_EXPERTS, 1), full2),
    ]
    out_specs = [
        pl.BlockSpec((ns, ls, D_MODEL), x_map),
        pl.BlockSpec((tm, D_MODEL), tok_map),
        pl.BlockSpec((tm, LANES), tok_map),
        pl.BlockSpec((ns, GLA_HEADS, GLA_DV, GLA_DK), seq_map4),
    ]
    out_shape = [
        jax.ShapeDtypeStruct((n_seq, seq_len, D_MODEL), F32),
        jax.ShapeDtypeStruct((n_tok, D_MODEL), BF16),
        jax.ShapeDtypeStruct((n_tok, LANES), F32),
        jax.ShapeDtypeStruct((n_seq, GLA_HEADS, GLA_DV, GLA_DK), F32),
    ]
    if emit_v:
        out_specs.append(pl.BlockSpec((tm, GMLP_WIDTH), tok_map))
        out_shape.append(jax.ShapeDtypeStruct((n_tok, GMLP_WIDTH), F32))

    kern = functools.partial(_mixer_kernel, chunk=chunk, n_chunks=tm // chunk,
                             chunk_is_seq=chunk_is_seq, emit_v=emit_v)
    return pl.pallas_call(
        kern,
        grid=grid,
        in_specs=in_specs,
        out_specs=out_specs,
        out_shape=out_shape,
        scratch_shapes=[pltpu.VMEM((tm, IN_WIDTH_PAD), F32), pltpu.VMEM((tm, D_MODEL), BF16)],
        compiler_params=pltpu.CompilerParams(
            dimension_semantics=("arbitrary", "arbitrary"), vmem_limit_bytes=VMEM_LIMIT),
        name="mixer_sample" if chunk_is_seq else "mixer_prompt",
    )(x, mod, s0_t, w["g_pre1"], w["w_in"], w["w_gk_up"], w["b_gk"], w["g_gmlp_v"], w["w_s"],
      w["b_s_full"], w["g_gla_o"], w["w_out"], w["g_post1"], w["g_pre2"], w["w_router"],
      w["b_router"])


def _moe_kernel(h2_ref, gates_ref, x1_ref, mod_ref, wgu_ref, wd_ref, wsgu_ref, wsd_ref, gpost2_ref,
                out_ref, acc_ref):
    e = pl.program_id(1)
    ns, ls, _ = x1_ref.shape
    tm = ns * ls

    @pl.when(e == 0)
    def _():
        acc_ref[...] = jnp.zeros_like(acc_ref)

    hx = h2_ref[...]
    lane = lax.broadcasted_iota(jnp.int32, (tm, LANES), 1)
    g = jnp.sum(jnp.where(lane == e, gates_ref[...], 0.0), axis=1, keepdims=True)
    gu = jnp.dot(hx, wgu_ref[0], preferred_element_type=F32)
    hid = _silu(gu[:, 0:EXPERT_FF]) * gu[:, EXPERT_FF:2 * EXPERT_FF] * g
    acc_ref[...] += jnp.dot(hid.astype(BF16), wd_ref[0], preferred_element_type=F32)

    @pl.when(e == N_EXPERTS - 1)
    def _():
        sgu = jnp.dot(hx, wsgu_ref[...], preferred_element_type=F32)
        shid = _silu(sgu[:, 0:SHARED_FF]) * sgu[:, SHARED_FF:2 * SHARED_FF]
        f = acc_ref[...] + jnp.dot(shid.astype(BF16), wsd_ref[...], preferred_element_type=F32)
        gate2 = mod_ref[...][:, 5:6]
        out_ref[...] = x1_ref[...] + gate2 * _rms(f, gpost2_ref[...]).reshape(ns, ls, D_MODEL)


def _moe(h2, gates, x1, mod, w, *, seq_tile):
    n_seq, seq_len, _ = x1.shape
    if seq_len < seq_tile:
        ns, ls = seq_tile // seq_len, seq_len
        n_tiles = n_seq // ns
        x_map = lambda i, e: (i, 0, 0)
        seq_map3 = lambda i, e: (i, 0, 0)
    else:
        ns, ls = 1, seq_tile
        per_seq = seq_len // seq_tile
        n_tiles = n_seq * per_seq
        x_map = lambda i, e: (i // per_seq, i % per_seq, 0)
        seq_map3 = lambda i, e: (i // per_seq, 0, 0)
    tm = ns * ls
    tok_map = lambda i, e: (i, 0)
    full2 = lambda i, e: (0, 0)
    return pl.pallas_call(
        _moe_kernel,
        grid=(n_tiles, N_EXPERTS),
        in_specs=[
            pl.BlockSpec((tm, D_MODEL), tok_map),
            pl.BlockSpec((tm, LANES), tok_map),
            pl.BlockSpec((ns, ls, D_MODEL), x_map),
            pl.BlockSpec((ns, MOD_ROWS, D_MODEL), seq_map3),
            pl.BlockSpec((1, D_MODEL, 2 * EXPERT_FF), lambda i, e: (e, 0, 0)),
            pl.BlockSpec((1, EXPERT_FF, D_MODEL), lambda i, e: (e, 0, 0)),
            pl.BlockSpec((D_MODEL, 2 * SHARED_FF), full2),
            pl.BlockSpec((SHARED_FF, D_MODEL), full2),
            pl.BlockSpec((1, D_MODEL), full2),
        ],
        out_specs=pl.BlockSpec((ns, ls, D_MODEL), x_map),
        out_shape=jax.ShapeDtypeStruct(x1.shape, F32),
        scratch_shapes=[pltpu.VMEM((tm, D_MODEL), F32)],
        compiler_params=pltpu.CompilerParams(
            dimension_semantics=("arbitrary", "arbitrary"), vmem_limit_bytes=VMEM_LIMIT),
        name="moe",
    )(h2, gates, x1, mod, w["w_e_gu"], w["w_e_down"], w["w_sh_gu"], w["w_sh_down"], w["g_post2"])


def _prep_weights(g_pre1, g_post1, w_in, w_gk_up, b_gk, g_gmlp_v, w_s, b_s, g_gla_o, w_out, g_pre2,
                  g_post2, w_router, b_router, w_e_gate, w_e_up, w_e_down, w_sh_gate, w_sh_up,
                  w_sh_down):
    row = lambda v: v.reshape(1, -1)
    return {
        "g_pre1": row(g_pre1), "g_post1": row(g_post1), "g_pre2": row(g_pre2), "g_post2": row(g_post2),
        "w_in": jnp.pad(w_in, ((0, 0), (0, GK_PAD - GLA_GATE_RANK))).astype(BF16),
        "w_gk_up": jnp.pad(w_gk_up, ((0, GK_PAD - GLA_GATE_RANK), (0, 0))).astype(BF16),
        "b_gk": row(b_gk),
        "g_gmlp_v": row(g_gmlp_v),
        "w_s": w_s,
        "b_s_full": jnp.repeat(b_s.T, GMLP_HEAD, axis=1),
        "g_gla_o": g_gla_o,
        "w_out": w_out.astype(BF16),
        "w_router": jnp.pad(w_router, ((0, 0), (0, LANES - N_EXPERTS))),
        "b_router": b_router.reshape(N_EXPERTS, 1),
        "w_e_gu": jnp.concatenate([w_e_gate, w_e_up], axis=-1).astype(BF16),
        "w_e_down": w_e_down.astype(BF16),
        "w_sh_gu": jnp.concatenate([w_sh_gate, w_sh_up], axis=-1).astype(BF16),
        "w_sh_down": w_sh_down.astype(BF16),
    }


PROMPT_TILE = 512
SAMPLE_TILE = 256
MOE_TILE = 1024


def _layer(x, mod, s0, w, *, emit_v):
    n_seq, seq_len, _ = x.shape
    s0_t = jnp.swapaxes(s0, -1, -2)
    if seq_len >= GMLP_CHUNK:
        outs = _mixer(x, mod, s0_t, w, seq_tile=PROMPT_TILE, chunk=GMLP_CHUNK, emit_v=emit_v)
    else:
        outs = _mixer(x, mod, s0_t, w, seq_tile=SAMPLE_TILE, chunk=seq_len, emit_v=emit_v)
    x1, h2, gates, st = outs[:4]
    y = _moe(h2, gates, x1, mod, w, seq_tile=MOE_TILE)
    v = outs[4].reshape(n_seq, seq_len, GMLP_GROUPS, GMLP_HEAD) if emit_v else None
    return y, jnp.swapaxes(st, -1, -2), v


def kernel(x_prompt, x_sample, state_gla, c_prompt, c_sample, w_ada, b_ada, g_pre1, g_post1, w_in, w_gk_up, b_gk, g_gmlp_v, w_s, b_s, g_gla_o, w_out, g_pre2, g_post2, w_router, b_router, w_e_gate, w_e_up, w_e_down, w_sh_gate, w_sh_up, w_sh_down):
    depth = w_ada.shape[0]
    n_p, n_s = x_prompt.shape[0], x_sample.shape[0]
    c_all = jnp.concatenate([c_prompt, c_sample], axis=0)
    x_p, x_s = x_prompt, x_sample
    sp_list, ss_list, vs_list = [], [], []
    for l in range(depth):
        w = _prep_weights(g_pre1[l], g_post1[l], w_in[l], w_gk_up[l], b_gk[l], g_gmlp_v[l], w_s[l],
                          b_s[l], g_gla_o[l], w_out[l], g_pre2[l], g_post2[l], w_router[l],
                          b_router[l], w_e_gate[l], w_e_up[l], w_e_down[l], w_sh_gate[l], w_sh_up[l],
                          w_sh_down[l])
        mod = _ada(c_all, w_ada[l], b_ada[l])
        mod = jnp.pad(jnp.swapaxes(mod, 0, 1), ((0, 0), (0, MOD_ROWS - 6), (0, 0)))
        s0_p = jnp.zeros((n_p, GLA_HEADS, GLA_DK, GLA_DV), F32)
        x_p, s_p, _ = _layer(x_p, mod[:n_p], s0_p, w, emit_v=False)
        x_s, s_s, v_s = _layer(x_s, mod[n_p:], state_gla[l], w, emit_v=True)
        sp_list.append(s_p)
        ss_list.append(s_s)
        vs_list.append(v_s)
    return (x_p, x_s, jnp.stack(sp_list), jnp.stack(ss_list), jnp.stack(vs_list))
```

```python
import functools

import jax
import jax.numpy as jnp
from jax import lax
from jax.experimental import pallas as pl
from jax.experimental.pallas import tpu as pltpu

D_MODEL = 1024
GMLP_WIDTH = 512
GMLP_GROUPS = 4
GMLP_HEAD = 128
GMLP_CHUNK = 128
CAUSAL_BLOCK = 64
GLA_WIDTH = 512
GLA_HEADS = 4
GLA_DV = 128
GLA_DK = 64
GLA_KEY_WIDTH = 256
GLA_GATE_RANK = 16
GLA_GATE_NORMALIZER = 16.0
N_EXPERTS = 64
TOP_K = 8
N_EXPERT_GROUPS = 8
GROUP_SIZE = N_EXPERTS // N_EXPERT_GROUPS
TOPK_GROUPS = 4
EXPERT_FF = 256
SHARED_FF = 256
ROUTED_SCALE = 2.5
EPS = 1e-6

LANES = 128
GK_PAD = LANES
IN_WIDTH_PAD = 2 * GMLP_WIDTH + 2 * GLA_KEY_WIDTH + 2 * GLA_WIDTH + GK_PAD
OFF_U = 0
OFF_VG = GMLP_WIDTH
OFF_Q = 2 * GMLP_WIDTH
OFF_K = OFF_Q + GLA_KEY_WIDTH
OFF_VL = OFF_K + GLA_KEY_WIDTH
OFF_R = OFF_VL + GLA_WIDTH
OFF_GK = OFF_R + GLA_WIDTH
MOD_ROWS = 8
VMEM_LIMIT = 56 * 1024 * 1024

F32 = jnp.float32
BF16 = jnp.bfloat16
NT_DIMS = (((1,), (1,)), ((), ()))
TN_DIMS = (((0,), (0,)), ((), ()))


def _rms(x, g):
    return x * lax.rsqrt(jnp.mean(x * x, axis=-1, keepdims=True) + EPS) * g


def _gelu(x):
    return 0.5 * x * (1.0 + jnp.tanh(0.7978845608028654 * (x + 0.044715 * (x * x * x))))


def _sigmoid(x):
    return 1.0 / (1.0 + jnp.exp(-x))


def _silu(x):
    return x * _sigmoid(x)


def _ada_kernel(c_ref, w_ref, b_ref, o_ref):
    a = _silu(c_ref[...])
    o_ref[0] = jnp.dot(a, w_ref[...], precision=lax.Precision.HIGHEST,
                       preferred_element_type=F32) + b_ref[0]


def _ada(c_all, w_ada, b_ada):
    n = c_all.shape[0]
    return pl.pallas_call(
        _ada_kernel,
        grid=(6,),
        in_specs=[pl.BlockSpec((n, D_MODEL), lambda j: (0, 0)),
                  pl.BlockSpec((D_MODEL, D_MODEL), lambda j: (0, j)),
                  pl.BlockSpec((1, 1, D_MODEL), lambda j: (j, 0, 0))],
        out_specs=pl.BlockSpec((1, n, D_MODEL), lambda j: (j, 0, 0)),
        out_shape=jax.ShapeDtypeStruct((6, n, D_MODEL), F32),
        compiler_params=pltpu.CompilerParams(vmem_limit_bytes=VMEM_LIMIT),
        name="ada",
    )(c_all, w_ada, b_ada.reshape(6, 1, D_MODEL))


def _route(logits_t, bias_t):
    t = logits_t.shape[1]
    scores = _sigmoid(logits_t)
    sel = scores + bias_t
    sub = lax.broadcasted_iota(jnp.int32, (GROUP_SIZE, t), 0)
    gscore = []
    for g in range(N_EXPERT_GROUPS):
        blk = sel[g * GROUP_SIZE:(g + 1) * GROUP_SIZE]
        m1 = jnp.max(blk, axis=0, keepdims=True)
        first = jnp.min(jnp.where(blk == m1, sub, GROUP_SIZE), axis=0, keepdims=True)
        m2 = jnp.max(jnp.where(sub == first, -jnp.inf, blk), axis=0, keepdims=True)
        gscore.append(m1 + m2)
    neg = jnp.full((GROUP_SIZE, t), -jnp.inf, F32)
    masked = []
    for g in range(N_EXPERT_GROUPS):
        rank = jnp.zeros((1, t), jnp.int32)
        for o in range(N_EXPERT_GROUPS):
            if o == g:
                continue
            ahead = (gscore[o] >= gscore[g]) if o < g else (gscore[o] > gscore[g])
            rank = rank + jnp.where(ahead, 1, 0)
        keep = jnp.broadcast_to(rank < TOPK_GROUPS, (GROUP_SIZE, t))
        masked.append(jnp.where(keep, sel[g * GROUP_SIZE:(g + 1) * GROUP_SIZE], neg))
    selm = jnp.concatenate(masked, axis=0)
    eidx = lax.broadcasted_iota(jnp.int32, (N_EXPERTS, t), 0)
    rank = jnp.zeros((N_EXPERTS, t), jnp.int32)
    for o in range(N_EXPERTS):
        so = selm[o:o + 1]
        rank = rank + jnp.where(eidx > o, jnp.where(so >= selm, 1, 0), jnp.where(so > selm, 1, 0))
    chosen = jnp.where(rank < TOP_K, scores, 0.0)
    denom = jnp.sum(chosen, axis=0, keepdims=True)
    return chosen * (ROUTED_SCALE / denom)


def _mixer_kernel(x_ref, mod_ref, s0_ref, gpre1_ref, win_ref, wgk_ref, bgk_ref, ggv_ref, ws_ref,
                  bsf_ref, ggo_ref, wout_ref, gpost1_ref, gpre2_ref, wr_ref, br_ref,
                  x1_ref, h2_ref, gates_ref, st_ref, *rest,
                  chunk, n_chunks, chunk_is_seq, emit_v):
    if emit_v:
        v_ref, proj_scr, mix_scr = rest
    else:
        proj_scr, mix_scr = rest
    ns, ls, _ = x_ref.shape
    tm = ns * ls

    x = x_ref[...]
    mod = mod_ref[...]
    shift1, scale1, gate1 = mod[:, 0:1], mod[:, 1:2], mod[:, 2:3]
    shift2, scale2 = mod[:, 3:4], mod[:, 4:5]

    h = _rms(x, gpre1_ref[...]) * (1.0 + scale1) + shift1
    proj_scr[...] = jnp.dot(h.reshape(tm, D_MODEL).astype(BF16), win_ref[...],
                            preferred_element_type=F32)

    if chunk_is_seq:
        st_ref[...] = s0_ref[...]
    else:
        @pl.when(pl.program_id(1) == 0)
        def _():
            st_ref[...] = s0_ref[...]

    row = lax.broadcasted_iota(jnp.int32, (chunk, chunk), 0)
    col = lax.broadcasted_iota(jnp.int32, (chunk, chunk), 1)
    tri = (row >= col).astype(F32)
    causal = row >= col
    block_causal = (row // CAUSAL_BLOCK) >= (col // CAUSAL_BLOCK)
    wmix = [jnp.where(block_causal, ws_ref[g, 0:chunk, 0:chunk], 0.0).astype(BF16)
            for g in range(GMLP_GROUPS)]
    mid = chunk // 2

    def chunk_body(c, carry):
        rows = pl.ds(pl.multiple_of(c * chunk, chunk), chunk)
        sidx = c if chunk_is_seq else 0

        vg = _gelu(proj_scr[rows, OFF_VG:OFF_VG + GMLP_WIDTH])
        vn = _rms(vg, ggv_ref[...])
        if emit_v:
            v_ref[rows, :] = vn
        vnb = vn.astype(BF16)
        mixed = jnp.concatenate(
            [jnp.dot(wmix[g], vnb[:, g * GMLP_HEAD:(g + 1) * GMLP_HEAD], preferred_element_type=F32)
             for g in range(GMLP_GROUPS)], axis=-1) + bsf_ref[0:chunk, :]
        u = _gelu(proj_scr[rows, OFF_U:OFF_U + GMLP_WIDTH])
        mix_scr[rows, 0:GMLP_WIDTH] = (u * mixed).astype(BF16)

        q = proj_scr[rows, OFF_Q:OFF_Q + GLA_KEY_WIDTH] * (GLA_DK ** -0.5)
        k = proj_scr[rows, OFF_K:OFF_K + GLA_KEY_WIDTH]
        gk = proj_scr[rows, OFF_GK:OFF_GK + GK_PAD]
        z = jnp.dot(gk.astype(BF16), wgk_ref[...], preferred_element_type=F32) + bgk_ref[...]
        la = (jnp.minimum(z, 0.0) - jnp.log(1.0 + jnp.exp(-jnp.abs(z)))) * (1.0 / GLA_GATE_NORMALIZER)
        b = jnp.dot(tri, la, precision=lax.Precision.HIGHEST, preferred_element_type=F32)
        b_mid = b[mid - 1:mid]
        b_end = b[chunk - 1:chunk]
        qt = (q * jnp.exp(b - b_mid)).astype(BF16)
        kt = (k * jnp.exp(b_mid - b)).astype(BF16)
        qs = (q * jnp.exp(b)).astype(BF16)
        kd = (k * jnp.exp(b_end - b)).astype(BF16)
        decay = jnp.exp(b_end)
        for hd in range(GLA_HEADS):
            ksl = slice(hd * GLA_DK, (hd + 1) * GLA_DK)
            vsl = slice(OFF_VL + hd * GLA_DV, OFF_VL + (hd + 1) * GLA_DV)
            rsl = slice(OFF_R + hd * GLA_DV, OFF_R + (hd + 1) * GLA_DV)
            att = lax.dot_general(qt[:, ksl], kt[:, ksl], NT_DIMS, preferred_element_type=F32)
            att = jnp.where(causal, att, 0.0).astype(BF16)
            vh = proj_scr[rows, vsl].astype(BF16)
            st = st_ref[sidx, hd]
            o = jnp.dot(att, vh, preferred_element_type=F32) + lax.dot_general(
                qs[:, ksl], st.astype(BF16), NT_DIMS, preferred_element_type=F32)
            on = _rms(o, ggo_ref[hd:hd + 1, :])
            mix_scr[rows, GMLP_WIDTH + hd * GLA_DV:GMLP_WIDTH + (hd + 1) * GLA_DV] = (
                on * _silu(proj_scr[rows, rsl])).astype(BF16)
            st_ref[sidx, hd] = st * decay[:, ksl] + lax.dot_general(
                vh, kd[:, ksl], TN_DIMS, preferred_element_type=F32)
        return carry

    lax.fori_loop(0, n_chunks, chunk_body, 0)

    mixo = jnp.dot(mix_scr[...], wout_ref[...], preferred_element_type=F32).reshape(ns, ls, D_MODEL)
    x1 = x + gate1 * _rms(mixo, gpost1_ref[...])
    x1_ref[...] = x1
    h2 = (_rms(x1, gpre2_ref[...]) * (1.0 + scale2) + shift2).reshape(tm, D_MODEL)
    h2_ref[...] = h2.astype(BF16)

    logits = jnp.dot(h2, wr_ref[...], precision=lax.Precision.HIGHEST, preferred_element_type=F32)
    gates_ref[...] = _route(logits.T[0:N_EXPERTS], br_ref[...])


def _mixer(x, mod, s0_t, w, *, seq_tile, chunk, emit_v):
    n_seq, seq_len, _ = x.shape
    chunk_is_seq = seq_len == chunk
    if chunk_is_seq:
        ns, ls = seq_tile // chunk, chunk
        grid = (1, n_seq // ns)
        x_map = lambda b, t: (t, 0, 0)
        seq_map3 = lambda b, t: (t, 0, 0)
        seq_map4 = lambda b, t: (t, 0, 0, 0)
    else:
        ns, ls = 1, seq_tile
        grid = (n_seq, seq_len // seq_tile)
        x_map = lambda b, t: (b, t, 0)
        seq_map3 = lambda b, t: (b, 0, 0)
        seq_map4 = lambda b, t: (b, 0, 0, 0)
    tm = ns * ls
    n_tok = n_seq * seq_len
    tok_map = lambda b, t: (b * grid[1] + t, 0)
    full2 = lambda b, t: (0, 0)
    full3 = lambda b, t: (0, 0, 0)

    in_specs = [
        pl.BlockSpec((ns, ls, D_MODEL), x_map),
        pl.BlockSpec((ns, MOD_ROWS, D_MODEL), seq_map3),
        pl.BlockSpec((ns, GLA_HEADS, GLA_DV, GLA_DK), seq_map4),
        pl.BlockSpec((1, D_MODEL), full2),
        pl.BlockSpec((D_MODEL, IN_WIDTH_PAD), full2),
        pl.BlockSpec((GK_PAD, GLA_KEY_WIDTH), full2),
        pl.BlockSpec((1, GLA_KEY_WIDTH), full2),
        pl.BlockSpec((1, GMLP_WIDTH), full2),
        pl.BlockSpec((GMLP_GROUPS, GMLP_CHUNK, GMLP_CHUNK), full3),
        pl.BlockSpec((GMLP_CHUNK, GMLP_WIDTH), full2),
        pl.BlockSpec((GLA_HEADS, GLA_DV), full2),
        pl.BlockSpec((D_MODEL, D_MODEL), full2),
        pl.BlockSpec((1, D_MODEL), full2),
        pl.BlockSpec((1, D_MODEL), full2),
        pl.BlockSpec((D_MODEL, LANES), full2),
        pl.BlockSpec((N_EXPERTS, 1), full2),
    ]
    out_specs = [
        pl.BlockSpec((ns, ls, D_MODEL), x_map),
        pl.BlockSpec((tm, D_MODEL), tok_map),
        pl.BlockSpec((N_EXPERTS, tm), lambda b, t: (0, b * grid[1] + t)),
        pl.BlockSpec((ns, GLA_HEADS, GLA_DV, GLA_DK), seq_map4),
    ]
    out_shape = [
        jax.ShapeDtypeStruct((n_seq, seq_len, D_MODEL), F32),
        jax.ShapeDtypeStruct((n_tok, D_MODEL), BF16),
        jax.ShapeDtypeStruct((N_EXPERTS, n_tok), F32),
        jax.ShapeDtypeStruct((n_seq, GLA_HEADS, GLA_DV, GLA_DK), F32),
    ]
    if emit_v:
        out_specs.append(pl.BlockSpec((tm, GMLP_WIDTH), tok_map))
        out_shape.append(jax.ShapeDtypeStruct((n_tok, GMLP_WIDTH), F32))

    kern = functools.partial(_mixer_kernel, chunk=chunk, n_chunks=tm // chunk,
                             chunk_is_seq=chunk_is_seq, emit_v=emit_v)
    return pl.pallas_call(
        kern,
        grid=grid,
        in_specs=in_specs,
        out_specs=out_specs,
        out_shape=out_shape,
        scratch_shapes=[pltpu.VMEM((tm, IN_WIDTH_PAD), F32), pltpu.VMEM((tm, D_MODEL), BF16)],
        compiler_params=pltpu.CompilerParams(
            dimension_semantics=("arbitrary", "arbitrary"), vmem_limit_bytes=VMEM_LIMIT),
        name="mixer_sample" if chunk_is_seq else "mixer_prompt",
    )(x, mod, s0_t, w["g_pre1"], w["w_in"], w["w_gk_up"], w["b_gk"], w["g_gmlp_v"], w["w_s"],
      w["b_s_full"], w["g_gla_o"], w["w_out"], w["g_post1"], w["g_pre2"], w["w_router"],
      w["b_router"])


MOE_BLOCK = 256
SEG_ALIGN = 16
WIN = 64
FFN_TILE = 512
BLOCK_ROWS = 3072
ROW_CHUNK = 512
COUNT_LANES = LANES


def _count_kernel(g_ref, o_ref):
    n = g_ref.shape[1]
    selb = jnp.where(g_ref[...] > 0.0, 1.0, 0.0).astype(BF16)
    tok = lax.broadcasted_iota(jnp.int32, (n, COUNT_LANES), 0)
    lane = lax.broadcasted_iota(jnp.int32, (n, COUNT_LANES), 1)
    ind = jnp.where(tok // MOE_BLOCK == lane, 1.0, 0.0).astype(BF16)
    o_ref[0] = jnp.dot(selb, ind, preferred_element_type=F32)


def _fill_permutation(p_scr, rk_scr, g_ref, cntp_ref, blk, weighted):
    tb = g_ref.shape[1]
    g = g_ref[...]
    sel = g > 0.0
    r_i = lax.broadcasted_iota(jnp.int32, (tb, tb), 0)
    c_i = lax.broadcasted_iota(jnp.int32, (tb, tb), 1)
    earlier = jnp.where(r_i < c_i, 1.0, 0.0).astype(BF16)
    rank = jnp.dot(jnp.where(sel, 1.0, 0.0).astype(BF16), earlier, preferred_element_type=F32)
    rk_scr[...] = jnp.where(sel, rank, -1.0)
    p_scr[...] = jnp.zeros_like(p_scr)
    jwin = lax.broadcasted_iota(jnp.int32, (WIN, tb), 0)

    def expert_body(e, off):
        n = cntp_ref[blk * N_EXPERTS + e]
        rk = rk_scr[pl.ds(e, 1), :]
        val = g_ref[pl.ds(e, 1), :] if weighted else 1.0

        def win_body(wi, c):
            r0 = wi * WIN
            hit = rk == (jwin + r0).astype(F32)
            p_scr[pl.ds(pl.multiple_of(off + r0, SEG_ALIGN), WIN), :] = jnp.where(hit, val, 0.0)
            return c

        lax.fori_loop(0, (n + WIN - 1) // WIN, win_body, 0)
        return off + n

    lax.fori_loop(0, N_EXPERTS, expert_body, 0)


def _for_segments(cntp_ref, blk, make_copy, action):
    def body(e, off):
        n = cntp_ref[blk * N_EXPERTS + e]

        @pl.when(n > 0)
        def _():
            action(make_copy(e, pl.multiple_of(off, SEG_ALIGN), pl.multiple_of(n, SEG_ALIGN)))

        return off + n

    lax.fori_loop(0, N_EXPERTS, body, 0)


def _dispatch_kernel(cntp_ref, goff_ref, toff_ref, tlen_ref, g_ref, h2_ref, xs_hbm,
                     p_scr, rk_scr, xs_scr, zero_scr, sems, tail_sem, *, n_blocks):
    b = pl.program_id(0)
    slot = lax.rem(b, 2)

    def seg_copy(blk, s):
        def make(e, off, n):
            dst = pl.multiple_of(goff_ref[blk * N_EXPERTS + e], SEG_ALIGN)
            return pltpu.make_async_copy(xs_scr.at[s, pl.ds(off, n)], xs_hbm.at[pl.ds(dst, n)], sems.at[s])
        return make

    def for_tails(action):
        def body(e, c):
            n = tlen_ref[e]

            @pl.when(n > 0)
            def _():
                dst = pl.multiple_of(toff_ref[e], SEG_ALIGN)
                nn = pl.multiple_of(n, SEG_ALIGN)
                action(pltpu.make_async_copy(zero_scr.at[pl.ds(0, nn)], xs_hbm.at[pl.ds(dst, nn)], tail_sem))

            return c

        lax.fori_loop(0, N_EXPERTS, body, 0)

    @pl.when(b == 0)
    def _():
        zero_scr[...] = jnp.zeros_like(zero_scr)
        for_tails(lambda cp: cp.start())

    @pl.when(b >= 2)
    def _():
        _for_segments(cntp_ref, b - 2, seg_copy(b - 2, slot), lambda cp: cp.wait())

    _fill_permutation(p_scr, rk_scr, g_ref, cntp_ref, b, weighted=False)
    h2 = h2_ref[...]
    for c in range(BLOCK_ROWS // ROW_CHUNK):
        rows = slice(c * ROW_CHUNK, (c + 1) * ROW_CHUNK)
        xs = jnp.dot(p_scr[rows, :].astype(BF16), h2, preferred_element_type=F32)
        xs_scr[slot, rows, :] = xs.astype(BF16)
    _for_segments(cntp_ref, b, seg_copy(b, slot), lambda cp: cp.start())

    @pl.when(b == n_blocks - 1)
    def _():
        for_tails(lambda cp: cp.wait())
        if n_blocks >= 2:
            _for_segments(cntp_ref, b - 1, seg_copy(b - 1, 1 - slot), lambda cp: cp.wait())
        _for_segments(cntp_ref, b, seg_copy(b, slot), lambda cp: cp.wait())


def _ffn_kernel(te_ref, nu_ref, x_ref, wgu_ref, wd_ref, y_ref):
    @pl.when(pl.program_id(0) < nu_ref[0])
    def _():
        gu = jnp.dot(x_ref[...], wgu_ref[0], preferred_element_type=F32)
        hid = _silu(gu[:, 0:EXPERT_FF]) * gu[:, EXPERT_FF:2 * EXPERT_FF]
        y_ref[...] = jnp.dot(hid.astype(BF16), wd_ref[0], preferred_element_type=F32).astype(BF16)


def _combine_kernel(cntp_ref, goff_ref, g_ref, h2_ref, x1_ref, mod_ref, wsgu_ref, wsd_ref, gpost2_ref,
                    ys_hbm, out_ref, p_scr, rk_scr, ys_scr, sems, *, n_blocks):
    b = pl.program_id(0)
    slot = lax.rem(b, 2)
    ns, ls, _ = x1_ref.shape

    def seg_copy(blk, s):
        def make(e, off, n):
            src = pl.multiple_of(goff_ref[blk * N_EXPERTS + e], SEG_ALIGN)
            return pltpu.make_async_copy(ys_hbm.at[pl.ds(src, n)], ys_scr.at[s, pl.ds(off, n)], sems.at[s])
        return make

    @pl.when(b == 0)
    def _():
        ys_scr[...] = jnp.zeros_like(ys_scr)
        _for_segments(cntp_ref, 0, seg_copy(0, 0), lambda cp: cp.start())

    @pl.when(b + 1 < n_blocks)
    def _():
        _for_segments(cntp_ref, b + 1, seg_copy(b + 1, 1 - slot), lambda cp: cp.start())

    _fill_permutation(p_scr, rk_scr, g_ref, cntp_ref, b, weighted=True)
    _for_segments(cntp_ref, b, seg_copy(b, slot), lambda cp: cp.wait())

    pw = p_scr[...].astype(BF16)
    routed = lax.dot_general(pw, ys_scr[slot], TN_DIMS, preferred_element_type=F32)
    hx = h2_ref[...]
    sgu = jnp.dot(hx, wsgu_ref[...], preferred_element_type=F32)
    shid = _silu(sgu[:, 0:SHARED_FF]) * sgu[:, SHARED_FF:2 * SHARED_FF]
    f = routed + jnp.dot(shid.astype(BF16), wsd_ref[...], preferred_element_type=F32)
    gate2 = mod_ref[...][:, 5:6]
    out_ref[...] = x1_ref[...] + gate2 * _rms(f, gpost2_ref[...]).reshape(ns, ls, D_MODEL)


def _segment_plan(cnt, n_tiles_max):
    cntp = (cnt + SEG_ALIGN - 1) // SEG_ALIGN * SEG_ALIGN
    tot = jnp.sum(cntp, axis=0)
    reg = (tot + FFN_TILE - 1) // FFN_TILE * FFN_TILE
    base = jnp.cumsum(reg) - reg
    goff = base[None, :] + jnp.cumsum(cntp, axis=0) - cntp
    tile_end = jnp.cumsum(reg // FFN_TILE)
    n_used = tile_end[-1:]
    tile = jnp.minimum(jnp.arange(n_tiles_max, dtype=jnp.int32), n_used - 1)
    tile_expert = jnp.minimum(jnp.sum(tile_end[None, :] <= tile[:, None], axis=1), N_EXPERTS - 1)
    i32 = lambda a: a.astype(jnp.int32)
    return (i32(cntp.reshape(-1)), i32(goff.reshape(-1)), i32(base + tot), i32(reg - tot),
            i32(tile_expert), i32(n_used))


def _moe(h2, gates_t, x1, mod, w):
    n_seq, seq_len, _ = x1.shape
    n_tok = n_seq * seq_len
    n_blocks = n_tok // MOE_BLOCK
    if seq_len < MOE_BLOCK:
        ns, ls = MOE_BLOCK // seq_len, seq_len
        x_map = lambda b, *_: (b, 0, 0)
        seq_map3 = lambda b, *_: (b, 0, 0)
    else:
        ns, ls = 1, MOE_BLOCK
        per_seq = seq_len // MOE_BLOCK
        x_map = lambda b, *_: (b // per_seq, b % per_seq, 0)
        seq_map3 = lambda b, *_: (b // per_seq, 0, 0)
    rows_max = (n_tok * TOP_K + n_blocks * N_EXPERTS * (SEG_ALIGN - 1)
                + N_EXPERTS * (FFN_TILE - SEG_ALIGN))
    n_tiles_max = -(-rows_max // FFN_TILE)
    rows_alloc = n_tiles_max * FFN_TILE
    params = pltpu.CompilerParams(dimension_semantics=("arbitrary",), vmem_limit_bytes=VMEM_LIMIT)

    chunk = min(n_tok, COUNT_LANES * MOE_BLOCK // 16)
    per_chunk = chunk // MOE_BLOCK
    cnt = pl.pallas_call(
        _count_kernel,
        grid=(n_tok // chunk,),
        in_specs=[pl.BlockSpec((N_EXPERTS, chunk), lambda i: (0, i))],
        out_specs=pl.BlockSpec((1, N_EXPERTS, COUNT_LANES), lambda i: (i, 0, 0)),
        out_shape=jax.ShapeDtypeStruct((n_tok // chunk, N_EXPERTS, COUNT_LANES), F32),
        compiler_params=params,
        name="moe_count",
    )(gates_t)
    cnt = jnp.swapaxes(cnt[:, :, :per_chunk], 1, 2).reshape(n_blocks, N_EXPERTS).astype(jnp.int32)
    cntp, goff, tail_off, tail_len, tile_expert, n_used = _segment_plan(cnt, n_tiles_max)

    g_spec = pl.BlockSpec((N_EXPERTS, MOE_BLOCK), lambda b, *_: (0, b))
    h2_spec = pl.BlockSpec((MOE_BLOCK, D_MODEL), lambda b, *_: (b, 0))
    any_spec = pl.BlockSpec(memory_space=pl.ANY)
    perm_scratch = [pltpu.VMEM((BLOCK_ROWS, MOE_BLOCK), F32), pltpu.VMEM((N_EXPERTS, MOE_BLOCK), F32)]

    x_sorted = pl.pallas_call(
        functools.partial(_dispatch_kernel, n_blocks=n_blocks),
        grid_spec=pltpu.PrefetchScalarGridSpec(
            num_scalar_prefetch=4, grid=(n_blocks,),
            in_specs=[g_spec, h2_spec], out_specs=any_spec,
            scratch_shapes=perm_scratch + [
                pltpu.VMEM((2, BLOCK_ROWS, D_MODEL), BF16),
                pltpu.VMEM((FFN_TILE, D_MODEL), BF16),
                pltpu.SemaphoreType.DMA((2,)), pltpu.SemaphoreType.DMA(())]),
        out_shape=jax.ShapeDtypeStruct((rows_alloc, D_MODEL), BF16),
        compiler_params=params,
        name="moe_dispatch",
    )(cntp, goff, tail_off, tail_len, gates_t, h2)

    row_map = lambda i, te, nu: (jnp.minimum(i, nu[0] - 1), 0)
    y_sorted = pl.pallas_call(
        _ffn_kernel,
        grid_spec=pltpu.PrefetchScalarGridSpec(
            num_scalar_prefetch=2, grid=(n_tiles_max,),
            in_specs=[pl.BlockSpec((FFN_TILE, D_MODEL), row_map),
                      pl.BlockSpec((1, D_MODEL, 2 * EXPERT_FF), lambda i, te, nu: (te[i], 0, 0)),
                      pl.BlockSpec((1, EXPERT_FF, D_MODEL), lambda i, te, nu: (te[i], 0, 0))],
            out_specs=pl.BlockSpec((FFN_TILE, D_MODEL), row_map)),
        out_shape=jax.ShapeDtypeStruct((rows_alloc, D_MODEL), BF16),
        compiler_params=params,
        name="moe_ffn",
    )(tile_expert, n_used, x_sorted, w["w_e_gu"], w["w_e_down"])

    full2 = lambda b, *_: (0, 0)
    return pl.pallas_call(
        functools.partial(_combine_kernel, n_blocks=n_blocks),
        grid_spec=pltpu.PrefetchScalarGridSpec(
            num_scalar_prefetch=2, grid=(n_blocks,),
            in_specs=[g_spec, h2_spec,
                      pl.BlockSpec((ns, ls, D_MODEL), x_map),
                      pl.BlockSpec((ns, MOD_ROWS, D_MODEL), seq_map3),
                      pl.BlockSpec((D_MODEL, 2 * SHARED_FF), full2),
                      pl.BlockSpec((SHARED_FF, D_MODEL), full2),
                      pl.BlockSpec((1, D_MODEL), full2),
                      any_spec],
            out_specs=pl.BlockSpec((ns, ls, D_MODEL), x_map),
            scratch_shapes=perm_scratch + [
                pltpu.VMEM((2, BLOCK_ROWS, D_MODEL), BF16),
                pltpu.SemaphoreType.DMA((2,))]),
        out_shape=jax.ShapeDtypeStruct(x1.shape, F32),
        compiler_params=params,
        name="moe_combine",
    )(cntp, goff, gates_t, h2, x1, mod, w["w_sh_gu"], w["w_sh_down"], w["g_post2"], y_sorted)


def _prep_weights(g_pre1, g_post1, w_in, w_gk_up, b_gk, g_gmlp_v, w_s, b_s, g_gla_o, w_out, g_pre2,
                  g_post2, w_router, b_router, w_e_gate, w_e_up, w_e_down, w_sh_gate, w_sh_up,
                  w_sh_down):
    row = lambda v: v.reshape(1, -1)
    return {
        "g_pre1": row(g_pre1), "g_post1": row(g_post1), "g_pre2": row(g_pre2), "g_post2": row(g_post2),
        "w_in": jnp.pad(w_in, ((0, 0), (0, GK_PAD - GLA_GATE_RANK))).astype(BF16),
        "w_gk_up": jnp.pad(w_gk_up, ((0, GK_PAD - GLA_GATE_RANK), (0, 0))).astype(BF16),
        "b_gk": row(b_gk),
        "g_gmlp_v": row(g_gmlp_v),
        "w_s": w_s,
        "b_s_full": jnp.repeat(b_s.T, GMLP_HEAD, axis=1),
        "g_gla_o": g_gla_o,
        "w_out": w_out.astype(BF16),
        "w_router": jnp.pad(w_router, ((0, 0), (0, LANES - N_EXPERTS))),
        "b_router": b_router.reshape(N_EXPERTS, 1),
        "w_e_gu": jnp.concatenate([w_e_gate, w_e_up], axis=-1).astype(BF16),
        "w_e_down": w_e_down.astype(BF16),
        "w_sh_gu": jnp.concatenate([w_sh_gate, w_sh_up], axis=-1).astype(BF16),
        "w_sh_down": w_sh_down.astype(BF16),
    }


PROMPT_TILE = 512
SAMPLE_TILE = 256


def _layer(x, mod, s0, w, *, emit_v):
    n_seq, seq_len, _ = x.shape
    s0_t = jnp.swapaxes(s0, -1, -2)
    if seq_len >= GMLP_CHUNK:
        outs = _mixer(x, mod, s0_t, w, seq_tile=PROMPT_TILE, chunk=GMLP_CHUNK, emit_v=emit_v)
    else:
        outs = _mixer(x, mod, s0_t, w, seq_tile=SAMPLE_TILE, chunk=seq_len, emit_v=emit_v)
    x1, h2, gates_t, st = outs[:4]
    y = _moe(h2, gates_t, x1, mod, w)
    v = outs[4].reshape(n_seq, seq_len, GMLP_GROUPS, GMLP_HEAD) if emit_v else None
    return y, jnp.swapaxes(st, -1, -2), v


def kernel(x_prompt, x_sample, state_gla, c_prompt, c_sample, w_ada, b_ada, g_pre1, g_post1, w_in, w_gk_up, b_gk, g_gmlp_v, w_s, b_s, g_gla_o, w_out, g_pre2, g_post2, w_router, b_router, w_e_gate, w_e_up, w_e_down, w_sh_gate, w_sh_up, w_sh_down):
    depth = w_ada.shape[0]
    n_p, n_s = x_prompt.shape[0], x_sample.shape[0]
    c_all = jnp.concatenate([c_prompt, c_sample], axis=0)
    x_p, x_s = x_prompt, x_sample
    sp_list, ss_list, vs_list = [], [], []
    for l in range(depth):
        w = _prep_weights(g_pre1[l], g_post1[l], w_in[l], w_gk_up[l], b_gk[l], g_gmlp_v[l], w_s[l],
                          b_s[l], g_gla_o[l], w_out[l], g_pre2[l], g_post2[l], w_router[l],
                          b_router[l], w_e_gate[l], w_e_up[l], w_e_down[l], w_sh_gate[l], w_sh_up[l],
                          w_sh_down[l])
        mod = _ada(c_all, w_ada[l], b_ada[l])
        mod = jnp.pad(jnp.swapaxes(mod, 0, 1), ((0, 0), (0, MOD_ROWS - 6), (0, 0)))
        s0_p = jnp.zeros((n_p, GLA_HEADS, GLA_DK, GLA_DV), F32)
        x_p, s_p, _ = _layer(x_p, mod[:n_p], s0_p, w, emit_v=False)
        x_s, s_s, v_s = _layer(x_s, mod[n_p:], state_gla[l], w, emit_v=True)
        sp_list.append(s_p)
        ss_list.append(s_s)
        vs_list.append(v_s)
    return (x_p, x_s, jnp.stack(sp_list), jnp.stack(ss_list), jnp.stack(vs_list))
```

```python
import functools

import jax
import jax.numpy as jnp
from jax import lax
from jax.experimental import pallas as pl
from jax.experimental.pallas import tpu as pltpu

D_MODEL = 1024
GMLP_WIDTH = 512
GMLP_GROUPS = 4
GMLP_HEAD = 128
GMLP_CHUNK = 128
CAUSAL_BLOCK = 64
GLA_WIDTH = 512
GLA_HEADS = 4
GLA_DV = 128
GLA_DK = 64
GLA_KEY_WIDTH = 256
GLA_GATE_RANK = 16
GLA_GATE_NORMALIZER = 16.0
N_EXPERTS = 64
TOP_K = 8
N_EXPERT_GROUPS = 8
GROUP_SIZE = N_EXPERTS // N_EXPERT_GROUPS
TOPK_GROUPS = 4
EXPERT_FF = 256
SHARED_FF = 256
ROUTED_SCALE = 2.5
EPS = 1e-6

LANES = 128
GK_PAD = LANES
IN_WIDTH_PAD = 2 * GMLP_WIDTH + 2 * GLA_KEY_WIDTH + 2 * GLA_WIDTH + GK_PAD
OFF_U = 0
OFF_VG = GMLP_WIDTH
OFF_Q = 2 * GMLP_WIDTH
OFF_K = OFF_Q + GLA_KEY_WIDTH
OFF_VL = OFF_K + GLA_KEY_WIDTH
OFF_R = OFF_VL + GLA_WIDTH
OFF_GK = OFF_R + GLA_WIDTH
MOD_ROWS = 8
VMEM_LIMIT = 56 * 1024 * 1024

F32 = jnp.float32
BF16 = jnp.bfloat16
NT_DIMS = (((1,), (1,)), ((), ()))
TN_DIMS = (((0,), (0,)), ((), ()))


def _rms(x, g):
    return x * lax.rsqrt(jnp.mean(x * x, axis=-1, keepdims=True) + EPS) * g


def _gelu(x):
    return 0.5 * x * (1.0 + jnp.tanh(0.7978845608028654 * (x + 0.044715 * (x * x * x))))


def _sigmoid(x):
    return 1.0 / (1.0 + jnp.exp(-x))


def _silu(x):
    return x * _sigmoid(x)


def _ada_kernel(c_ref, w_ref, b_ref, o_ref):
    a = _silu(c_ref[...])
    o_ref[0] = jnp.dot(a, w_ref[...], precision=lax.Precision.HIGHEST,
                       preferred_element_type=F32) + b_ref[0]


def _ada(c_all, w_ada, b_ada):
    n = c_all.shape[0]
    return pl.pallas_call(
        _ada_kernel,
        grid=(6,),
        in_specs=[pl.BlockSpec((n, D_MODEL), lambda j: (0, 0)),
                  pl.BlockSpec((D_MODEL, D_MODEL), lambda j: (0, j)),
                  pl.BlockSpec((1, 1, D_MODEL), lambda j: (j, 0, 0))],
        out_specs=pl.BlockSpec((1, n, D_MODEL), lambda j: (j, 0, 0)),
        out_shape=jax.ShapeDtypeStruct((6, n, D_MODEL), F32),
        compiler_params=pltpu.CompilerParams(vmem_limit_bytes=VMEM_LIMIT),
        name="ada",
    )(c_all, w_ada, b_ada.reshape(6, 1, D_MODEL))


def _route(logits_t, bias_t):
    t = logits_t.shape[1]
    scores = _sigmoid(logits_t)
    sel = scores + bias_t
    sub = lax.broadcasted_iota(jnp.int32, (GROUP_SIZE, t), 0)
    gscore = []
    for g in range(N_EXPERT_GROUPS):
        blk = sel[g * GROUP_SIZE:(g + 1) * GROUP_SIZE]
        m1 = jnp.max(blk, axis=0, keepdims=True)
        first = jnp.min(jnp.where(blk == m1, sub, GROUP_SIZE), axis=0, keepdims=True)
        m2 = jnp.max(jnp.where(sub == first, -jnp.inf, blk), axis=0, keepdims=True)
        gscore.append(m1 + m2)
    neg = jnp.full((GROUP_SIZE, t), -jnp.inf, F32)
    masked = []
    for g in range(N_EXPERT_GROUPS):
        rank = jnp.zeros((1, t), jnp.int32)
        for o in range(N_EXPERT_GROUPS):
            if o == g:
                continue
            ahead = (gscore[o] >= gscore[g]) if o < g else (gscore[o] > gscore[g])
            rank = rank + jnp.where(ahead, 1, 0)
        keep = jnp.broadcast_to(rank < TOPK_GROUPS, (GROUP_SIZE, t))
        masked.append(jnp.where(keep, sel[g * GROUP_SIZE:(g + 1) * GROUP_SIZE], neg))
    selm = jnp.concatenate(masked, axis=0)
    eidx = lax.broadcasted_iota(jnp.int32, (N_EXPERTS, t), 0)
    rank = jnp.zeros((N_EXPERTS, t), jnp.int32)
    for o in range(N_EXPERTS):
        so = selm[o:o + 1]
        rank = rank + jnp.where(eidx > o, jnp.where(so >= selm, 1, 0), jnp.where(so > selm, 1, 0))
    chosen = jnp.where(rank < TOP_K, scores, 0.0)
    denom = jnp.sum(chosen, axis=0, keepdims=True)
    return chosen * (ROUTED_SCALE / denom)


def _mixer_kernel(x_ref, mod_ref, s0_ref, gpre1_ref, win_ref, wgk_ref, bgk_ref, ggv_ref, ws_ref,
                  bsf_ref, ggo_ref, wout_ref, gpost1_ref, gpre2_ref, wr_ref, br_ref,
                  x1_ref, h2_ref, gates_ref, st_ref, *rest,
                  chunk, n_chunks, chunk_is_seq, emit_v):
    if emit_v:
        v_ref, proj_scr, mix_scr = rest
    else:
        proj_scr, mix_scr = rest
    ns, ls, _ = x_ref.shape
    tm = ns * ls

    mod_row = lambda i: mod_ref[:, i:i + 1, :]

    h = _rms(x_ref[...], gpre1_ref[...]) * (1.0 + mod_row(1)) + mod_row(0)
    proj_scr[...] = jnp.dot(h.reshape(tm, D_MODEL).astype(BF16), win_ref[...],
                            preferred_element_type=F32)

    if chunk_is_seq:
        st_ref[...] = s0_ref[...]
    else:
        @pl.when(pl.program_id(1) == 0)
        def _():
            st_ref[...] = s0_ref[...]

    row = lax.broadcasted_iota(jnp.int32, (chunk, chunk), 0)
    col = lax.broadcasted_iota(jnp.int32, (chunk, chunk), 1)
    tri = (row >= col).astype(F32)
    causal = row >= col
    block_causal = (row // CAUSAL_BLOCK) >= (col // CAUSAL_BLOCK)
    wmix = [jnp.where(block_causal, ws_ref[g, 0:chunk, 0:chunk], 0.0).astype(BF16)
            for g in range(GMLP_GROUPS)]
    mid = chunk // 2

    def chunk_body(c, carry):
        rows = pl.ds(pl.multiple_of(c * chunk, chunk), chunk)
        sidx = c if chunk_is_seq else 0

        vg = _gelu(proj_scr[rows, OFF_VG:OFF_VG + GMLP_WIDTH])
        vn = _rms(vg, ggv_ref[...])
        if emit_v:
            v_ref[rows, :] = vn
        vnb = vn.astype(BF16)
        mixed = jnp.concatenate(
            [jnp.dot(wmix[g], vnb[:, g * GMLP_HEAD:(g + 1) * GMLP_HEAD], preferred_element_type=F32)
             for g in range(GMLP_GROUPS)], axis=-1) + bsf_ref[0:chunk, :]
        u = _gelu(proj_scr[rows, OFF_U:OFF_U + GMLP_WIDTH])
        mix_scr[rows, 0:GMLP_WIDTH] = (u * mixed).astype(BF16)

        q = proj_scr[rows, OFF_Q:OFF_Q + GLA_KEY_WIDTH] * (GLA_DK ** -0.5)
        k = proj_scr[rows, OFF_K:OFF_K + GLA_KEY_WIDTH]
        gk = proj_scr[rows, OFF_GK:OFF_GK + GK_PAD]
        z = jnp.dot(gk.astype(BF16), wgk_ref[...], preferred_element_type=F32) + bgk_ref[...]
        la = (jnp.minimum(z, 0.0) - jnp.log(1.0 + jnp.exp(-jnp.abs(z)))) * (1.0 / GLA_GATE_NORMALIZER)
        b = jnp.dot(tri, la, precision=lax.Precision.HIGHEST, preferred_element_type=F32)
        b_mid = b[mid - 1:mid]
        b_end = b[chunk - 1:chunk]
        qt = (q * jnp.exp(b - b_mid)).astype(BF16)
        kt = (k * jnp.exp(b_mid - b)).astype(BF16)
        qs = (q * jnp.exp(b)).astype(BF16)
        kd = (k * jnp.exp(b_end - b)).astype(BF16)
        decay = jnp.exp(b_end)
        for hd in range(GLA_HEADS):
            ksl = slice(hd * GLA_DK, (hd + 1) * GLA_DK)
            vsl = slice(OFF_VL + hd * GLA_DV, OFF_VL + (hd + 1) * GLA_DV)
            rsl = slice(OFF_R + hd * GLA_DV, OFF_R + (hd + 1) * GLA_DV)
            att = lax.dot_general(qt[:, ksl], kt[:, ksl], NT_DIMS, preferred_element_type=F32)
            att = jnp.where(causal, att, 0.0).astype(BF16)
            vh = proj_scr[rows, vsl].astype(BF16)
            st = st_ref[sidx, hd]
            o = jnp.dot(att, vh, preferred_element_type=F32) + lax.dot_general(
                qs[:, ksl], st.astype(BF16), NT_DIMS, preferred_element_type=F32)
            on = _rms(o, ggo_ref[hd:hd + 1, :])
            mix_scr[rows, GMLP_WIDTH + hd * GLA_DV:GMLP_WIDTH + (hd + 1) * GLA_DV] = (
                on * _silu(proj_scr[rows, rsl])).astype(BF16)
            st_ref[sidx, hd] = st * decay[:, ksl] + lax.dot_general(
                vh, kd[:, ksl], TN_DIMS, preferred_element_type=F32)
        return carry

    lax.fori_loop(0, n_chunks, chunk_body, 0, unroll=2)

    mixo = jnp.dot(mix_scr[...], wout_ref[...], preferred_element_type=F32).reshape(ns, ls, D_MODEL)
    x1 = x_ref[...] + mod_row(2) * _rms(mixo, gpost1_ref[...])
    x1_ref[...] = x1
    h2 = (_rms(x1, gpre2_ref[...]) * (1.0 + mod_row(4)) + mod_row(3)).reshape(tm, D_MODEL)
    h2_hi = h2.astype(BF16)
    h2_ref[...] = h2_hi

    h2_lo = (h2 - h2_hi.astype(F32)).astype(BF16)
    parts = jnp.dot(jnp.concatenate([h2_hi, h2_lo], axis=-1), wr_ref[...], preferred_element_type=F32)
    logits = parts[:, 0:LANES] + parts[:, LANES:2 * LANES]
    gates_ref[...] = _route(logits.T[0:N_EXPERTS], br_ref[...])


def _mixer(x, mod, s0_t, w, *, seq_tile, chunk, emit_v):
    n_seq, seq_len, _ = x.shape
    chunk_is_seq = seq_len == chunk
    if chunk_is_seq:
        ns, ls = seq_tile // chunk, chunk
        grid = (1, n_seq // ns)
        x_map = lambda b, t: (t, 0, 0)
        seq_map3 = lambda b, t: (t, 0, 0)
        seq_map4 = lambda b, t: (t, 0, 0, 0)
    else:
        ns, ls = 1, seq_tile
        grid = (n_seq, seq_len // seq_tile)
        x_map = lambda b, t: (b, t, 0)
        seq_map3 = lambda b, t: (b, 0, 0)
        seq_map4 = lambda b, t: (b, 0, 0, 0)
    tm = ns * ls
    n_tok = n_seq * seq_len
    tok_map = lambda b, t: (b * grid[1] + t, 0)
    full2 = lambda b, t: (0, 0)
    full3 = lambda b, t: (0, 0, 0)

    in_specs = [
        pl.BlockSpec((ns, ls, D_MODEL), x_map),
        pl.BlockSpec((ns, MOD_ROWS, D_MODEL), seq_map3),
        pl.BlockSpec((ns, GLA_HEADS, GLA_DV, GLA_DK), seq_map4),
        pl.BlockSpec((1, D_MODEL), full2),
        pl.BlockSpec((D_MODEL, IN_WIDTH_PAD), full2),
        pl.BlockSpec((GK_PAD, GLA_KEY_WIDTH), full2),
        pl.BlockSpec((1, GLA_KEY_WIDTH), full2),
        pl.BlockSpec((1, GMLP_WIDTH), full2),
        pl.BlockSpec((GMLP_GROUPS, GMLP_CHUNK, GMLP_CHUNK), full3),
        pl.BlockSpec((GMLP_CHUNK, GMLP_WIDTH), full2),
        pl.BlockSpec((GLA_HEADS, GLA_DV), full2),
        pl.BlockSpec((D_MODEL, D_MODEL), full2),
        pl.BlockSpec((1, D_MODEL), full2),
        pl.BlockSpec((1, D_MODEL), full2),
        pl.BlockSpec((2 * D_MODEL, 2 * LANES), full2),
        pl.BlockSpec((N_EXPERTS, 1), full2),
    ]
    out_specs = [
        pl.BlockSpec((ns, ls, D_MODEL), x_map),
        pl.BlockSpec((tm, D_MODEL), tok_map),
        pl.BlockSpec((N_EXPERTS, tm), lambda b, t: (0, b * grid[1] + t)),
        pl.BlockSpec((ns, GLA_HEADS, GLA_DV, GLA_DK), seq_map4),
    ]
    out_shape = [
        jax.ShapeDtypeStruct((n_seq, seq_len, D_MODEL), F32),
        jax.ShapeDtypeStruct((n_tok, D_MODEL), BF16),
        jax.ShapeDtypeStruct((N_EXPERTS, n_tok), F32),
        jax.ShapeDtypeStruct((n_seq, GLA_HEADS, GLA_DV, GLA_DK), F32),
    ]
    if emit_v:
        out_specs.append(pl.BlockSpec((tm, GMLP_WIDTH), tok_map))
        out_shape.append(jax.ShapeDtypeStruct((n_tok, GMLP_WIDTH), F32))

    kern = functools.partial(_mixer_kernel, chunk=chunk, n_chunks=tm // chunk,
                             chunk_is_seq=chunk_is_seq, emit_v=emit_v)
    return pl.pallas_call(
        kern,
        grid=grid,
        in_specs=in_specs,
        out_specs=out_specs,
        out_shape=out_shape,
        scratch_shapes=[pltpu.VMEM((tm, IN_WIDTH_PAD), F32), pltpu.VMEM((tm, D_MODEL), BF16)],
        compiler_params=pltpu.CompilerParams(
            dimension_semantics=("arbitrary", "arbitrary"), vmem_limit_bytes=VMEM_LIMIT),
        name="mixer_sample" if chunk_is_seq else "mixer_prompt",
    )(x, mod, s0_t, w["g_pre1"], w["w_in"], w["w_gk_up"], w["b_gk"], w["g_gmlp_v"], w["w_s"],
      w["b_s_full"], w["g_gla_o"], w["w_out"], w["g_post1"], w["g_pre2"], w["w_router"],
      w["b_router"])


MOE_BLOCK = 256
SEG_ALIGN = 16
WIN = 64
FFN_TILE = 512
BLOCK_ROWS = MOE_BLOCK * TOP_K + SEG_ALIGN * N_EXPERTS + WIN
COL_CHUNK = 256
COUNT_LANES = LANES


def _count_kernel(g_ref, o_ref):
    n = g_ref.shape[1]
    selb = jnp.where(g_ref[...] > 0.0, 1.0, 0.0).astype(BF16)
    tok = lax.broadcasted_iota(jnp.int32, (n, COUNT_LANES), 0)
    lane = lax.broadcasted_iota(jnp.int32, (n, COUNT_LANES), 1)
    ind = jnp.where(tok // MOE_BLOCK == lane, 1.0, 0.0).astype(BF16)
    o_ref[0] = jnp.dot(selb, ind, preferred_element_type=F32)


def _fill_permutation(p_scr, rk_scr, g_ref, cntp_ref, loff_ref, long_ref, blk, weighted):
    tb = g_ref.shape[1]
    g = g_ref[...]
    sel = g > 0.0
    r_i = lax.broadcasted_iota(jnp.int32, (tb, tb), 0)
    c_i = lax.broadcasted_iota(jnp.int32, (tb, tb), 1)
    earlier = jnp.where(r_i < c_i, 1.0, 0.0).astype(BF16)
    rank = jnp.dot(jnp.where(sel, 1.0, 0.0).astype(BF16), earlier, preferred_element_type=F32)
    rk = jnp.where(sel, rank, -1.0)
    p_scr[...] = jnp.zeros_like(p_scr)
    jwin = lax.broadcasted_iota(jnp.int32, (WIN, tb), 0).astype(F32)

    for e in range(N_EXPERTS):
        off = pl.multiple_of(loff_ref[blk * N_EXPERTS + e], SEG_ALIGN)
        val = g[e:e + 1] if weighted else 1.0
        p_scr[pl.ds(off, WIN), :] = jnp.where(rk[e:e + 1] == jwin, val, 0.0).astype(BF16)

    @pl.when(long_ref[blk] > 0)
    def _():
        rk_scr[...] = rk

        def expert_body(e, c):
            n = cntp_ref[blk * N_EXPERTS + e]
            off = loff_ref[blk * N_EXPERTS + e]
            rke = rk_scr[pl.ds(e, 1), :]
            val = g_ref[pl.ds(e, 1), :] if weighted else 1.0

            def win_body(wi, c2):
                rows = pl.ds(pl.multiple_of(off + wi * WIN, SEG_ALIGN), WIN)
                j = jwin + (wi * WIN).astype(F32)
                new = jnp.where(rke == j, val, 0.0)
                p_scr[rows, :] = jnp.where(j < n.astype(F32), new, p_scr[rows, :].astype(F32)).astype(BF16)
                return c2

            lax.fori_loop(1, (n + WIN - 1) // WIN, win_body, 0)
            return c

        lax.fori_loop(0, N_EXPERTS, expert_body, 0)


def _start_segments(cntp_ref, loff_ref, blk, make_copy):
    for e in range(N_EXPERTS):
        make_copy(e, pl.multiple_of(loff_ref[blk * N_EXPERTS + e], SEG_ALIGN),
                  pl.multiple_of(cntp_ref[blk * N_EXPERTS + e], SEG_ALIGN)).start()


def _dispatch_kernel(cntp_ref, loff_ref, goff_ref, rows_ref, long_ref, toff_ref, tlen_ref, g_ref, h2_ref,
                     xs_hbm, p_scr, rk_scr, xs_scr, zero_scr, sems, tail_sem, *, n_blocks):
    b = pl.program_id(0)
    slot = lax.rem(b, 2)

    def seg_copy(blk, s):
        def make(e, off, n):
            dst = pl.multiple_of(goff_ref[blk * N_EXPERTS + e], SEG_ALIGN)
            return pltpu.make_async_copy(xs_scr.at[s, pl.ds(off, n)], xs_hbm.at[pl.ds(dst, n)], sems.at[s])
        return make

    def wait_block(blk, s):
        n = pl.multiple_of(rows_ref[blk], SEG_ALIGN)
        pltpu.make_async_copy(xs_scr.at[s, pl.ds(0, n)], xs_hbm.at[pl.ds(0, n)], sems.at[s]).wait()

    def for_tails(action):
        def body(e, c):
            n = tlen_ref[e]

            @pl.when(n > 0)
            def _():
                dst = pl.multiple_of(toff_ref[e], SEG_ALIGN)
                nn = pl.multiple_of(n, SEG_ALIGN)
                action(pltpu.make_async_copy(zero_scr.at[pl.ds(0, nn)], xs_hbm.at[pl.ds(dst, nn)], tail_sem))

            return c

        lax.fori_loop(0, N_EXPERTS, body, 0)

    @pl.when(b == 0)
    def _():
        zero_scr[...] = jnp.zeros_like(zero_scr)
        for_tails(lambda cp: cp.start())

    @pl.when(b >= 2)
    def _():
        wait_block(b - 2, slot)

    _fill_permutation(p_scr, rk_scr, g_ref, cntp_ref, loff_ref, long_ref, b, weighted=False)
    for c in range(D_MODEL // COL_CHUNK):
        cols = slice(c * COL_CHUNK, (c + 1) * COL_CHUNK)
        xs = jnp.dot(p_scr[...], h2_ref[:, cols], preferred_element_type=F32)
        xs_scr[slot, :, cols] = xs.astype(BF16)
    _start_segments(cntp_ref, loff_ref, b, seg_copy(b, slot))

    @pl.when(b == n_blocks - 1)
    def _():
        for_tails(lambda cp: cp.wait())
        if n_blocks >= 2:
            wait_block(b - 1, 1 - slot)
        wait_block(b, slot)


def _ffn_kernel(te_ref, nu_ref, x_ref, wgu_ref, wd_ref, y_ref):
    @pl.when(pl.program_id(0) < nu_ref[0])
    def _():
        gu = jnp.dot(x_ref[...], wgu_ref[0], preferred_element_type=F32)
        hid = _silu(gu[:, 0:EXPERT_FF]) * gu[:, EXPERT_FF:2 * EXPERT_FF]
        y_ref[...] = jnp.dot(hid.astype(BF16), wd_ref[0], preferred_element_type=F32).astype(BF16)


def _combine_kernel(cntp_ref, loff_ref, goff_ref, rows_ref, long_ref, g_ref, h2_ref, x1_ref, mod_ref,
                    wsgu_ref, wsd_ref, gpost2_ref, ys_hbm, out_ref, p_scr, rk_scr, ys_scr, sems,
                    *, n_blocks):
    b = pl.program_id(0)
    slot = lax.rem(b, 2)
    ns, ls, _ = x1_ref.shape

    def seg_copy(blk, s):
        def make(e, off, n):
            src = pl.multiple_of(goff_ref[blk * N_EXPERTS + e], SEG_ALIGN)
            return pltpu.make_async_copy(ys_hbm.at[pl.ds(src, n)], ys_scr.at[s, pl.ds(off, n)], sems.at[s])
        return make

    @pl.when(b == 0)
    def _():
        ys_scr[...] = jnp.zeros_like(ys_scr)
        _start_segments(cntp_ref, loff_ref, 0, seg_copy(0, 0))

    def wait_block(blk, s):
        n = pl.multiple_of(rows_ref[blk], SEG_ALIGN)
        pltpu.make_async_copy(ys_hbm.at[pl.ds(0, n)], ys_scr.at[s, pl.ds(0, n)], sems.at[s]).wait()

    nxt = jnp.minimum(b + 1, n_blocks - 1)
    _start_segments(cntp_ref, loff_ref, nxt, seg_copy(nxt, 1 - slot))

    _fill_permutation(p_scr, rk_scr, g_ref, cntp_ref, loff_ref, long_ref, b, weighted=True)
    wait_block(b, slot)

    @pl.when(b == n_blocks - 1)
    def _():
        wait_block(b, 1 - slot)

    routed = lax.dot_general(p_scr[...], ys_scr[slot], TN_DIMS, preferred_element_type=F32)
    hx = h2_ref[...]
    sgu = jnp.dot(hx, wsgu_ref[...], preferred_element_type=F32)
    shid = _silu(sgu[:, 0:SHARED_FF]) * sgu[:, SHARED_FF:2 * SHARED_FF]
    f = routed + jnp.dot(shid.astype(BF16), wsd_ref[...], preferred_element_type=F32)
    gate2 = mod_ref[...][:, 5:6]
    out_ref[...] = x1_ref[...] + gate2 * _rms(f, gpost2_ref[...]).reshape(ns, ls, D_MODEL)


def _segment_plan(cnt, n_tiles_max):
    cntp = jnp.maximum((cnt + SEG_ALIGN - 1) // SEG_ALIGN, 1) * SEG_ALIGN
    loff = jnp.cumsum(cntp, axis=1) - cntp
    block_rows = jnp.sum(cntp, axis=1)
    has_long = jnp.max(cntp, axis=1) > WIN
    tot = jnp.sum(cntp, axis=0)
    reg = (tot + FFN_TILE - 1) // FFN_TILE * FFN_TILE
    base = jnp.cumsum(reg) - reg
    goff = base[None, :] + jnp.cumsum(cntp, axis=0) - cntp
    tile_end = jnp.cumsum(reg // FFN_TILE)
    n_used = tile_end[-1:]
    tile = jnp.minimum(jnp.arange(n_tiles_max, dtype=jnp.int32), n_used - 1)
    tile_expert = jnp.minimum(jnp.sum(tile_end[None, :] <= tile[:, None], axis=1), N_EXPERTS - 1)
    i32 = lambda a: a.astype(jnp.int32)
    block_tables = (i32(cntp.reshape(-1)), i32(loff.reshape(-1)), i32(goff.reshape(-1)),
                    i32(block_rows), i32(has_long))
    return block_tables, i32(base + tot), i32(reg - tot), i32(tile_expert), i32(n_used)


def _moe(h2, gates_t, x1, mod, w):
    n_seq, seq_len, _ = x1.shape
    n_tok = n_seq * seq_len
    n_blocks = n_tok // MOE_BLOCK
    if seq_len < MOE_BLOCK:
        ns, ls = MOE_BLOCK // seq_len, seq_len
        x_map = lambda b, *_: (b, 0, 0)
        seq_map3 = lambda b, *_: (b, 0, 0)
    else:
        ns, ls = 1, MOE_BLOCK
        per_seq = seq_len // MOE_BLOCK
        x_map = lambda b, *_: (b // per_seq, b % per_seq, 0)
        seq_map3 = lambda b, *_: (b // per_seq, 0, 0)
    rows_max = (n_tok * TOP_K + n_blocks * N_EXPERTS * SEG_ALIGN
                + N_EXPERTS * (FFN_TILE - SEG_ALIGN))
    n_tiles_max = -(-rows_max // FFN_TILE)
    rows_alloc = n_tiles_max * FFN_TILE
    params = pltpu.CompilerParams(dimension_semantics=("arbitrary",), vmem_limit_bytes=VMEM_LIMIT)

    chunk = min(n_tok, COUNT_LANES * MOE_BLOCK // 16)
    per_chunk = chunk // MOE_BLOCK
    cnt = pl.pallas_call(
        _count_kernel,
        grid=(n_tok // chunk,),
        in_specs=[pl.BlockSpec((N_EXPERTS, chunk), lambda i: (0, i))],
        out_specs=pl.BlockSpec((1, N_EXPERTS, COUNT_LANES), lambda i: (i, 0, 0)),
        out_shape=jax.ShapeDtypeStruct((n_tok // chunk, N_EXPERTS, COUNT_LANES), F32),
        compiler_params=params,
        name="moe_count",
    )(gates_t)
    cnt = jnp.swapaxes(cnt[:, :, :per_chunk], 1, 2).reshape(n_blocks, N_EXPERTS).astype(jnp.int32)
    block_tables, tail_off, tail_len, tile_expert, n_used = _segment_plan(cnt, n_tiles_max)

    g_spec = pl.BlockSpec((N_EXPERTS, MOE_BLOCK), lambda b, *_: (0, b))
    h2_spec = pl.BlockSpec((MOE_BLOCK, D_MODEL), lambda b, *_: (b, 0))
    any_spec = pl.BlockSpec(memory_space=pl.ANY)
    perm_scratch = [pltpu.VMEM((BLOCK_ROWS, MOE_BLOCK), BF16), pltpu.VMEM((N_EXPERTS, MOE_BLOCK), F32)]

    x_sorted = pl.pallas_call(
        functools.partial(_dispatch_kernel, n_blocks=n_blocks),
        grid_spec=pltpu.PrefetchScalarGridSpec(
            num_scalar_prefetch=7, grid=(n_blocks,),
            in_specs=[g_spec, h2_spec], out_specs=any_spec,
            scratch_shapes=perm_scratch + [
                pltpu.VMEM((2, BLOCK_ROWS, D_MODEL), BF16),
                pltpu.VMEM((FFN_TILE, D_MODEL), BF16),
                pltpu.SemaphoreType.DMA((2,)), pltpu.SemaphoreType.DMA(())]),
        out_shape=jax.ShapeDtypeStruct((rows_alloc, D_MODEL), BF16),
        compiler_params=params,
        name="moe_dispatch",
    )(*block_tables, tail_off, tail_len, gates_t, h2)

    row_map = lambda i, te, nu: (jnp.maximum(jnp.minimum(i, nu[0] - 1), 0), 0)
    y_sorted = pl.pallas_call(
        _ffn_kernel,
        grid_spec=pltpu.PrefetchScalarGridSpec(
            num_scalar_prefetch=2, grid=(n_tiles_max,),
            in_specs=[pl.BlockSpec((FFN_TILE, D_MODEL), row_map),
                      pl.BlockSpec((1, D_MODEL, 2 * EXPERT_FF), lambda i, te, nu: (te[i], 0, 0)),
                      pl.BlockSpec((1, EXPERT_FF, D_MODEL), lambda i, te, nu: (te[i], 0, 0))],
            out_specs=pl.BlockSpec((FFN_TILE, D_MODEL), row_map)),
        out_shape=jax.ShapeDtypeStruct((rows_alloc, D_MODEL), BF16),
        compiler_params=params,
        name="moe_ffn",
    )(tile_expert, n_used, x_sorted, w["w_e_gu"], w["w_e_down"])

    full2 = lambda b, *_: (0, 0)
    return pl.pallas_call(
        functools.partial(_combine_kernel, n_blocks=n_blocks),
        grid_spec=pltpu.PrefetchScalarGridSpec(
            num_scalar_prefetch=5, grid=(n_blocks,),
            in_specs=[g_spec, h2_spec,
                      pl.BlockSpec((ns, ls, D_MODEL), x_map),
                      pl.BlockSpec((ns, MOD_ROWS, D_MODEL), seq_map3),
                      pl.BlockSpec((D_MODEL, 2 * SHARED_FF), full2),
                      pl.BlockSpec((SHARED_FF, D_MODEL), full2),
                      pl.BlockSpec((1, D_MODEL), full2),
                      any_spec],
            out_specs=pl.BlockSpec((ns, ls, D_MODEL), x_map),
            scratch_shapes=perm_scratch + [
                pltpu.VMEM((2, BLOCK_ROWS, D_MODEL), BF16),
                pltpu.SemaphoreType.DMA((2,))]),
        out_shape=jax.ShapeDtypeStruct(x1.shape, F32),
        compiler_params=params,
        name="moe_combine",
    )(*block_tables, gates_t, h2, x1, mod, w["w_sh_gu"], w["w_sh_down"], w["g_post2"], y_sorted)


def _prep_weights(g_pre1, g_post1, w_in, w_gk_up, b_gk, g_gmlp_v, w_s, b_s, g_gla_o, w_out, g_pre2,
                  g_post2, w_router, b_router, w_e_gate, w_e_up, w_e_down, w_sh_gate, w_sh_up,
                  w_sh_down):
    row = lambda v: v.reshape(1, -1)
    wr = jnp.pad(w_router, ((0, 0), (0, LANES - N_EXPERTS)))
    wr_top = lax.bitcast_convert_type(
        lax.bitcast_convert_type(wr, jnp.uint32) & jnp.uint32(0xFFFF0000), F32)
    wr_hi = wr_top.astype(BF16)
    wr_lo = (wr - wr_top).astype(BF16)
    wr_split = jnp.concatenate([jnp.concatenate([wr_hi, wr_lo], axis=1),
                                jnp.concatenate([wr_hi, jnp.zeros_like(wr_lo)], axis=1)], axis=0)
    return {
        "g_pre1": row(g_pre1), "g_post1": row(g_post1), "g_pre2": row(g_pre2), "g_post2": row(g_post2),
        "w_in": jnp.pad(w_in, ((0, 0), (0, GK_PAD - GLA_GATE_RANK))).astype(BF16),
        "w_gk_up": jnp.pad(w_gk_up, ((0, GK_PAD - GLA_GATE_RANK), (0, 0))).astype(BF16),
        "b_gk": row(b_gk),
        "g_gmlp_v": row(g_gmlp_v),
        "w_s": w_s,
        "b_s_full": jnp.repeat(b_s.T, GMLP_HEAD, axis=1),
        "g_gla_o": g_gla_o,
        "w_out": w_out.astype(BF16),
        "w_router": wr_split,
        "b_router": b_router.reshape(N_EXPERTS, 1),
        "w_e_gu": jnp.concatenate([w_e_gate, w_e_up], axis=-1).astype(BF16),
        "w_e_down": w_e_down.astype(BF16),
        "w_sh_gu": jnp.concatenate([w_sh_gate, w_sh_up], axis=-1).astype(BF16),
        "w_sh_down": w_sh_down.astype(BF16),
    }


PROMPT_TILE = 512
SAMPLE_TILE = 256


def _layer(x, mod, s0, w, *, emit_v):
    n_seq, seq_len, _ = x.shape
    s0_t = jnp.swapaxes(s0, -1, -2)
    if seq_len >= GMLP_CHUNK:
        outs = _mixer(x, mod, s0_t, w, seq_tile=PROMPT_TILE, chunk=GMLP_CHUNK, emit_v=emit_v)
    else:
        outs = _mixer(x, mod, s0_t, w, seq_tile=SAMPLE_TILE, chunk=seq_len, emit_v=emit_v)
    x1, h2, gates_t, st = outs[:4]
    y = _moe(h2, gates_t, x1, mod, w)
    v = outs[4].reshape(n_seq, seq_len, GMLP_GROUPS, GMLP_HEAD) if emit_v else None
    return y, jnp.swapaxes(st, -1, -2), v


def kernel(x_prompt, x_sample, state_gla, c_prompt, c_sample, w_ada, b_ada, g_pre1, g_post1, w_in, w_gk_up, b_gk, g_gmlp_v, w_s, b_s, g_gla_o, w_out, g_pre2, g_post2, w_router, b_router, w_e_gate, w_e_up, w_e_down, w_sh_gate, w_sh_up, w_sh_down):
    depth = w_ada.shape[0]
    n_p, n_s = x_prompt.shape[0], x_sample.shape[0]
    c_all = jnp.concatenate([c_prompt, c_sample], axis=0)
    x_p, x_s = x_prompt, x_sample
    sp_list, ss_list, vs_list = [], [], []
    for l in range(depth):
        w = _prep_weights(g_pre1[l], g_post1[l], w_in[l], w_gk_up[l], b_gk[l], g_gmlp_v[l], w_s[l],
                          b_s[l], g_gla_o[l], w_out[l], g_pre2[l], g_post2[l], w_router[l],
                          b_router[l], w_e_gate[l], w_e_up[l], w_e_down[l], w_sh_gate[l], w_sh_up[l],
                          w_sh_down[l])
        mod = _ada(c_all, w_ada[l], b_ada[l])
        mod = jnp.pad(jnp.swapaxes(mod, 0, 1), ((0, 0), (0, MOD_ROWS - 6), (0, 0)))
        s0_p = jnp.zeros((n_p, GLA_HEADS, GLA_DK, GLA_DV), F32)
        x_p, s_p, _ = _layer(x_p, mod[:n_p], s0_p, w, emit_v=False)
        x_s, s_s, v_s = _layer(x_s, mod[n_p:], state_gla[l], w, emit_v=True)
        sp_list.append(s_p)
        ss_list.append(s_s)
        vs_list.append(v_s)
    return (x_p, x_s, jnp.stack(sp_list), jnp.stack(ss_list), jnp.stack(vs_list))
```

```python
import functools

import jax
import jax.numpy as jnp
from jax import lax
from jax.experimental import pallas as pl
from jax.experimental.pallas import tpu as pltpu

D_MODEL = 1024
GMLP_WIDTH = 512
GMLP_GROUPS = 4
GMLP_HEAD = 128
GMLP_CHUNK = 128
CAUSAL_BLOCK = 64
GLA_WIDTH = 512
GLA_HEADS = 4
GLA_DV = 128
GLA_DK = 64
GLA_KEY_WIDTH = 256
GLA_GATE_RANK = 16
GLA_GATE_NORMALIZER = 16.0
N_EXPERTS = 64
TOP_K = 8
N_EXPERT_GROUPS = 8
GROUP_SIZE = N_EXPERTS // N_EXPERT_GROUPS
TOPK_GROUPS = 4
EXPERT_FF = 256
SHARED_FF = 256
ROUTED_SCALE = 2.5
EPS = 1e-6

LANES = 128
GK_PAD = LANES
IN_WIDTH_PAD = 2 * GMLP_WIDTH + 2 * GLA_KEY_WIDTH + 2 * GLA_WIDTH + GK_PAD
OFF_U = 0
OFF_VG = GMLP_WIDTH
OFF_Q = 2 * GMLP_WIDTH
OFF_K = OFF_Q + GLA_KEY_WIDTH
OFF_VL = OFF_K + GLA_KEY_WIDTH
OFF_R = OFF_VL + GLA_WIDTH
OFF_GK = OFF_R + GLA_WIDTH
MOD_ROWS = 8
VMEM_LIMIT = 56 * 1024 * 1024

F32 = jnp.float32
BF16 = jnp.bfloat16
NT_DIMS = (((1,), (1,)), ((), ()))
TN_DIMS = (((0,), (0,)), ((), ()))


def _rms(x, g):
    return x * lax.rsqrt(jnp.mean(x * x, axis=-1, keepdims=True) + EPS) * g


def _gelu(x):
    return 0.5 * x * (1.0 + jnp.tanh(0.7978845608028654 * (x + 0.044715 * (x * x * x))))


def _sigmoid(x):
    return 1.0 / (1.0 + jnp.exp(-x))


def _silu(x):
    return x * _sigmoid(x)


def _ada_kernel(c_ref, w_ref, b_ref, o_ref):
    a = _silu(c_ref[...])
    o_ref[0] = jnp.dot(a, w_ref[...], precision=lax.Precision.HIGHEST,
                       preferred_element_type=F32) + b_ref[0]


def _ada(c_all, w_ada, b_ada):
    n = c_all.shape[0]
    return pl.pallas_call(
        _ada_kernel,
        grid=(6,),
        in_specs=[pl.BlockSpec((n, D_MODEL), lambda j: (0, 0)),
                  pl.BlockSpec((D_MODEL, D_MODEL), lambda j: (0, j)),
                  pl.BlockSpec((1, 1, D_MODEL), lambda j: (j, 0, 0))],
        out_specs=pl.BlockSpec((1, n, D_MODEL), lambda j: (j, 0, 0)),
        out_shape=jax.ShapeDtypeStruct((6, n, D_MODEL), F32),
        compiler_params=pltpu.CompilerParams(vmem_limit_bytes=VMEM_LIMIT),
        name="ada",
    )(c_all, w_ada, b_ada.reshape(6, 1, D_MODEL))


def _route(logits_t, bias_t):
    t = logits_t.shape[1]
    scores = _sigmoid(logits_t)
    sel = scores + bias_t
    sub = lax.broadcasted_iota(jnp.int32, (GROUP_SIZE, t), 0)
    gscore = []
    for g in range(N_EXPERT_GROUPS):
        blk = sel[g * GROUP_SIZE:(g + 1) * GROUP_SIZE]
        m1 = jnp.max(blk, axis=0, keepdims=True)
        first = jnp.min(jnp.where(blk == m1, sub, GROUP_SIZE), axis=0, keepdims=True)
        m2 = jnp.max(jnp.where(sub == first, -jnp.inf, blk), axis=0, keepdims=True)
        gscore.append(m1 + m2)
    neg = jnp.full((GROUP_SIZE, t), -jnp.inf, F32)
    masked = []
    for g in range(N_EXPERT_GROUPS):
        rank = jnp.zeros((1, t), jnp.int32)
        for o in range(N_EXPERT_GROUPS):
            if o == g:
                continue
            ahead = (gscore[o] >= gscore[g]) if o < g else (gscore[o] > gscore[g])
            rank = rank + jnp.where(ahead, 1, 0)
        keep = jnp.broadcast_to(rank < TOPK_GROUPS, (GROUP_SIZE, t))
        masked.append(jnp.where(keep, sel[g * GROUP_SIZE:(g + 1) * GROUP_SIZE], neg))
    selm = jnp.concatenate(masked, axis=0)
    eidx = lax.broadcasted_iota(jnp.int32, (N_EXPERTS, t), 0)
    rank = jnp.zeros((N_EXPERTS, t), jnp.int32)
    for o in range(N_EXPERTS):
        so = selm[o:o + 1]
        rank = rank + jnp.where(eidx > o, jnp.where(so >= selm, 1, 0), jnp.where(so > selm, 1, 0))
    chosen = jnp.where(rank < TOP_K, scores, 0.0)
    denom = jnp.sum(chosen, axis=0, keepdims=True)
    return chosen * (ROUTED_SCALE / denom)


def _mixer_kernel(x_ref, mod_ref, s0_ref, gpre1_ref, win_ref, wgk_ref, bgk_ref, ggv_ref, ws_ref,
                  bsf_ref, ggo_ref, wout_ref, gpost1_ref, gpre2_ref, wr_ref, br_ref,
                  x1_ref, h2_ref, gates_ref, st_ref, *rest,
                  chunk, n_chunks, chunk_is_seq, emit_v):
    if emit_v:
        v_ref, proj_scr, mix_scr = rest
    else:
        proj_scr, mix_scr = rest
    ns, ls, _ = x_ref.shape
    tm = ns * ls

    mod_row = lambda i: mod_ref[:, i:i + 1, :]

    h = _rms(x_ref[...], gpre1_ref[...]) * (1.0 + mod_row(1)) + mod_row(0)
    proj_scr[...] = jnp.dot(h.reshape(tm, D_MODEL).astype(BF16), win_ref[...],
                            preferred_element_type=F32)

    if chunk_is_seq:
        st_ref[...] = s0_ref[...]
    else:
        @pl.when(pl.program_id(1) == 0)
        def _():
            st_ref[...] = s0_ref[...]

    row = lax.broadcasted_iota(jnp.int32, (chunk, chunk), 0)
    col = lax.broadcasted_iota(jnp.int32, (chunk, chunk), 1)
    tri = (row >= col).astype(F32)
    causal = row >= col
    block_causal = (row // CAUSAL_BLOCK) >= (col // CAUSAL_BLOCK)
    wmix = [jnp.where(block_causal, ws_ref[g, 0:chunk, 0:chunk], 0.0).astype(BF16)
            for g in range(GMLP_GROUPS)]
    mid = chunk // 2

    def chunk_body(c, carry):
        rows = pl.ds(pl.multiple_of(c * chunk, chunk), chunk)
        sidx = c if chunk_is_seq else 0

        vg = _gelu(proj_scr[rows, OFF_VG:OFF_VG + GMLP_WIDTH])
        vn = _rms(vg, ggv_ref[...])
        if emit_v:
            v_ref[rows, :] = vn
        vnb = vn.astype(BF16)
        mixed = jnp.concatenate(
            [jnp.dot(wmix[g], vnb[:, g * GMLP_HEAD:(g + 1) * GMLP_HEAD], preferred_element_type=F32)
             for g in range(GMLP_GROUPS)], axis=-1) + bsf_ref[0:chunk, :]
        u = _gelu(proj_scr[rows, OFF_U:OFF_U + GMLP_WIDTH])
        mix_scr[rows, 0:GMLP_WIDTH] = (u * mixed).astype(BF16)

        q = proj_scr[rows, OFF_Q:OFF_Q + GLA_KEY_WIDTH] * (GLA_DK ** -0.5)
        k = proj_scr[rows, OFF_K:OFF_K + GLA_KEY_WIDTH]
        gk = proj_scr[rows, OFF_GK:OFF_GK + GK_PAD]
        z = jnp.dot(gk.astype(BF16), wgk_ref[...], preferred_element_type=F32) + bgk_ref[...]
        la = (jnp.minimum(z, 0.0) - jnp.log(1.0 + jnp.exp(-jnp.abs(z)))) * (1.0 / GLA_GATE_NORMALIZER)
        b = jnp.dot(tri, la, precision=lax.Precision.HIGHEST, preferred_element_type=F32)
        b_mid = b[mid - 1:mid]
        b_end = b[chunk - 1:chunk]
        qt = (q * jnp.exp(b - b_mid)).astype(BF16)
        kt = (k * jnp.exp(b_mid - b)).astype(BF16)
        qs = (q * jnp.exp(b)).astype(BF16)
        kd = (k * jnp.exp(b_end - b)).astype(BF16)
        decay = jnp.exp(b_end)
        for hd in range(GLA_HEADS):
            ksl = slice(hd * GLA_DK, (hd + 1) * GLA_DK)
            vsl = slice(OFF_VL + hd * GLA_DV, OFF_VL + (hd + 1) * GLA_DV)
            rsl = slice(OFF_R + hd * GLA_DV, OFF_R + (hd + 1) * GLA_DV)
            att = lax.dot_general(qt[:, ksl], kt[:, ksl], NT_DIMS, preferred_element_type=F32)
            att = jnp.where(causal, att, 0.0).astype(BF16)
            vh = proj_scr[rows, vsl].astype(BF16)
            st = st_ref[sidx, hd]
            o = jnp.dot(att, vh, preferred_element_type=F32) + lax.dot_general(
                qs[:, ksl], st.astype(BF16), NT_DIMS, preferred_element_type=F32)
            on = _rms(o, ggo_ref[hd:hd + 1, :])
            mix_scr[rows, GMLP_WIDTH + hd * GLA_DV:GMLP_WIDTH + (hd + 1) * GLA_DV] = (
                on * _silu(proj_scr[rows, rsl])).astype(BF16)
            st_ref[sidx, hd] = st * decay[:, ksl] + lax.dot_general(
                vh, kd[:, ksl], TN_DIMS, preferred_element_type=F32)
        return carry

    lax.fori_loop(0, n_chunks, chunk_body, 0, unroll=4)

    mixo = jnp.dot(mix_scr[...], wout_ref[...], preferred_element_type=F32).reshape(ns, ls, D_MODEL)
    x1 = x_ref[...] + mod_row(2) * _rms(mixo, gpost1_ref[...])
    x1_ref[...] = x1
    h2 = (_rms(x1, gpre2_ref[...]) * (1.0 + mod_row(4)) + mod_row(3)).reshape(tm, D_MODEL)
    h2_hi = h2.astype(BF16)
    h2_ref[...] = h2_hi

    h2_lo = (h2 - h2_hi.astype(F32)).astype(BF16)
    parts = jnp.dot(jnp.concatenate([h2_hi, h2_lo], axis=-1), wr_ref[...], preferred_element_type=F32)
    logits = parts[:, 0:LANES] + parts[:, LANES:2 * LANES]
    gates_ref[...] = _route(logits.T[0:N_EXPERTS], br_ref[...])


def _mixer(x, mod, s0_t, w, *, seq_tile, chunk, emit_v):
    n_seq, seq_len, _ = x.shape
    chunk_is_seq = seq_len == chunk
    if chunk_is_seq:
        ns, ls = seq_tile // chunk, chunk
        grid = (1, n_seq // ns)
        x_map = lambda b, t: (t, 0, 0)
        seq_map3 = lambda b, t: (t, 0, 0)
        seq_map4 = lambda b, t: (t, 0, 0, 0)
    else:
        ns, ls = 1, seq_tile
        grid = (n_seq, seq_len // seq_tile)
        x_map = lambda b, t: (b, t, 0)
        seq_map3 = lambda b, t: (b, 0, 0)
        seq_map4 = lambda b, t: (b, 0, 0, 0)
    tm = ns * ls
    n_tok = n_seq * seq_len
    tok_map = lambda b, t: (b * grid[1] + t, 0)
    full2 = lambda b, t: (0, 0)
    full3 = lambda b, t: (0, 0, 0)

    in_specs = [
        pl.BlockSpec((ns, ls, D_MODEL), x_map),
        pl.BlockSpec((ns, MOD_ROWS, D_MODEL), seq_map3),
        pl.BlockSpec((ns, GLA_HEADS, GLA_DV, GLA_DK), seq_map4),
        pl.BlockSpec((1, D_MODEL), full2),
        pl.BlockSpec((D_MODEL, IN_WIDTH_PAD), full2),
        pl.BlockSpec((GK_PAD, GLA_KEY_WIDTH), full2),
        pl.BlockSpec((1, GLA_KEY_WIDTH), full2),
        pl.BlockSpec((1, GMLP_WIDTH), full2),
        pl.BlockSpec((GMLP_GROUPS, GMLP_CHUNK, GMLP_CHUNK), full3),
        pl.BlockSpec((GMLP_CHUNK, GMLP_WIDTH), full2),
        pl.BlockSpec((GLA_HEADS, GLA_DV), full2),
        pl.BlockSpec((D_MODEL, D_MODEL), full2),
        pl.BlockSpec((1, D_MODEL), full2),
        pl.BlockSpec((1, D_MODEL), full2),
        pl.BlockSpec((2 * D_MODEL, 2 * LANES), full2),
        pl.BlockSpec((N_EXPERTS, 1), full2),
    ]
    out_specs = [
        pl.BlockSpec((ns, ls, D_MODEL), x_map),
        pl.BlockSpec((tm, D_MODEL), tok_map),
        pl.BlockSpec((N_EXPERTS, tm), lambda b, t: (0, b * grid[1] + t)),
        pl.BlockSpec((ns, GLA_HEADS, GLA_DV, GLA_DK), seq_map4),
    ]
    out_shape = [
        jax.ShapeDtypeStruct((n_seq, seq_len, D_MODEL), F32),
        jax.ShapeDtypeStruct((n_tok, D_MODEL), BF16),
        jax.ShapeDtypeStruct((N_EXPERTS, n_tok), F32),
        jax.ShapeDtypeStruct((n_seq, GLA_HEADS, GLA_DV, GLA_DK), F32),
    ]
    if emit_v:
        out_specs.append(pl.BlockSpec((tm, GMLP_WIDTH), tok_map))
        out_shape.append(jax.ShapeDtypeStruct((n_tok, GMLP_WIDTH), F32))

    kern = functools.partial(_mixer_kernel, chunk=chunk, n_chunks=tm // chunk,
                             chunk_is_seq=chunk_is_seq, emit_v=emit_v)
    return pl.pallas_call(
        kern,
        grid=grid,
        in_specs=in_specs,
        out_specs=out_specs,
        out_shape=out_shape,
        scratch_shapes=[pltpu.VMEM((tm, IN_WIDTH_PAD), F32), pltpu.VMEM((tm, D_MODEL), BF16)],
        compiler_params=pltpu.CompilerParams(
            dimension_semantics=("arbitrary", "arbitrary"), vmem_limit_bytes=VMEM_LIMIT),
        name="mixer_sample" if chunk_is_seq else "mixer_prompt",
    )(x, mod, s0_t, w["g_pre1"], w["w_in"], w["w_gk_up"], w["b_gk"], w["g_gmlp_v"], w["w_s"],
      w["b_s_full"], w["g_gla_o"], w["w_out"], w["g_post1"], w["g_pre2"], w["w_router"],
      w["b_router"])


MOE_BLOCK = 256
SEG_ALIGN = 16
WIN = 64
FFN_TILE = 512
FFN_TILE_SMALL = 128
BLOCK_ROWS = MOE_BLOCK * TOP_K + SEG_ALIGN * N_EXPERTS + WIN
COL_CHUNK = 256
COUNT_LANES = LANES


def _count_kernel(g_ref, o_ref):
    n = g_ref.shape[1]
    selb = jnp.where(g_ref[...] > 0.0, 1.0, 0.0).astype(BF16)
    tok = lax.broadcasted_iota(jnp.int32, (n, COUNT_LANES), 0)
    lane = lax.broadcasted_iota(jnp.int32, (n, COUNT_LANES), 1)
    ind = jnp.where(tok // MOE_BLOCK == lane, 1.0, 0.0).astype(BF16)
    o_ref[0] = jnp.dot(selb, ind, preferred_element_type=F32)


def _fill_permutation(p_scr, rk_scr, g_ref, cntp_ref, loff_ref, long_ref, blk, weighted):
    tb = g_ref.shape[1]
    g = g_ref[...]
    sel = g > 0.0
    r_i = lax.broadcasted_iota(jnp.int32, (tb, tb), 0)
    c_i = lax.broadcasted_iota(jnp.int32, (tb, tb), 1)
    earlier = jnp.where(r_i < c_i, 1.0, 0.0).astype(BF16)
    rank = jnp.dot(jnp.where(sel, 1.0, 0.0).astype(BF16), earlier, preferred_element_type=F32)
    rk = jnp.where(sel, rank, -1.0)
    p_scr[...] = jnp.zeros_like(p_scr)
    jwin = lax.broadcasted_iota(jnp.int32, (WIN, tb), 0).astype(F32)

    for e in range(N_EXPERTS):
        off = pl.multiple_of(loff_ref[blk * N_EXPERTS + e], SEG_ALIGN)
        val = g[e:e + 1] if weighted else 1.0
        p_scr[pl.ds(off, WIN), :] = jnp.where(rk[e:e + 1] == jwin, val, 0.0).astype(BF16)

    @pl.when(long_ref[blk] > 0)
    def _():
        rk_scr[...] = rk

        def expert_body(e, c):
            n = cntp_ref[blk * N_EXPERTS + e]
            off = loff_ref[blk * N_EXPERTS + e]
            rke = rk_scr[pl.ds(e, 1), :]
            val = g_ref[pl.ds(e, 1), :] if weighted else 1.0

            def win_body(wi, c2):
                rows = pl.ds(pl.multiple_of(off + wi * WIN, SEG_ALIGN), WIN)
                j = jwin + (wi * WIN).astype(F32)
                new = jnp.where(rke == j, val, 0.0)
                p_scr[rows, :] = jnp.where(j < n.astype(F32), new, p_scr[rows, :].astype(F32)).astype(BF16)
                return c2

            lax.fori_loop(1, (n + WIN - 1) // WIN, win_body, 0)
            return c

        lax.fori_loop(0, N_EXPERTS, expert_body, 0)


def _start_segments(cntp_ref, loff_ref, blk, make_copy):
    for e in range(N_EXPERTS):
        make_copy(e, pl.multiple_of(loff_ref[blk * N_EXPERTS + e], SEG_ALIGN),
                  pl.multiple_of(cntp_ref[blk * N_EXPERTS + e], SEG_ALIGN)).start()


def _dispatch_kernel(cntp_ref, loff_ref, goff_ref, rows_ref, long_ref, toff_ref, tlen_ref, g_ref, h2_ref,
                     xs_hbm, p_scr, rk_scr, xs_scr, zero_scr, sems, tail_sem, *, n_blocks):
    b = pl.program_id(0)
    slot = lax.rem(b, 2)

    def seg_copy(blk, s):
        def make(e, off, n):
            dst = pl.multiple_of(goff_ref[blk * N_EXPERTS + e], SEG_ALIGN)
            return pltpu.make_async_copy(xs_scr.at[s, pl.ds(off, n)], xs_hbm.at[pl.ds(dst, n)], sems.at[s])
        return make

    def wait_block(blk, s):
        n = pl.multiple_of(rows_ref[blk], SEG_ALIGN)
        pltpu.make_async_copy(xs_scr.at[s, pl.ds(0, n)], xs_hbm.at[pl.ds(0, n)], sems.at[s]).wait()

    def for_tails(action):
        def body(e, c):
            n = tlen_ref[e]

            @pl.when(n > 0)
            def _():
                dst = pl.multiple_of(toff_ref[e], SEG_ALIGN)
                nn = pl.multiple_of(n, SEG_ALIGN)
                action(pltpu.make_async_copy(zero_scr.at[pl.ds(0, nn)], xs_hbm.at[pl.ds(dst, nn)], tail_sem))

            return c

        lax.fori_loop(0, N_EXPERTS, body, 0)

    @pl.when(b == 0)
    def _():
        zero_scr[...] = jnp.zeros_like(zero_scr)
        for_tails(lambda cp: cp.start())

    @pl.when(b >= 2)
    def _():
        wait_block(b - 2, slot)

    _fill_permutation(p_scr, rk_scr, g_ref, cntp_ref, loff_ref, long_ref, b, weighted=False)
    for c in range(D_MODEL // COL_CHUNK):
        cols = slice(c * COL_CHUNK, (c + 1) * COL_CHUNK)
        xs = jnp.dot(p_scr[...], h2_ref[:, cols], preferred_element_type=F32)
        xs_scr[slot, :, cols] = xs.astype(BF16)
    _start_segments(cntp_ref, loff_ref, b, seg_copy(b, slot))

    @pl.when(b == n_blocks - 1)
    def _():
        for_tails(lambda cp: cp.wait())
        if n_blocks >= 2:
            wait_block(b - 1, 1 - slot)
        wait_block(b, slot)


def _ffn_kernel(te_ref, x_ref, wg_ref, wu_ref, wd_ref, y_ref, wgu_scr, wd_scr):
    i = pl.program_id(0)

    @pl.when(jnp.logical_or(i == 0, te_ref[i] != te_ref[jnp.maximum(i - 1, 0)]))
    def _():
        wgu_scr[:, 0:EXPERT_FF] = wg_ref[0].astype(BF16)
        wgu_scr[:, EXPERT_FF:2 * EXPERT_FF] = wu_ref[0].astype(BF16)
        wd_scr[...] = wd_ref[0].astype(BF16)

    gu = jnp.dot(x_ref[...], wgu_scr[...], preferred_element_type=F32)
    hid = _silu(gu[:, 0:EXPERT_FF]) * gu[:, EXPERT_FF:2 * EXPERT_FF]
    y_ref[...] = jnp.dot(hid.astype(BF16), wd_scr[...], preferred_element_type=F32).astype(BF16)


def _combine_kernel(cntp_ref, loff_ref, goff_ref, rows_ref, long_ref, g_ref, h2_ref, x1_ref, mod_ref,
                    wsgu_ref, wsd_ref, gpost2_ref, ys_hbm, out_ref, p_scr, rk_scr, ys_scr, sems,
                    *, n_blocks):
    b = pl.program_id(0)
    slot = lax.rem(b, 2)
    ns, ls, _ = x1_ref.shape

    def seg_copy(blk, s):
        def make(e, off, n):
            src = pl.multiple_of(goff_ref[blk * N_EXPERTS + e], SEG_ALIGN)
            return pltpu.make_async_copy(ys_hbm.at[pl.ds(src, n)], ys_scr.at[s, pl.ds(off, n)], sems.at[s])
        return make

    @pl.when(b == 0)
    def _():
        ys_scr[...] = jnp.zeros_like(ys_scr)
        _start_segments(cntp_ref, loff_ref, 0, seg_copy(0, 0))

    def wait_block(blk, s):
        n = pl.multiple_of(rows_ref[blk], SEG_ALIGN)
        pltpu.make_async_copy(ys_hbm.at[pl.ds(0, n)], ys_scr.at[s, pl.ds(0, n)], sems.at[s]).wait()

    nxt = jnp.minimum(b + 1, n_blocks - 1)
    _start_segments(cntp_ref, loff_ref, nxt, seg_copy(nxt, 1 - slot))

    _fill_permutation(p_scr, rk_scr, g_ref, cntp_ref, loff_ref, long_ref, b, weighted=True)
    wait_block(b, slot)

    @pl.when(b == n_blocks - 1)
    def _():
        wait_block(b, 1 - slot)

    routed = lax.dot_general(p_scr[...], ys_scr[slot], TN_DIMS, preferred_element_type=F32)
    hx = h2_ref[...]
    sgu = jnp.dot(hx, wsgu_ref[...], preferred_element_type=F32)
    shid = _silu(sgu[:, 0:SHARED_FF]) * sgu[:, SHARED_FF:2 * SHARED_FF]
    f = routed + jnp.dot(shid.astype(BF16), wsd_ref[...], preferred_element_type=F32)
    gate2 = mod_ref[...][:, 5:6]
    out_ref[...] = x1_ref[...] + gate2 * _rms(f, gpost2_ref[...]).reshape(ns, ls, D_MODEL)


def _segment_plan(cnt, n_tiles_max, ffn_tile):
    cntp = jnp.maximum((cnt + SEG_ALIGN - 1) // SEG_ALIGN, 1) * SEG_ALIGN
    loff = jnp.cumsum(cntp, axis=1) - cntp
    block_rows = jnp.sum(cntp, axis=1)
    has_long = jnp.max(cntp, axis=1) > WIN
    tot = jnp.sum(cntp, axis=0)
    reg = (tot + ffn_tile - 1) // ffn_tile * ffn_tile
    base = jnp.cumsum(reg) - reg
    goff = base[None, :] + jnp.cumsum(cntp, axis=0) - cntp
    tile_end = jnp.cumsum(reg // ffn_tile)
    tile = jnp.arange(n_tiles_max, dtype=jnp.int32)
    tile_expert = jnp.minimum(jnp.sum(tile_end[None, :] <= tile[:, None], axis=1), N_EXPERTS - 1)
    i32 = lambda a: a.astype(jnp.int32)
    block_tables = (i32(cntp.reshape(-1)), i32(loff.reshape(-1)), i32(goff.reshape(-1)),
                    i32(block_rows), i32(has_long))
    return block_tables, i32(base + tot), i32(reg - tot), i32(tile_expert), i32(tile_end[-1])


def _moe(h2, gates_t, x1, mod, w, *, ffn_tile):
    n_seq, seq_len, _ = x1.shape
    n_tok = n_seq * seq_len
    n_blocks = n_tok // MOE_BLOCK
    if seq_len < MOE_BLOCK:
        ns, ls = MOE_BLOCK // seq_len, seq_len
        x_map = lambda b, *_: (b, 0, 0)
        seq_map3 = lambda b, *_: (b, 0, 0)
    else:
        ns, ls = 1, MOE_BLOCK
        per_seq = seq_len // MOE_BLOCK
        x_map = lambda b, *_: (b // per_seq, b % per_seq, 0)
        seq_map3 = lambda b, *_: (b // per_seq, 0, 0)
    rows_max = (n_tok * TOP_K + n_blocks * N_EXPERTS * SEG_ALIGN
                + N_EXPERTS * (ffn_tile - SEG_ALIGN))
    n_tiles_max = -(-rows_max // ffn_tile)
    rows_alloc = n_tiles_max * ffn_tile
    params = pltpu.CompilerParams(dimension_semantics=("arbitrary",), vmem_limit_bytes=VMEM_LIMIT)

    chunk = min(n_tok, COUNT_LANES * MOE_BLOCK // 16)
    per_chunk = chunk // MOE_BLOCK
    cnt = pl.pallas_call(
        _count_kernel,
        grid=(n_tok // chunk,),
        in_specs=[pl.BlockSpec((N_EXPERTS, chunk), lambda i: (0, i))],
        out_specs=pl.BlockSpec((1, N_EXPERTS, COUNT_LANES), lambda i: (i, 0, 0)),
        out_shape=jax.ShapeDtypeStruct((n_tok // chunk, N_EXPERTS, COUNT_LANES), F32),
        compiler_params=params,
        name="moe_count",
    )(gates_t)
    cnt = jnp.swapaxes(cnt[:, :, :per_chunk], 1, 2).reshape(n_blocks, N_EXPERTS).astype(jnp.int32)
    block_tables, tail_off, tail_len, tile_expert, n_used = _segment_plan(cnt, n_tiles_max, ffn_tile)

    g_spec = pl.BlockSpec((N_EXPERTS, MOE_BLOCK), lambda b, *_: (0, b))
    h2_spec = pl.BlockSpec((MOE_BLOCK, D_MODEL), lambda b, *_: (b, 0))
    any_spec = pl.BlockSpec(memory_space=pl.ANY)
    perm_scratch = [pltpu.VMEM((BLOCK_ROWS, MOE_BLOCK), BF16), pltpu.VMEM((N_EXPERTS, MOE_BLOCK), F32)]

    x_sorted = pl.pallas_call(
        functools.partial(_dispatch_kernel, n_blocks=n_blocks),
        grid_spec=pltpu.PrefetchScalarGridSpec(
            num_scalar_prefetch=7, grid=(n_blocks,),
            in_specs=[g_spec, h2_spec], out_specs=any_spec,
            scratch_shapes=perm_scratch + [
                pltpu.VMEM((2, BLOCK_ROWS, D_MODEL), BF16),
                pltpu.VMEM((ffn_tile, D_MODEL), BF16),
                pltpu.SemaphoreType.DMA((2,)), pltpu.SemaphoreType.DMA(())]),
        out_shape=jax.ShapeDtypeStruct((rows_alloc, D_MODEL), BF16),
        compiler_params=params,
        name="moe_dispatch",
    )(*block_tables, tail_off, tail_len, gates_t, h2)

    row_map = lambda i, te: (i, 0)
    w_map = lambda i, te: (te[i], 0, 0)
    y_sorted = pl.pallas_call(
        _ffn_kernel,
        grid_spec=pltpu.PrefetchScalarGridSpec(
            num_scalar_prefetch=1, grid=(n_used,),
            in_specs=[pl.BlockSpec((ffn_tile, D_MODEL), row_map),
                      pl.BlockSpec((1, D_MODEL, EXPERT_FF), w_map),
                      pl.BlockSpec((1, D_MODEL, EXPERT_FF), w_map),
                      pl.BlockSpec((1, EXPERT_FF, D_MODEL), w_map)],
            out_specs=pl.BlockSpec((ffn_tile, D_MODEL), row_map),
            scratch_shapes=[pltpu.VMEM((D_MODEL, 2 * EXPERT_FF), BF16),
                            pltpu.VMEM((EXPERT_FF, D_MODEL), BF16)]),
        out_shape=jax.ShapeDtypeStruct((rows_alloc, D_MODEL), BF16),
        compiler_params=params,
        name="moe_ffn",
    )(tile_expert, x_sorted, w["w_e_gate"], w["w_e_up"], w["w_e_down"])

    full2 = lambda b, *_: (0, 0)
    return pl.pallas_call(
        functools.partial(_combine_kernel, n_blocks=n_blocks),
        grid_spec=pltpu.PrefetchScalarGridSpec(
            num_scalar_prefetch=5, grid=(n_blocks,),
            in_specs=[g_spec, h2_spec,
                      pl.BlockSpec((ns, ls, D_MODEL), x_map),
                      pl.BlockSpec((ns, MOD_ROWS, D_MODEL), seq_map3),
                      pl.BlockSpec((D_MODEL, 2 * SHARED_FF), full2),
                      pl.BlockSpec((SHARED_FF, D_MODEL), full2),
                      pl.BlockSpec((1, D_MODEL), full2),
                      any_spec],
            out_specs=pl.BlockSpec((ns, ls, D_MODEL), x_map),
            scratch_shapes=perm_scratch + [
                pltpu.VMEM((2, BLOCK_ROWS, D_MODEL), BF16),
                pltpu.SemaphoreType.DMA((2,))]),
        out_shape=jax.ShapeDtypeStruct(x1.shape, F32),
        compiler_params=params,
        name="moe_combine",
    )(*block_tables, gates_t, h2, x1, mod, w["w_sh_gu"], w["w_sh_down"], w["g_post2"], y_sorted)


def _prep_weights(g_pre1, g_post1, w_in, w_gk_up, b_gk, g_gmlp_v, w_s, b_s, g_gla_o, w_out, g_pre2,
                  g_post2, w_router, b_router, w_e_gate, w_e_up, w_e_down, w_sh_gate, w_sh_up,
                  w_sh_down):
    row = lambda v: v.reshape(1, -1)
    wr = jnp.pad(w_router, ((0, 0), (0, LANES - N_EXPERTS)))
    wr_top = lax.bitcast_convert_type(
        lax.bitcast_convert_type(wr, jnp.uint32) & jnp.uint32(0xFFFF0000), F32)
    wr_hi = wr_top.astype(BF16)
    wr_lo = (wr - wr_top).astype(BF16)
    wr_split = jnp.concatenate([jnp.concatenate([wr_hi, wr_lo], axis=1),
                                jnp.concatenate([wr_hi, jnp.zeros_like(wr_lo)], axis=1)], axis=0)
    return {
        "g_pre1": row(g_pre1), "g_post1": row(g_post1), "g_pre2": row(g_pre2), "g_post2": row(g_post2),
        "w_in": jnp.pad(w_in, ((0, 0), (0, GK_PAD - GLA_GATE_RANK))).astype(BF16),
        "w_gk_up": jnp.pad(w_gk_up, ((0, GK_PAD - GLA_GATE_RANK), (0, 0))).astype(BF16),
        "b_gk": row(b_gk),
        "g_gmlp_v": row(g_gmlp_v),
        "w_s": w_s,
        "b_s_full": jnp.repeat(b_s.T, GMLP_HEAD, axis=1),
        "g_gla_o": g_gla_o,
        "w_out": w_out.astype(BF16),
        "w_router": wr_split,
        "b_router": b_router.reshape(N_EXPERTS, 1),
        "w_e_gate": w_e_gate, "w_e_up": w_e_up, "w_e_down": w_e_down,
        "w_sh_gu": jnp.concatenate([w_sh_gate, w_sh_up], axis=-1).astype(BF16),
        "w_sh_down": w_sh_down.astype(BF16),
    }


PROMPT_TILE = 512
SAMPLE_TILE = 256


def _layer(x, mod, s0, w, *, emit_v):
    n_seq, seq_len, _ = x.shape
    s0_t = jnp.swapaxes(s0, -1, -2)
    if seq_len >= GMLP_CHUNK:
        outs = _mixer(x, mod, s0_t, w, seq_tile=PROMPT_TILE, chunk=GMLP_CHUNK, emit_v=emit_v)
    else:
        outs = _mixer(x, mod, s0_t, w, seq_tile=SAMPLE_TILE, chunk=seq_len, emit_v=emit_v)
    x1, h2, gates_t, st = outs[:4]
    rows_per_expert = n_seq * seq_len * TOP_K // N_EXPERTS
    y = _moe(h2, gates_t, x1, mod, w, ffn_tile=FFN_TILE if rows_per_expert >= 4 * FFN_TILE else FFN_TILE_SMALL)
    v = outs[4].reshape(n_seq, seq_len, GMLP_GROUPS, GMLP_HEAD) if emit_v else None
    return y, jnp.swapaxes(st, -1, -2), v


def kernel(x_prompt, x_sample, state_gla, c_prompt, c_sample, w_ada, b_ada, g_pre1, g_post1, w_in, w_gk_up, b_gk, g_gmlp_v, w_s, b_s, g_gla_o, w_out, g_pre2, g_post2, w_router, b_router, w_e_gate, w_e_up, w_e_down, w_sh_gate, w_sh_up, w_sh_down):
    depth = w_ada.shape[0]
    n_p, n_s = x_prompt.shape[0], x_sample.shape[0]
    c_all = jnp.concatenate([c_prompt, c_sample], axis=0)
    x_p, x_s = x_prompt, x_sample
    sp_list, ss_list, vs_list = [], [], []
    for l in range(depth):
        w = _prep_weights(g_pre1[l], g_post1[l], w_in[l], w_gk_up[l], b_gk[l], g_gmlp_v[l], w_s[l],
                          b_s[l], g_gla_o[l], w_out[l], g_pre2[l], g_post2[l], w_router[l],
                          b_router[l], w_e_gate[l], w_e_up[l], w_e_down[l], w_sh_gate[l], w_sh_up[l],
                          w_sh_down[l])
        mod = _ada(c_all, w_ada[l], b_ada[l])
        mod = jnp.pad(jnp.swapaxes(mod, 0, 1), ((0, 0), (0, MOD_ROWS - 6), (0, 0)))
        s0_p = jnp.zeros((n_p, GLA_HEADS, GLA_DK, GLA_DV), F32)
        x_p, s_p, _ = _layer(x_p, mod[:n_p], s0_p, w, emit_v=False)
        x_s, s_s, v_s = _layer(x_s, mod[n_p:], state_gla[l], w, emit_v=True)
        sp_list.append(s_p)
        ss_list.append(s_s)
        vs_list.append(v_s)
    return (x_p, x_s, jnp.stack(sp_list), jnp.stack(ss_list), jnp.stack(vs_list))
```

```python
import functools

import jax
import jax.numpy as jnp
from jax import lax
from jax.experimental import pallas as pl
from jax.experimental.pallas import tpu as pltpu

D_MODEL = 1024
GMLP_WIDTH = 512
GMLP_GROUPS = 4
GMLP_HEAD = 128
GMLP_CHUNK = 128
CAUSAL_BLOCK = 64
GLA_WIDTH = 512
GLA_HEADS = 4
GLA_DV = 128
GLA_DK = 64
GLA_KEY_WIDTH = 256
GLA_GATE_RANK = 16
GLA_GATE_NORMALIZER = 16.0
N_EXPERTS = 64
TOP_K = 8
N_EXPERT_GROUPS = 8
GROUP_SIZE = N_EXPERTS // N_EXPERT_GROUPS
TOPK_GROUPS = 4
EXPERT_FF = 256
SHARED_FF = 256
ROUTED_SCALE = 2.5
EPS = 1e-6

LANES = 128
GK_PAD = LANES
IN_WIDTH_PAD = 2 * GMLP_WIDTH + 2 * GLA_KEY_WIDTH + 2 * GLA_WIDTH + GK_PAD
OFF_U = 0
OFF_VG = GMLP_WIDTH
OFF_Q = 2 * GMLP_WIDTH
OFF_K = OFF_Q + GLA_KEY_WIDTH
OFF_VL = OFF_K + GLA_KEY_WIDTH
OFF_R = OFF_VL + GLA_WIDTH
OFF_GK = OFF_R + GLA_WIDTH
MOD_ROWS = 8
VMEM_LIMIT = 56 * 1024 * 1024

F32 = jnp.float32
BF16 = jnp.bfloat16
NT_DIMS = (((1,), (1,)), ((), ()))
TN_DIMS = (((0,), (0,)), ((), ()))


def _rms(x, g):
    return x * lax.rsqrt(jnp.mean(x * x, axis=-1, keepdims=True) + EPS) * g


def _gelu(x):
    return 0.5 * x * (1.0 + jnp.tanh(0.7978845608028654 * (x + 0.044715 * (x * x * x))))


def _sigmoid(x):
    return 1.0 / (1.0 + jnp.exp(-x))


def _silu(x):
    return x * _sigmoid(x)


def _ada_kernel(c_ref, w_ref, b_ref, o_ref):
    a = _silu(c_ref[...])
    o_ref[0] = jnp.dot(a, w_ref[...], precision=lax.Precision.HIGHEST,
                       preferred_element_type=F32) + b_ref[0]


def _ada(c_all, w_ada, b_ada):
    n = c_all.shape[0]
    return pl.pallas_call(
        _ada_kernel,
        grid=(6,),
        in_specs=[pl.BlockSpec((n, D_MODEL), lambda j: (0, 0)),
                  pl.BlockSpec((D_MODEL, D_MODEL), lambda j: (0, j)),
                  pl.BlockSpec((1, 1, D_MODEL), lambda j: (j, 0, 0))],
        out_specs=pl.BlockSpec((1, n, D_MODEL), lambda j: (j, 0, 0)),
        out_shape=jax.ShapeDtypeStruct((6, n, D_MODEL), F32),
        compiler_params=pltpu.CompilerParams(vmem_limit_bytes=VMEM_LIMIT),
        name="ada",
    )(c_all, w_ada, b_ada.reshape(6, 1, D_MODEL))


def _route(logits_t, bias_t):
    t = logits_t.shape[1]
    scores = _sigmoid(logits_t)
    sel = scores + bias_t
    sub = lax.broadcasted_iota(jnp.int32, (GROUP_SIZE, t), 0)
    gscore = []
    for g in range(N_EXPERT_GROUPS):
        blk = sel[g * GROUP_SIZE:(g + 1) * GROUP_SIZE]
        m1 = jnp.max(blk, axis=0, keepdims=True)
        first = jnp.min(jnp.where(blk == m1, sub, GROUP_SIZE), axis=0, keepdims=True)
        m2 = jnp.max(jnp.where(sub == first, -jnp.inf, blk), axis=0, keepdims=True)
        gscore.append(m1 + m2)
    neg = jnp.full((GROUP_SIZE, t), -jnp.inf, F32)
    masked = []
    for g in range(N_EXPERT_GROUPS):
        rank = jnp.zeros((1, t), jnp.int32)
        for o in range(N_EXPERT_GROUPS):
            if o == g:
                continue
            ahead = (gscore[o] >= gscore[g]) if o < g else (gscore[o] > gscore[g])
            rank = rank + jnp.where(ahead, 1, 0)
        keep = jnp.broadcast_to(rank < TOPK_GROUPS, (GROUP_SIZE, t))
        masked.append(jnp.where(keep, sel[g * GROUP_SIZE:(g + 1) * GROUP_SIZE], neg))
    selm = jnp.concatenate(masked, axis=0)
    eidx = lax.broadcasted_iota(jnp.int32, (N_EXPERTS, t), 0)
    picked = jnp.zeros((N_EXPERTS, t), F32)
    for _ in range(TOP_K):
        best = jnp.max(selm, axis=0, keepdims=True)
        first = jnp.min(jnp.where(selm == best, eidx, N_EXPERTS), axis=0, keepdims=True)
        hit = eidx == first
        picked = jnp.where(hit, 1.0, picked)
        selm = jnp.where(hit, -jnp.inf, selm)
    chosen = jnp.where(picked > 0.0, scores, 0.0)
    denom = jnp.sum(chosen, axis=0, keepdims=True)
    return chosen * (ROUTED_SCALE / denom)


def _mixer_kernel(x_ref, mod_ref, s0_ref, gpre1_ref, win_ref, wgk_ref, bgk_ref, ggv_ref,
                  ws_ref, bsf_ref, ggo_ref, wout_ref, gpost1_ref, gpre2_ref, wr_ref, br_ref,
                  x1_ref, h2_ref, gates_ref, st_ref, *rest,
                  chunk, n_chunks, chunk_is_seq, emit_v):
    if emit_v:
        v_ref, *rest = rest
    proj, mix_scr = rest
    ns, ls, _ = x_ref.shape
    tm = ns * ls

    mod_row = lambda i: mod_ref[:, i:i + 1, :]

    h = _rms(x_ref[...], gpre1_ref[...]) * (1.0 + mod_row(1)) + mod_row(0)
    proj[...] = jnp.dot(h.reshape(tm, D_MODEL).astype(BF16), win_ref[...],
                        preferred_element_type=F32)

    if chunk_is_seq:
        st_ref[...] = s0_ref[...]
    else:
        @pl.when(pl.program_id(1) == 0)
        def _():
            st_ref[...] = s0_ref[...]

    row = lax.broadcasted_iota(jnp.int32, (chunk, chunk), 0)
    col = lax.broadcasted_iota(jnp.int32, (chunk, chunk), 1)
    tri = (row >= col).astype(F32)
    causal = row >= col
    block_causal = (row // CAUSAL_BLOCK) >= (col // CAUSAL_BLOCK)
    wmix = [jnp.where(block_causal, ws_ref[g, 0:chunk, 0:chunk], 0.0).astype(BF16)
            for g in range(GMLP_GROUPS)]
    mid = chunk // 2

    def chunk_body(c, carry):
        rows = pl.ds(pl.multiple_of(c * chunk, chunk), chunk)
        sidx = c if chunk_is_seq else 0

        vg = _gelu(proj[rows, OFF_VG:OFF_VG + GMLP_WIDTH])
        vn = _rms(vg, ggv_ref[...])
        if emit_v:
            v_ref[rows, :] = vn
        vnb = vn.astype(BF16)
        mixed = jnp.concatenate(
            [jnp.dot(wmix[g], vnb[:, g * GMLP_HEAD:(g + 1) * GMLP_HEAD], preferred_element_type=F32)
             for g in range(GMLP_GROUPS)], axis=-1) + bsf_ref[0:chunk, :]
        u = _gelu(proj[rows, OFF_U:OFF_U + GMLP_WIDTH])
        mix_scr[rows, 0:GMLP_WIDTH] = (u * mixed).astype(BF16)

        q = proj[rows, OFF_Q:OFF_Q + GLA_KEY_WIDTH] * (GLA_DK ** -0.5)
        k = proj[rows, OFF_K:OFF_K + GLA_KEY_WIDTH]
        gk = proj[rows, OFF_GK:OFF_GK + GK_PAD]
        z = jnp.dot(gk.astype(BF16), wgk_ref[...], preferred_element_type=F32) + bgk_ref[...]
        la = (jnp.minimum(z, 0.0) - jnp.log(1.0 + jnp.exp(-jnp.abs(z)))) * (1.0 / GLA_GATE_NORMALIZER)
        b = jnp.dot(tri, la, precision=lax.Precision.HIGHEST, preferred_element_type=F32)
        b_mid = b[mid - 1:mid]
        b_end = b[chunk - 1:chunk]
        qt = (q * jnp.exp(b - b_mid)).astype(BF16)
        kt = (k * jnp.exp(b_mid - b)).astype(BF16)
        qs = (q * jnp.exp(b)).astype(BF16)
        kd = (k * jnp.exp(b_end - b)).astype(BF16)
        decay = jnp.exp(b_end)
        for hd in range(GLA_HEADS):
            ksl = slice(hd * GLA_DK, (hd + 1) * GLA_DK)
            vsl = slice(OFF_VL + hd * GLA_DV, OFF_VL + (hd + 1) * GLA_DV)
            rsl = slice(OFF_R + hd * GLA_DV, OFF_R + (hd + 1) * GLA_DV)
            att = lax.dot_general(qt[:, ksl], kt[:, ksl], NT_DIMS, preferred_element_type=F32)
            att = jnp.where(causal, att, 0.0).astype(BF16)
            vh = proj[rows, vsl].astype(BF16)
            st = st_ref[sidx, hd]
            o = jnp.dot(att, vh, preferred_element_type=F32) + lax.dot_general(
                qs[:, ksl], st.astype(BF16), NT_DIMS, preferred_element_type=F32)
            on = _rms(o, ggo_ref[hd:hd + 1, :])
            mix_scr[rows, GMLP_WIDTH + hd * GLA_DV:GMLP_WIDTH + (hd + 1) * GLA_DV] = (
                on * _silu(proj[rows, rsl])).astype(BF16)
            st_ref[sidx, hd] = st * decay[:, ksl] + lax.dot_general(
                vh, kd[:, ksl], TN_DIMS, preferred_element_type=F32)
        return carry

    lax.fori_loop(0, n_chunks, chunk_body, 0, unroll=4)

    mixo = jnp.dot(mix_scr[...], wout_ref[...], preferred_element_type=F32).reshape(ns, ls, D_MODEL)
    x1 = x_ref[...] + mod_row(2) * _rms(mixo, gpost1_ref[...])
    x1_ref[...] = x1
    h2 = (_rms(x1, gpre2_ref[...]) * (1.0 + mod_row(4)) + mod_row(3)).reshape(tm, D_MODEL)
    h2_hi = h2.astype(BF16)
    h2_ref[...] = h2_hi

    h2_lo = (h2 - h2_hi.astype(F32)).astype(BF16)
    parts = jnp.dot(jnp.concatenate([h2_hi, h2_lo], axis=-1), wr_ref[...], preferred_element_type=F32)
    logits = parts[:, 0:LANES] + parts[:, LANES:2 * LANES]
    gates_ref[...] = _route(logits.T[0:N_EXPERTS], br_ref[...])


def _mixer(x, mod, s0_t, w, *, seq_tile, chunk, emit_v):
    n_seq, seq_len, _ = x.shape
    chunk_is_seq = seq_len == chunk
    if chunk_is_seq:
        ns, ls = seq_tile // chunk, chunk
        grid = (1, n_seq // ns)
        x_map = lambda b, t: (t, 0, 0)
        seq_map3 = lambda b, t: (t, 0, 0)
        seq_map4 = lambda b, t: (t, 0, 0, 0)
    else:
        ns, ls = 1, seq_tile
        grid = (n_seq, seq_len // seq_tile)
        x_map = lambda b, t: (b, t, 0)
        seq_map3 = lambda b, t: (b, 0, 0)
        seq_map4 = lambda b, t: (b, 0, 0, 0)
    tm = ns * ls
    n_tok = n_seq * seq_len
    tok_block = lambda b, t: b * grid[1] + t
    tok_map = lambda b, t: (tok_block(b, t), 0)
    full2 = lambda b, t: (0, 0)
    full3 = lambda b, t: (0, 0, 0)

    in_specs = [
        pl.BlockSpec((ns, ls, D_MODEL), x_map),
        pl.BlockSpec((ns, MOD_ROWS, D_MODEL), seq_map3),
        pl.BlockSpec((ns, GLA_HEADS, GLA_DV, GLA_DK), seq_map4),
        pl.BlockSpec((1, D_MODEL), full2),
        pl.BlockSpec((D_MODEL, IN_WIDTH_PAD), full2),
        pl.BlockSpec((GK_PAD, GLA_KEY_WIDTH), full2),
        pl.BlockSpec((1, GLA_KEY_WIDTH), full2),
        pl.BlockSpec((1, GMLP_WIDTH), full2),
        pl.BlockSpec((GMLP_GROUPS, GMLP_CHUNK, GMLP_CHUNK), full3),
        pl.BlockSpec((GMLP_CHUNK, GMLP_WIDTH), full2),
        pl.BlockSpec((GLA_HEADS, GLA_DV), full2),
        pl.BlockSpec((D_MODEL, D_MODEL), full2),
        pl.BlockSpec((1, D_MODEL), full2),
        pl.BlockSpec((1, D_MODEL), full2),
        pl.BlockSpec((2 * D_MODEL, 2 * LANES), full2),
        pl.BlockSpec((N_EXPERTS, 1), full2),
    ]
    out_specs = [
        pl.BlockSpec((ns, ls, D_MODEL), x_map),
        pl.BlockSpec((tm, D_MODEL), tok_map),
        pl.BlockSpec((N_EXPERTS, tm), lambda b, t: (0, tok_block(b, t))),
        pl.BlockSpec((ns, GLA_HEADS, GLA_DV, GLA_DK), seq_map4),
    ]
    out_shape = [
        jax.ShapeDtypeStruct((n_seq, seq_len, D_MODEL), F32),
        jax.ShapeDtypeStruct((n_tok, D_MODEL), BF16),
        jax.ShapeDtypeStruct((N_EXPERTS, n_tok), F32),
        jax.ShapeDtypeStruct((n_seq, GLA_HEADS, GLA_DV, GLA_DK), F32),
    ]
    if emit_v:
        out_specs.append(pl.BlockSpec((tm, GMLP_WIDTH), tok_map))
        out_shape.append(jax.ShapeDtypeStruct((n_tok, GMLP_WIDTH), F32))

    kern = functools.partial(_mixer_kernel, chunk=chunk, n_chunks=tm // chunk,
                             chunk_is_seq=chunk_is_seq, emit_v=emit_v)
    return pl.pallas_call(
        kern,
        grid=grid,
        in_specs=in_specs,
        out_specs=out_specs,
        out_shape=out_shape,
        scratch_shapes=[pltpu.VMEM((tm, IN_WIDTH_PAD), F32), pltpu.VMEM((tm, D_MODEL), BF16)],
        compiler_params=pltpu.CompilerParams(
            dimension_semantics=("arbitrary", "arbitrary"), vmem_limit_bytes=VMEM_LIMIT),
        name="mixer_sample" if chunk_is_seq else "mixer_prompt",
    )(x, mod, s0_t, w["g_pre1"], w["w_in"], w["w_gk_up"], w["b_gk"], w["g_gmlp_v"], w["w_s"],
      w["b_s_full"], w["g_gla_o"], w["w_out"], w["g_post1"], w["g_pre2"], w["w_router"],
      w["b_router"])


MOE_BLOCK = 256
SEG_ALIGN = 16
WIN = 64
FFN_TILE = 1024
FFN_TILE_SMALL = 128
BLOCK_ROWS = MOE_BLOCK * TOP_K + SEG_ALIGN * N_EXPERTS + WIN
MAIN_ROWS = 2688
COL_CHUNK = 256
COUNT_LANES = LANES


def _count_kernel(g_ref, o_ref):
    n = g_ref.shape[1]
    selb = jnp.where(g_ref[...] > 0.0, 1.0, 0.0).astype(BF16)
    tok = lax.broadcasted_iota(jnp.int32, (n, COUNT_LANES), 0)
    lane = lax.broadcasted_iota(jnp.int32, (n, COUNT_LANES), 1)
    ind = jnp.where(tok // MOE_BLOCK == lane, 1.0, 0.0).astype(BF16)
    o_ref[0] = jnp.dot(selb, ind, preferred_element_type=F32)


def _fill_permutation(p_scr, rk_scr, g_ref, cntp_ref, loff_ref, long_ref, blk, weighted):
    tb = g_ref.shape[1]
    g = g_ref[...]
    sel = g > 0.0
    r_i = lax.broadcasted_iota(jnp.int32, (tb, tb), 0)
    c_i = lax.broadcasted_iota(jnp.int32, (tb, tb), 1)
    earlier = jnp.where(r_i < c_i, 1.0, 0.0).astype(BF16)
    rank = jnp.dot(jnp.where(sel, 1.0, 0.0).astype(BF16), earlier, preferred_element_type=F32)
    rk = jnp.where(sel, rank, -1.0)
    p_scr[...] = jnp.zeros_like(p_scr)
    jwin = lax.broadcasted_iota(jnp.int32, (WIN, tb), 0).astype(F32)

    for e in range(N_EXPERTS):
        off = pl.multiple_of(loff_ref[blk * N_EXPERTS + e], SEG_ALIGN)
        val = g[e:e + 1] if weighted else 1.0
        p_scr[pl.ds(off, WIN), :] = jnp.where(rk[e:e + 1] == jwin, val, 0.0).astype(BF16)

    @pl.when(long_ref[blk] > 0)
    def _():
        rk_scr[...] = rk

        def expert_body(e, c):
            n = cntp_ref[blk * N_EXPERTS + e]
            off = loff_ref[blk * N_EXPERTS + e]
            rke = rk_scr[pl.ds(e, 1), :]
            val = g_ref[pl.ds(e, 1), :] if weighted else 1.0

            def win_body(wi, c2):
                rows = pl.ds(pl.multiple_of(off + wi * WIN, SEG_ALIGN), WIN)
                j = jwin + (wi * WIN).astype(F32)
                new = jnp.where(rke == j, val, 0.0)
                p_scr[rows, :] = jnp.where(j < n.astype(F32), new, p_scr[rows, :].astype(F32)).astype(BF16)
                return c2

            lax.fori_loop(1, (n + WIN - 1) // WIN, win_body, 0)
            return c

        lax.fori_loop(0, N_EXPERTS, expert_body, 0)


def _start_segments(cntp_ref, loff_ref, blk, make_copy):
    for e in range(N_EXPERTS):
        make_copy(e, pl.multiple_of(loff_ref[blk * N_EXPERTS + e], SEG_ALIGN),
                  pl.multiple_of(cntp_ref[blk * N_EXPERTS + e], SEG_ALIGN)).start()


def _dispatch_kernel(cntp_ref, loff_ref, goff_ref, rows_ref, long_ref, toff_ref, tlen_ref, g_ref, h2_ref,
                     xs_hbm, p_scr, rk_scr, xs_scr, zero_scr, sems, tail_sem, *, n_blocks):
    b = pl.program_id(0)
    slot = lax.rem(b, 2)

    def seg_copy(blk, s):
        def make(e, off, n):
            dst = pl.multiple_of(goff_ref[blk * N_EXPERTS + e], SEG_ALIGN)
            return pltpu.make_async_copy(xs_scr.at[s, pl.ds(off, n)], xs_hbm.at[pl.ds(dst, n)], sems.at[s])
        return make

    def wait_block(blk, s):
        n = pl.multiple_of(rows_ref[blk], SEG_ALIGN)
        pltpu.make_async_copy(xs_scr.at[s, pl.ds(0, n)], xs_hbm.at[pl.ds(0, n)], sems.at[s]).wait()

    def for_tails(action):
        def body(e, c):
            n = tlen_ref[e]

            @pl.when(n > 0)
            def _():
                dst = pl.multiple_of(toff_ref[e], SEG_ALIGN)
                nn = pl.multiple_of(n, SEG_ALIGN)
                action(pltpu.make_async_copy(zero_scr.at[pl.ds(0, nn)], xs_hbm.at[pl.ds(dst, nn)], tail_sem))

            return c

        lax.fori_loop(0, N_EXPERTS, body, 0)

    @pl.when(b == 0)
    def _():
        zero_scr[...] = jnp.zeros_like(zero_scr)
        for_tails(lambda cp: cp.start())

    @pl.when(b >= 2)
    def _():
        wait_block(b - 2, slot)

    _fill_permutation(p_scr, rk_scr, g_ref, cntp_ref, loff_ref, long_ref, b, weighted=False)
    def sort_rows(rows):
        for c in range(D_MODEL // COL_CHUNK):
            cols = slice(c * COL_CHUNK, (c + 1) * COL_CHUNK)
            xs = jnp.dot(p_scr[rows, :], h2_ref[:, cols], preferred_element_type=F32)
            xs_scr[slot, rows, cols] = xs.astype(BF16)

    sort_rows(slice(0, MAIN_ROWS))

    @pl.when(rows_ref[b] > MAIN_ROWS)
    def _():
        sort_rows(slice(MAIN_ROWS, BLOCK_ROWS))

    _start_segments(cntp_ref, loff_ref, b, seg_copy(b, slot))

    @pl.when(b == n_blocks - 1)
    def _():
        for_tails(lambda cp: cp.wait())
        if n_blocks >= 2:
            wait_block(b - 1, 1 - slot)
        wait_block(b, slot)


def _ffn_kernel(te_ref, nu_ref, x_ref, wg_ref, wu_ref, wd_ref, y_ref, wgu_scr, wd_scr):
    i = pl.program_id(0)

    @pl.when(i < nu_ref[0])
    def _():
        @pl.when(jnp.logical_or(i == 0, te_ref[i] != te_ref[jnp.maximum(i - 1, 0)]))
        def _():
            wgu_scr[:, 0:EXPERT_FF] = wg_ref[0].astype(BF16)
            wgu_scr[:, EXPERT_FF:2 * EXPERT_FF] = wu_ref[0].astype(BF16)
            wd_scr[...] = wd_ref[0].astype(BF16)

        gu = jnp.dot(x_ref[...], wgu_scr[...], preferred_element_type=F32)
        hid = _silu(gu[:, 0:EXPERT_FF]) * gu[:, EXPERT_FF:2 * EXPERT_FF]
        y_ref[...] = jnp.dot(hid.astype(BF16), wd_scr[...], preferred_element_type=F32).astype(BF16)


def _combine_kernel(cntp_ref, loff_ref, goff_ref, rows_ref, long_ref, g_ref, h2_ref, x1_ref, mod_ref,
                    wsgu_ref, wsd_ref, gpost2_ref, ys_hbm, out_ref, p_scr, rk_scr, ys_scr, routed_scr, sems,
                    *, n_blocks):
    b = pl.program_id(0)
    slot = lax.rem(b, 2)
    ns, ls, _ = x1_ref.shape

    def seg_copy(blk, s):
        def make(e, off, n):
            src = pl.multiple_of(goff_ref[blk * N_EXPERTS + e], SEG_ALIGN)
            return pltpu.make_async_copy(ys_hbm.at[pl.ds(src, n)], ys_scr.at[s, pl.ds(off, n)], sems.at[s])
        return make

    @pl.when(b == 0)
    def _():
        ys_scr[...] = jnp.zeros_like(ys_scr)
        _start_segments(cntp_ref, loff_ref, 0, seg_copy(0, 0))

    def wait_block(blk, s):
        n = pl.multiple_of(rows_ref[blk], SEG_ALIGN)
        pltpu.make_async_copy(ys_hbm.at[pl.ds(0, n)], ys_scr.at[s, pl.ds(0, n)], sems.at[s]).wait()

    nxt = jnp.minimum(b + 1, n_blocks - 1)
    _start_segments(cntp_ref, loff_ref, nxt, seg_copy(nxt, 1 - slot))

    _fill_permutation(p_scr, rk_scr, g_ref, cntp_ref, loff_ref, long_ref, b, weighted=True)
    wait_block(b, slot)

    @pl.when(b == n_blocks - 1)
    def _():
        wait_block(b, 1 - slot)

    def unsort_rows(rows):
        return lax.dot_general(p_scr[rows, :], ys_scr[slot, rows, :], TN_DIMS, preferred_element_type=F32)

    routed_scr[...] = unsort_rows(slice(0, MAIN_ROWS))

    @pl.when(rows_ref[b] > MAIN_ROWS)
    def _():
        routed_scr[...] += unsort_rows(slice(MAIN_ROWS, BLOCK_ROWS))

    routed = routed_scr[...]
    hx = h2_ref[...]
    sgu = jnp.dot(hx, wsgu_ref[...], preferred_element_type=F32)
    shid = _silu(sgu[:, 0:SHARED_FF]) * sgu[:, SHARED_FF:2 * SHARED_FF]
    f = routed + jnp.dot(shid.astype(BF16), wsd_ref[...], preferred_element_type=F32)
    gate2 = mod_ref[...][:, 5:6]
    out_ref[...] = x1_ref[...] + gate2 * _rms(f, gpost2_ref[...]).reshape(ns, ls, D_MODEL)


def _segment_plan(cnt, n_tiles_max, ffn_tile):
    cntp = jnp.maximum((cnt + SEG_ALIGN - 1) // SEG_ALIGN, 1) * SEG_ALIGN
    loff = jnp.cumsum(cntp, axis=1) - cntp
    block_rows = jnp.sum(cntp, axis=1)
    has_long = jnp.max(cntp, axis=1) > WIN
    tot = jnp.sum(cntp, axis=0)
    reg = (tot + ffn_tile - 1) // ffn_tile * ffn_tile
    base = jnp.cumsum(reg) - reg
    goff = base[None, :] + jnp.cumsum(cntp, axis=0) - cntp
    tile_end = jnp.cumsum(reg // ffn_tile)
    tile = jnp.arange(n_tiles_max, dtype=jnp.int32)
    tile_expert = jnp.minimum(jnp.sum(tile_end[None, :] <= tile[:, None], axis=1), N_EXPERTS - 1)
    i32 = lambda a: a.astype(jnp.int32)
    block_tables = (i32(cntp.reshape(-1)), i32(loff.reshape(-1)), i32(goff.reshape(-1)),
                    i32(block_rows), i32(has_long))
    return block_tables, i32(base + tot), i32(reg - tot), i32(tile_expert), i32(tile_end[-1:])


def _moe(h2, gates_t, x1, mod, w, *, ffn_tile):
    n_seq, seq_len, _ = x1.shape
    n_tok = n_seq * seq_len
    n_blocks = n_tok // MOE_BLOCK
    if seq_len < MOE_BLOCK:
        ns, ls = MOE_BLOCK // seq_len, seq_len
        x_map = lambda b, *_: (b, 0, 0)
        seq_map3 = lambda b, *_: (b, 0, 0)
    else:
        ns, ls = 1, MOE_BLOCK
        per_seq = seq_len // MOE_BLOCK
        x_map = lambda b, *_: (b // per_seq, b % per_seq, 0)
        seq_map3 = lambda b, *_: (b // per_seq, 0, 0)
    rows_max = (n_tok * TOP_K + n_blocks * N_EXPERTS * SEG_ALIGN
                + N_EXPERTS * (ffn_tile - SEG_ALIGN))
    n_tiles_max = -(-rows_max // ffn_tile)
    rows_alloc = n_tiles_max * ffn_tile
    params = pltpu.CompilerParams(dimension_semantics=("arbitrary",), vmem_limit_bytes=VMEM_LIMIT)

    chunk = min(n_tok, COUNT_LANES * MOE_BLOCK // 16)
    per_chunk = chunk // MOE_BLOCK
    cnt = pl.pallas_call(
        _count_kernel,
        grid=(n_tok // chunk,),
        in_specs=[pl.BlockSpec((N_EXPERTS, chunk), lambda i: (0, i))],
        out_specs=pl.BlockSpec((1, N_EXPERTS, COUNT_LANES), lambda i: (i, 0, 0)),
        out_shape=jax.ShapeDtypeStruct((n_tok // chunk, N_EXPERTS, COUNT_LANES), F32),
        compiler_params=params,
        name="moe_count",
    )(gates_t)
    cnt = jnp.swapaxes(cnt[:, :, :per_chunk], 1, 2).reshape(n_blocks, N_EXPERTS).astype(jnp.int32)
    block_tables, tail_off, tail_len, tile_expert, n_used = _segment_plan(cnt, n_tiles_max, ffn_tile)

    g_spec = pl.BlockSpec((N_EXPERTS, MOE_BLOCK), lambda b, *_: (0, b))
    h2_spec = pl.BlockSpec((MOE_BLOCK, D_MODEL), lambda b, *_: (b, 0))
    any_spec = pl.BlockSpec(memory_space=pl.ANY)
    perm_scratch = [pltpu.VMEM((BLOCK_ROWS, MOE_BLOCK), BF16), pltpu.VMEM((N_EXPERTS, MOE_BLOCK), F32)]

    x_sorted = pl.pallas_call(
        functools.partial(_dispatch_kernel, n_blocks=n_blocks),
        grid_spec=pltpu.PrefetchScalarGridSpec(
            num_scalar_prefetch=7, grid=(n_blocks,),
            in_specs=[g_spec, h2_spec], out_specs=any_spec,
            scratch_shapes=perm_scratch + [
                pltpu.VMEM((2, BLOCK_ROWS, D_MODEL), BF16),
                pltpu.VMEM((ffn_tile, D_MODEL), BF16),
                pltpu.SemaphoreType.DMA((2,)), pltpu.SemaphoreType.DMA(())]),
        out_shape=jax.ShapeDtypeStruct((rows_alloc, D_MODEL), BF16),
        compiler_params=params,
        name="moe_dispatch",
    )(*block_tables, tail_off, tail_len, gates_t, h2)

    last_used = lambda i, nu: jnp.maximum(jnp.minimum(i, nu[0] - 1), 0)
    row_map = lambda i, te, nu: (last_used(i, nu), 0)
    w_map = lambda i, te, nu: (te[last_used(i, nu)], 0, 0)
    y_sorted = pl.pallas_call(
        _ffn_kernel,
        grid_spec=pltpu.PrefetchScalarGridSpec(
            num_scalar_prefetch=2, grid=(n_tiles_max,),
            in_specs=[pl.BlockSpec((ffn_tile, D_MODEL), row_map),
                      pl.BlockSpec((1, D_MODEL, EXPERT_FF), w_map),
                      pl.BlockSpec((1, D_MODEL, EXPERT_FF), w_map),
                      pl.BlockSpec((1, EXPERT_FF, D_MODEL), w_map)],
            out_specs=pl.BlockSpec((ffn_tile, D_MODEL), row_map),
            scratch_shapes=[pltpu.VMEM((D_MODEL, 2 * EXPERT_FF), BF16),
                            pltpu.VMEM((EXPERT_FF, D_MODEL), BF16)]),
        out_shape=jax.ShapeDtypeStruct((rows_alloc, D_MODEL), BF16),
        compiler_params=params,
        name="moe_ffn",
    )(tile_expert, n_used, x_sorted, w["w_e_gate"], w["w_e_up"], w["w_e_down"])

    full2 = lambda b, *_: (0, 0)
    return pl.pallas_call(
        functools.partial(_combine_kernel, n_blocks=n_blocks),
        grid_spec=pltpu.PrefetchScalarGridSpec(
            num_scalar_prefetch=5, grid=(n_blocks,),
            in_specs=[g_spec, h2_spec,
                      pl.BlockSpec((ns, ls, D_MODEL), x_map),
                      pl.BlockSpec((ns, MOD_ROWS, D_MODEL), seq_map3),
                      pl.BlockSpec((D_MODEL, 2 * SHARED_FF), full2),
                      pl.BlockSpec((SHARED_FF, D_MODEL), full2),
                      pl.BlockSpec((1, D_MODEL), full2),
                      any_spec],
            out_specs=pl.BlockSpec((ns, ls, D_MODEL), x_map),
            scratch_shapes=perm_scratch + [
                pltpu.VMEM((2, BLOCK_ROWS, D_MODEL), BF16),
                pltpu.VMEM((MOE_BLOCK, D_MODEL), F32),
                pltpu.SemaphoreType.DMA((2,))]),
        out_shape=jax.ShapeDtypeStruct(x1.shape, F32),
        compiler_params=params,
        name="moe_combine",
    )(*block_tables, gates_t, h2, x1, mod, w["w_sh_gu"], w["w_sh_down"], w["g_post2"], y_sorted)


def _prep_weights(g_pre1, g_post1, w_in, w_gk_up, b_gk, g_gmlp_v, w_s, b_s, g_gla_o, w_out, g_pre2,
                  g_post2, w_router, b_router, w_e_gate, w_e_up, w_e_down, w_sh_gate, w_sh_up,
                  w_sh_down):
    row = lambda v: v.reshape(1, -1)
    wr = jnp.pad(w_router, ((0, 0), (0, LANES - N_EXPERTS)))
    wr_top = lax.bitcast_convert_type(
        lax.bitcast_convert_type(wr, jnp.uint32) & jnp.uint32(0xFFFF0000), F32)
    wr_hi = wr_top.astype(BF16)
    wr_lo = (wr - wr_top).astype(BF16)
    wr_split = jnp.concatenate([jnp.concatenate([wr_hi, wr_lo], axis=1),
                                jnp.concatenate([wr_hi, jnp.zeros_like(wr_lo)], axis=1)], axis=0)
    return {
        "g_pre1": row(g_pre1), "g_post1": row(g_post1), "g_pre2": row(g_pre2), "g_post2": row(g_post2),
        "w_in": jnp.pad(w_in, ((0, 0), (0, GK_PAD - GLA_GATE_RANK))).astype(BF16),
        "w_gk_up": jnp.pad(w_gk_up, ((0, GK_PAD - GLA_GATE_RANK), (0, 0))).astype(BF16),
        "b_gk": row(b_gk),
        "g_gmlp_v": row(g_gmlp_v),
        "w_s": w_s,
        "b_s_full": jnp.repeat(b_s.T, GMLP_HEAD, axis=1),
        "g_gla_o": g_gla_o,
        "w_out": w_out.astype(BF16),
        "w_router": wr_split,
        "b_router": b_router.reshape(N_EXPERTS, 1),
        "w_e_gate": w_e_gate, "w_e_up": w_e_up, "w_e_down": w_e_down,
        "w_sh_gu": jnp.concatenate([w_sh_gate, w_sh_up], axis=-1).astype(BF16),
        "w_sh_down": w_sh_down.astype(BF16),
    }


PROMPT_TILE = 512
SAMPLE_TILE = 256


def _layer(x, mod, s0, w, *, emit_v):
    n_seq, seq_len, _ = x.shape
    s0_t = jnp.swapaxes(s0, -1, -2)
    if seq_len >= GMLP_CHUNK:
        outs = _mixer(x, mod, s0_t, w, seq_tile=PROMPT_TILE, chunk=GMLP_CHUNK, emit_v=emit_v)
    else:
        outs = _mixer(x, mod, s0_t, w, seq_tile=SAMPLE_TILE, chunk=seq_len, emit_v=emit_v)
    x1, h2, gates_t, st = outs[:4]
    rows_per_expert = n_seq * seq_len * TOP_K // N_EXPERTS
    y = _moe(h2, gates_t, x1, mod, w, ffn_tile=FFN_TILE if rows_per_expert >= 4 * FFN_TILE else FFN_TILE_SMALL)
    v = outs[4].reshape(n_seq, seq_len, GMLP_GROUPS, GMLP_HEAD) if emit_v else None
    return y, jnp.swapaxes(st, -1, -2), v


def kernel(x_prompt, x_sample, state_gla, c_prompt, c_sample, w_ada, b_ada, g_pre1, g_post1, w_in, w_gk_up, b_gk, g_gmlp_v, w_s, b_s, g_gla_o, w_out, g_pre2, g_post2, w_router, b_router, w_e_gate, w_e_up, w_e_down, w_sh_gate, w_sh_up, w_sh_down):
    depth = w_ada.shape[0]
    n_p, n_s = x_prompt.shape[0], x_sample.shape[0]
    c_all = jnp.concatenate([c_prompt, c_sample], axis=0)
    x_p, x_s = x_prompt, x_sample
    sp_list, ss_list, vs_list = [], [], []
    for l in range(depth):
        w = _prep_weights(g_pre1[l], g_post1[l], w_in[l], w_gk_up[l], b_gk[l], g_gmlp_v[l], w_s[l],
                          b_s[l], g_gla_o[l], w_out[l], g_pre2[l], g_post2[l], w_router[l],
                          b_router[l], w_e_gate[l], w_e_up[l], w_e_down[l], w_sh_gate[l], w_sh_up[l],
                          w_sh_down[l])
        mod = _ada(c_all, w_ada[l], b_ada[l])
        mod = jnp.pad(jnp.swapaxes(mod, 0, 1), ((0, 0), (0, MOD_ROWS - 6), (0, 0)))
        s0_p = jnp.zeros((n_p, GLA_HEADS, GLA_DK, GLA_DV), F32)
        x_p, s_p, _ = _layer(x_p, mod[:n_p], s0_p, w, emit_v=False)
        x_s, s_s, v_s = _layer(x_s, mod[n_p:], state_gla[l], w, emit_v=True)
        sp_list.append(s_p)
        ss_list.append(s_s)
        vs_list.append(v_s)
    return (x_p, x_s, jnp.stack(sp_list), jnp.stack(ss_list), jnp.stack(vs_list))
```

```python
import functools

import jax
import jax.numpy as jnp
from jax import lax
from jax.experimental import pallas as pl
from jax.experimental.pallas import tpu as pltpu

D_MODEL = 1024
GMLP_WIDTH = 512
GMLP_GROUPS = 4
GMLP_HEAD = 128
GMLP_CHUNK = 128
CAUSAL_BLOCK = 64
GLA_WIDTH = 512
GLA_HEADS = 4
GLA_DV = 128
GLA_DK = 64
GLA_KEY_WIDTH = 256
GLA_GATE_RANK = 16
GLA_GATE_NORMALIZER = 16.0
N_EXPERTS = 64
TOP_K = 8
N_EXPERT_GROUPS = 8
GROUP_SIZE = N_EXPERTS // N_EXPERT_GROUPS
TOPK_GROUPS = 4
EXPERT_FF = 256
SHARED_FF = 256
ROUTED_SCALE = 2.5
EPS = 1e-6

LANES = 128
GK_PAD = LANES
IN_WIDTH_PAD = 2 * GMLP_WIDTH + 2 * GLA_KEY_WIDTH + 2 * GLA_WIDTH + GK_PAD
OFF_U = 0
OFF_VG = GMLP_WIDTH
OFF_Q = 2 * GMLP_WIDTH
OFF_K = OFF_Q + GLA_KEY_WIDTH
OFF_VL = OFF_K + GLA_KEY_WIDTH
OFF_R = OFF_VL + GLA_WIDTH
OFF_GK = OFF_R + GLA_WIDTH
MOD_ROWS = 8
VMEM_LIMIT = 56 * 1024 * 1024

F32 = jnp.float32
BF16 = jnp.bfloat16
NT_DIMS = (((1,), (1,)), ((), ()))
TN_DIMS = (((0,), (0,)), ((), ()))


def _rms(x, g):
    return x * lax.rsqrt(jnp.mean(x * x, axis=-1, keepdims=True) + EPS) * g


def _gelu(x):
    return 0.5 * x * (1.0 + jnp.tanh(0.7978845608028654 * (x + 0.044715 * (x * x * x))))


def _sigmoid(x):
    return 1.0 / (1.0 + jnp.exp(-x))


def _silu(x):
    return x * _sigmoid(x)


def _ada_kernel(c_ref, w_ref, b_ref, o_ref):
    a = _silu(c_ref[...])
    o_ref[0] = jnp.dot(a, w_ref[...], precision=lax.Precision.HIGHEST,
                       preferred_element_type=F32) + b_ref[0]


def _ada(c_all, w_ada, b_ada):
    n = c_all.shape[0]
    return pl.pallas_call(
        _ada_kernel,
        grid=(6,),
        in_specs=[pl.BlockSpec((n, D_MODEL), lambda j: (0, 0)),
                  pl.BlockSpec((D_MODEL, D_MODEL), lambda j: (0, j)),
                  pl.BlockSpec((1, 1, D_MODEL), lambda j: (j, 0, 0))],
        out_specs=pl.BlockSpec((1, n, D_MODEL), lambda j: (j, 0, 0)),
        out_shape=jax.ShapeDtypeStruct((6, n, D_MODEL), F32),
        compiler_params=pltpu.CompilerParams(vmem_limit_bytes=VMEM_LIMIT),
        name="ada",
    )(c_all, w_ada, b_ada.reshape(6, 1, D_MODEL))


def _route(logits_t, bias_t):
    t = logits_t.shape[1]
    scores = _sigmoid(logits_t)
    sel = scores + bias_t
    sub = lax.broadcasted_iota(jnp.int32, (GROUP_SIZE, t), 0)
    gscore = []
    for g in range(N_EXPERT_GROUPS):
        blk = sel[g * GROUP_SIZE:(g + 1) * GROUP_SIZE]
        m1 = jnp.max(blk, axis=0, keepdims=True)
        first = jnp.min(jnp.where(blk == m1, sub, GROUP_SIZE), axis=0, keepdims=True)
        m2 = jnp.max(jnp.where(sub == first, -jnp.inf, blk), axis=0, keepdims=True)
        gscore.append(m1 + m2)
    neg = jnp.full((GROUP_SIZE, t), -jnp.inf, F32)
    masked = []
    for g in range(N_EXPERT_GROUPS):
        rank = jnp.zeros((1, t), jnp.int32)
        for o in range(N_EXPERT_GROUPS):
            if o == g:
                continue
            ahead = (gscore[o] >= gscore[g]) if o < g else (gscore[o] > gscore[g])
            rank = rank + jnp.where(ahead, 1, 0)
        keep = jnp.broadcast_to(rank < TOPK_GROUPS, (GROUP_SIZE, t))
        masked.append(jnp.where(keep, sel[g * GROUP_SIZE:(g + 1) * GROUP_SIZE], neg))
    selm = jnp.concatenate(masked, axis=0)
    eidx = lax.broadcasted_iota(jnp.int32, (N_EXPERTS, t), 0)
    picked = jnp.zeros((N_EXPERTS, t), F32)
    for _ in range(TOP_K):
        best = jnp.max(selm, axis=0, keepdims=True)
        first = jnp.min(jnp.where(selm == best, eidx, N_EXPERTS), axis=0, keepdims=True)
        hit = eidx == first
        picked = jnp.where(hit, 1.0, picked)
        selm = jnp.where(hit, -jnp.inf, selm)
    chosen = jnp.where(picked > 0.0, scores, 0.0)
    denom = jnp.sum(chosen, axis=0, keepdims=True)
    return chosen * (ROUTED_SCALE / denom)


def _mixer_kernel(x_ref, mod_ref, s0_ref, gpre1_ref, win_ref, wgk_ref, bgk_ref, ggv_ref,
                  ws_ref, bsf_ref, ggo_ref, wout_ref, gpost1_ref, gpre2_ref, wr_ref, br_ref,
                  x1_ref, h2_ref, gates_ref, st_ref, *rest,
                  chunk, n_chunks, chunk_is_seq, emit_v):
    if emit_v:
        v_ref, *rest = rest
    proj, mix_scr = rest
    ns, ls, _ = x_ref.shape
    tm = ns * ls

    mod_row = lambda i: mod_ref[:, i:i + 1, :]

    h = _rms(x_ref[...], gpre1_ref[...]) * (1.0 + mod_row(1)) + mod_row(0)
    proj[...] = jnp.dot(h.reshape(tm, D_MODEL).astype(BF16), win_ref[...],
                        preferred_element_type=F32)

    if chunk_is_seq:
        st_ref[...] = s0_ref[...]
    else:
        @pl.when(pl.program_id(1) == 0)
        def _():
            st_ref[...] = s0_ref[...]

    row = lax.broadcasted_iota(jnp.int32, (chunk, chunk), 0)
    col = lax.broadcasted_iota(jnp.int32, (chunk, chunk), 1)
    tri = (row >= col).astype(F32)
    causal = row >= col
    block_causal = (row // CAUSAL_BLOCK) >= (col // CAUSAL_BLOCK)
    wmix = [jnp.where(block_causal, ws_ref[g, 0:chunk, 0:chunk], 0.0).astype(BF16)
            for g in range(GMLP_GROUPS)]
    mid = chunk // 2

    def chunk_body(c, carry):
        rows = pl.ds(pl.multiple_of(c * chunk, chunk), chunk)
        sidx = c if chunk_is_seq else 0

        vg = _gelu(proj[rows, OFF_VG:OFF_VG + GMLP_WIDTH])
        vn = _rms(vg, ggv_ref[...])
        if emit_v:
            v_ref[rows, :] = vn
        vnb = vn.astype(BF16)
        mixed = jnp.concatenate(
            [jnp.dot(wmix[g], vnb[:, g * GMLP_HEAD:(g + 1) * GMLP_HEAD], preferred_element_type=F32)
             for g in range(GMLP_GROUPS)], axis=-1) + bsf_ref[0:chunk, :]
        u = _gelu(proj[rows, OFF_U:OFF_U + GMLP_WIDTH])
        mix_scr[rows, 0:GMLP_WIDTH] = (u * mixed).astype(BF16)

        q = proj[rows, OFF_Q:OFF_Q + GLA_KEY_WIDTH] * (GLA_DK ** -0.5)
        k = proj[rows, OFF_K:OFF_K + GLA_KEY_WIDTH]
        gk = proj[rows, OFF_GK:OFF_GK + GK_PAD]
        z = jnp.dot(gk.astype(BF16), wgk_ref[...], preferred_element_type=F32) + bgk_ref[...]
        la = (jnp.minimum(z, 0.0) - jnp.log(1.0 + jnp.exp(-jnp.abs(z)))) * (1.0 / GLA_GATE_NORMALIZER)
        b = jnp.dot(tri, la, precision=lax.Precision.HIGHEST, preferred_element_type=F32)
        b_mid = b[mid - 1:mid]
        b_end = b[chunk - 1:chunk]
        qt = (q * jnp.exp(b - b_mid)).astype(BF16)
        kt = (k * jnp.exp(b_mid - b)).astype(BF16)
        qs = (q * jnp.exp(b)).astype(BF16)
        kd = (k * jnp.exp(b_end - b)).astype(BF16)
        decay = jnp.exp(b_end)
        for hd in range(GLA_HEADS):
            ksl = slice(hd * GLA_DK, (hd + 1) * GLA_DK)
            vsl = slice(OFF_VL + hd * GLA_DV, OFF_VL + (hd + 1) * GLA_DV)
            rsl = slice(OFF_R + hd * GLA_DV, OFF_R + (hd + 1) * GLA_DV)
            att = lax.dot_general(qt[:, ksl], kt[:, ksl], NT_DIMS, preferred_element_type=F32)
            att = jnp.where(causal, att, 0.0).astype(BF16)
            vh = proj[rows, vsl].astype(BF16)
            st = st_ref[sidx, hd]
            o = jnp.dot(att, vh, preferred_element_type=F32) + lax.dot_general(
                qs[:, ksl], st.astype(BF16), NT_DIMS, preferred_element_type=F32)
            on = _rms(o, ggo_ref[hd:hd + 1, :])
            mix_scr[rows, GMLP_WIDTH + hd * GLA_DV:GMLP_WIDTH + (hd + 1) * GLA_DV] = (
                on * _silu(proj[rows, rsl])).astype(BF16)
            st_ref[sidx, hd] = st * decay[:, ksl] + lax.dot_general(
                vh, kd[:, ksl], TN_DIMS, preferred_element_type=F32)
        return carry

    lax.fori_loop(0, n_chunks, chunk_body, 0, unroll=4)

    mixo = jnp.dot(mix_scr[...], wout_ref[...], preferred_element_type=F32).reshape(ns, ls, D_MODEL)
    x1 = x_ref[...] + mod_row(2) * _rms(mixo, gpost1_ref[...])
    x1_ref[...] = x1
    h2 = (_rms(x1, gpre2_ref[...]) * (1.0 + mod_row(4)) + mod_row(3)).reshape(tm, D_MODEL)
    h2_hi = h2.astype(BF16)
    h2_ref[...] = h2_hi

    h2_lo = (h2 - h2_hi.astype(F32)).astype(BF16)
    parts = jnp.dot(jnp.concatenate([h2_hi, h2_lo], axis=-1), wr_ref[...], preferred_element_type=F32)
    logits = parts[:, 0:LANES] + parts[:, LANES:2 * LANES]
    gates_ref[...] = _route(logits.T[0:N_EXPERTS], br_ref[...])


def _mixer(x, mod, s0_t, w, *, seq_tile, chunk, emit_v):
    n_seq, seq_len, _ = x.shape
    chunk_is_seq = seq_len == chunk
    if chunk_is_seq:
        ns, ls = seq_tile // chunk, chunk
        grid = (1, n_seq // ns)
        x_map = lambda b, t: (t, 0, 0)
        seq_map3 = lambda b, t: (t, 0, 0)
        seq_map4 = lambda b, t: (t, 0, 0, 0)
    else:
        ns, ls = 1, seq_tile
        grid = (n_seq, seq_len // seq_tile)
        x_map = lambda b, t: (b, t, 0)
        seq_map3 = lambda b, t: (b, 0, 0)
        seq_map4 = lambda b, t: (b, 0, 0, 0)
    tm = ns * ls
    n_tok = n_seq * seq_len
    tok_block = lambda b, t: b * grid[1] + t
    tok_map = lambda b, t: (tok_block(b, t), 0)
    full2 = lambda b, t: (0, 0)
    full3 = lambda b, t: (0, 0, 0)

    in_specs = [
        pl.BlockSpec((ns, ls, D_MODEL), x_map),
        pl.BlockSpec((ns, MOD_ROWS, D_MODEL), seq_map3),
        pl.BlockSpec((ns, GLA_HEADS, GLA_DV, GLA_DK), seq_map4),
        pl.BlockSpec((1, D_MODEL), full2),
        pl.BlockSpec((D_MODEL, IN_WIDTH_PAD), full2),
        pl.BlockSpec((GK_PAD, GLA_KEY_WIDTH), full2),
        pl.BlockSpec((1, GLA_KEY_WIDTH), full2),
        pl.BlockSpec((1, GMLP_WIDTH), full2),
        pl.BlockSpec((GMLP_GROUPS, GMLP_CHUNK, GMLP_CHUNK), full3),
        pl.BlockSpec((GMLP_CHUNK, GMLP_WIDTH), full2),
        pl.BlockSpec((GLA_HEADS, GLA_DV), full2),
        pl.BlockSpec((D_MODEL, D_MODEL), full2),
        pl.BlockSpec((1, D_MODEL), full2),
        pl.BlockSpec((1, D_MODEL), full2),
        pl.BlockSpec((2 * D_MODEL, 2 * LANES), full2),
        pl.BlockSpec((N_EXPERTS, 1), full2),
    ]
    out_specs = [
        pl.BlockSpec((ns, ls, D_MODEL), x_map),
        pl.BlockSpec((tm, D_MODEL), tok_map),
        pl.BlockSpec((N_EXPERTS, tm), lambda b, t: (0, tok_block(b, t))),
        pl.BlockSpec((ns, GLA_HEADS, GLA_DV, GLA_DK), seq_map4),
    ]
    out_shape = [
        jax.ShapeDtypeStruct((n_seq, seq_len, D_MODEL), F32),
        jax.ShapeDtypeStruct((n_tok, D_MODEL), BF16),
        jax.ShapeDtypeStruct((N_EXPERTS, n_tok), F32),
        jax.ShapeDtypeStruct((n_seq, GLA_HEADS, GLA_DV, GLA_DK), F32),
    ]
    if emit_v:
        out_specs.append(pl.BlockSpec((tm, GMLP_WIDTH), tok_map))
        out_shape.append(jax.ShapeDtypeStruct((n_tok, GMLP_WIDTH), F32))

    kern = functools.partial(_mixer_kernel, chunk=chunk, n_chunks=tm // chunk,
                             chunk_is_seq=chunk_is_seq, emit_v=emit_v)
    return pl.pallas_call(
        kern,
        grid=grid,
        in_specs=in_specs,
        out_specs=out_specs,
        out_shape=out_shape,
        scratch_shapes=[pltpu.VMEM((tm, IN_WIDTH_PAD), F32), pltpu.VMEM((tm, D_MODEL), BF16)],
        compiler_params=pltpu.CompilerParams(
            dimension_semantics=("arbitrary", "arbitrary"), vmem_limit_bytes=VMEM_LIMIT),
        name="mixer_sample" if chunk_is_seq else "mixer_prompt",
    )(x, mod, s0_t, w["g_pre1"], w["w_in"], w["w_gk_up"], w["b_gk"], w["g_gmlp_v"], w["w_s"],
      w["b_s_full"], w["g_gla_o"], w["w_out"], w["g_post1"], w["g_pre2"], w["w_router"],
      w["b_router"])


MOE_BLOCK = 256
SEG_ALIGN = 16
WIN = 64
FFN_TILE = 1024
BLOCK_ROWS = MOE_BLOCK * TOP_K + SEG_ALIGN * N_EXPERTS + WIN
MAIN_ROWS = 2688
COMBINE_SLOTS = 3
COL_CHUNK = 256
COUNT_LANES = LANES


def _count_kernel(g_ref, o_ref):
    n = g_ref.shape[1]
    selb = jnp.where(g_ref[...] > 0.0, 1.0, 0.0).astype(BF16)
    tok = lax.broadcasted_iota(jnp.int32, (n, COUNT_LANES), 0)
    lane = lax.broadcasted_iota(jnp.int32, (n, COUNT_LANES), 1)
    ind = jnp.where(tok // MOE_BLOCK == lane, 1.0, 0.0).astype(BF16)
    o_ref[0] = jnp.dot(selb, ind, preferred_element_type=F32)


def _fill_permutation(p_scr, rk_scr, g_ref, cntp_ref, loff_ref, long_ref, blk, weighted):
    tb = g_ref.shape[1]
    g = g_ref[...]
    sel = g > 0.0
    r_i = lax.broadcasted_iota(jnp.int32, (tb, tb), 0)
    c_i = lax.broadcasted_iota(jnp.int32, (tb, tb), 1)
    earlier = jnp.where(r_i < c_i, 1.0, 0.0).astype(BF16)
    rank = jnp.dot(jnp.where(sel, 1.0, 0.0).astype(BF16), earlier, preferred_element_type=F32)
    rk = jnp.where(sel, rank, -1.0)
    p_scr[...] = jnp.zeros_like(p_scr)
    jwin = lax.broadcasted_iota(jnp.int32, (WIN, tb), 0).astype(F32)

    for e in range(N_EXPERTS):
        off = pl.multiple_of(loff_ref[blk * N_EXPERTS + e], SEG_ALIGN)
        val = g[e:e + 1] if weighted else 1.0
        p_scr[pl.ds(off, WIN), :] = jnp.where(rk[e:e + 1] == jwin, val, 0.0).astype(BF16)

    @pl.when(long_ref[blk] > 0)
    def _():
        rk_scr[...] = rk

        def expert_body(e, c):
            n = cntp_ref[blk * N_EXPERTS + e]
            off = loff_ref[blk * N_EXPERTS + e]
            rke = rk_scr[pl.ds(e, 1), :]
            val = g_ref[pl.ds(e, 1), :] if weighted else 1.0

            def win_body(wi, c2):
                rows = pl.ds(pl.multiple_of(off + wi * WIN, SEG_ALIGN), WIN)
                j = jwin + (wi * WIN).astype(F32)
                new = jnp.where(rke == j, val, 0.0)
                p_scr[rows, :] = jnp.where(j < n.astype(F32), new, p_scr[rows, :].astype(F32)).astype(BF16)
                return c2

            lax.fori_loop(1, (n + WIN - 1) // WIN, win_body, 0)
            return c

        lax.fori_loop(0, N_EXPERTS, expert_body, 0)


def _start_segments(cntp_ref, loff_ref, blk, make_copy):
    for e in range(N_EXPERTS):
        make_copy(e, pl.multiple_of(loff_ref[blk * N_EXPERTS + e], SEG_ALIGN),
                  pl.multiple_of(cntp_ref[blk * N_EXPERTS + e], SEG_ALIGN)).start()


def _dispatch_kernel(cntp_ref, loff_ref, goff_ref, rows_ref, long_ref, toff_ref, tlen_ref,
                     ga_ref, gb_ref, h2a_ref, h2b_ref, xs_hbm,
                     p_scr, rk_scr, g_scr, h2_scr, xs_scr, zero_scr, sems, tail_sem,
                     *, n_blocks, n_blocks_a):
    b = pl.program_id(0)
    slot = lax.rem(b, 2)
    from_a = b < n_blocks_a
    g_scr[...] = jnp.where(from_a, ga_ref[...], gb_ref[...])
    h2_scr[...] = jnp.where(from_a, h2a_ref[...], h2b_ref[...])
    g_ref, h2_ref = g_scr, h2_scr

    def seg_copy(blk, s):
        def make(e, off, n):
            dst = pl.multiple_of(goff_ref[blk * N_EXPERTS + e], SEG_ALIGN)
            return pltpu.make_async_copy(xs_scr.at[s, pl.ds(off, n)], xs_hbm.at[pl.ds(dst, n)], sems.at[s])
        return make

    def wait_block(blk, s):
        n = pl.multiple_of(rows_ref[blk], SEG_ALIGN)
        pltpu.make_async_copy(xs_scr.at[s, pl.ds(0, n)], xs_hbm.at[pl.ds(0, n)], sems.at[s]).wait()

    def for_tails(action):
        def body(e, c):
            n = tlen_ref[e]

            @pl.when(n > 0)
            def _():
                dst = pl.multiple_of(toff_ref[e], SEG_ALIGN)
                nn = pl.multiple_of(n, SEG_ALIGN)
                action(pltpu.make_async_copy(zero_scr.at[pl.ds(0, nn)], xs_hbm.at[pl.ds(dst, nn)], tail_sem))

            return c

        lax.fori_loop(0, N_EXPERTS, body, 0)

    @pl.when(b == 0)
    def _():
        zero_scr[...] = jnp.zeros_like(zero_scr)
        for_tails(lambda cp: cp.start())

    @pl.when(b >= 2)
    def _():
        wait_block(b - 2, slot)

    _fill_permutation(p_scr, rk_scr, g_ref, cntp_ref, loff_ref, long_ref, b, weighted=False)
    def sort_rows(rows):
        for c in range(D_MODEL // COL_CHUNK):
            cols = slice(c * COL_CHUNK, (c + 1) * COL_CHUNK)
            xs = jnp.dot(p_scr[rows, :], h2_ref[:, cols], preferred_element_type=F32)
            xs_scr[slot, rows, cols] = xs.astype(BF16)

    sort_rows(slice(0, MAIN_ROWS))

    @pl.when(rows_ref[b] > MAIN_ROWS)
    def _():
        sort_rows(slice(MAIN_ROWS, BLOCK_ROWS))

    _start_segments(cntp_ref, loff_ref, b, seg_copy(b, slot))

    @pl.when(b == n_blocks - 1)
    def _():
        for_tails(lambda cp: cp.wait())
        if n_blocks >= 2:
            wait_block(b - 1, 1 - slot)
        wait_block(b, slot)


def _ffn_kernel(te_ref, nu_ref, x_ref, wg_ref, wu_ref, wd_ref, y_ref, wgu_scr, wd_scr):
    i = pl.program_id(0)

    @pl.when(i < nu_ref[0])
    def _():
        @pl.when(jnp.logical_or(i == 0, te_ref[i] != te_ref[jnp.maximum(i - 1, 0)]))
        def _():
            wgu_scr[:, 0:EXPERT_FF] = wg_ref[0].astype(BF16)
            wgu_scr[:, EXPERT_FF:2 * EXPERT_FF] = wu_ref[0].astype(BF16)
            wd_scr[...] = wd_ref[0].astype(BF16)

        gu = jnp.dot(x_ref[...], wgu_scr[...], preferred_element_type=F32)
        hid = _silu(gu[:, 0:EXPERT_FF]) * gu[:, EXPERT_FF:2 * EXPERT_FF]
        y_ref[...] = jnp.dot(hid.astype(BF16), wd_scr[...], preferred_element_type=F32).astype(BF16)


def _combine_kernel(cntp_ref, loff_ref, goff_ref, rows_ref, long_ref,
                    ga_ref, gb_ref, h2a_ref, h2b_ref, x1a_ref, x1b_ref, moda_ref, modb_ref,
                    wsgu_ref, wsd_ref, gpost2_ref, ys_hbm, outa_ref, outb_ref,
                    p_scr, rk_scr, g_scr, ys_scr, routed_scr, sems, *, n_blocks, n_blocks_a):
    b = pl.program_id(0)
    slot = lax.rem(b, COMBINE_SLOTS)
    from_a = b < n_blocks_a
    g_scr[...] = jnp.where(from_a, ga_ref[...], gb_ref[...])

    def seg_copy(blk, s):
        def make(e, off, n):
            src = pl.multiple_of(goff_ref[blk * N_EXPERTS + e], SEG_ALIGN)
            return pltpu.make_async_copy(ys_hbm.at[pl.ds(src, n)], ys_scr.at[s, pl.ds(off, n)], sems.at[s])
        return make

    @pl.when(b == 0)
    def _():
        ys_scr[...] = jnp.zeros_like(ys_scr)
        for ahead in range(min(COMBINE_SLOTS - 1, n_blocks)):
            _start_segments(cntp_ref, loff_ref, ahead, seg_copy(ahead, ahead))

    ahead = b + COMBINE_SLOTS - 1

    @pl.when(ahead < n_blocks)
    def _():
        _start_segments(cntp_ref, loff_ref, ahead, seg_copy(ahead, lax.rem(ahead, COMBINE_SLOTS)))

    _fill_permutation(p_scr, rk_scr, g_scr, cntp_ref, loff_ref, long_ref, b, weighted=True)
    n_rows = pl.multiple_of(rows_ref[b], SEG_ALIGN)
    pltpu.make_async_copy(ys_hbm.at[pl.ds(0, n_rows)], ys_scr.at[slot, pl.ds(0, n_rows)], sems.at[slot]).wait()

    def unsort_rows(rows):
        return lax.dot_general(p_scr[rows, :], ys_scr[slot, rows, :], TN_DIMS, preferred_element_type=F32)

    routed_scr[...] = unsort_rows(slice(0, MAIN_ROWS))

    @pl.when(rows_ref[b] > MAIN_ROWS)
    def _():
        routed_scr[...] += unsort_rows(slice(MAIN_ROWS, BLOCK_ROWS))

    hx = jnp.where(from_a, h2a_ref[...], h2b_ref[...])
    sgu = jnp.dot(hx, wsgu_ref[...], preferred_element_type=F32)
    shid = _silu(sgu[:, 0:SHARED_FF]) * sgu[:, SHARED_FF:2 * SHARED_FF]
    f = routed_scr[...] + jnp.dot(shid.astype(BF16), wsd_ref[...], preferred_element_type=F32)
    routed_scr[...] = _rms(f, gpost2_ref[...])

    def finish(x1_ref, mod_ref, out_ref):
        gate2 = mod_ref[:, 5:6, :]
        out_ref[...] = x1_ref[...] + gate2 * routed_scr[...].reshape(x1_ref.shape)

    @pl.when(from_a)
    def _():
        finish(x1a_ref, moda_ref, outa_ref)

    @pl.when(jnp.logical_not(from_a))
    def _():
        finish(x1b_ref, modb_ref, outb_ref)


def _segment_plan(cnt, n_tiles_max, ffn_tile):
    cntp = jnp.maximum((cnt + SEG_ALIGN - 1) // SEG_ALIGN, 1) * SEG_ALIGN
    loff = jnp.cumsum(cntp, axis=1) - cntp
    block_rows = jnp.sum(cntp, axis=1)
    has_long = jnp.max(cntp, axis=1) > WIN
    tot = jnp.sum(cntp, axis=0)
    reg = (tot + ffn_tile - 1) // ffn_tile * ffn_tile
    base = jnp.cumsum(reg) - reg
    goff = base[None, :] + jnp.cumsum(cntp, axis=0) - cntp
    tile_end = jnp.cumsum(reg // ffn_tile)
    tile = jnp.arange(n_tiles_max, dtype=jnp.int32)
    tile_expert = jnp.minimum(jnp.sum(tile_end[None, :] <= tile[:, None], axis=1), N_EXPERTS - 1)
    i32 = lambda a: a.astype(jnp.int32)
    block_tables = (i32(cntp.reshape(-1)), i32(loff.reshape(-1)), i32(goff.reshape(-1)),
                    i32(block_rows), i32(has_long))
    return block_tables, i32(base + tot), i32(reg - tot), i32(tile_expert), i32(tile_end[-1:])


def _count_blocks(gates_t, params):
    n_tok = gates_t.shape[1]
    chunk = min(n_tok, COUNT_LANES * MOE_BLOCK // 16)
    per_chunk = chunk // MOE_BLOCK
    cnt = pl.pallas_call(
        _count_kernel,
        grid=(n_tok // chunk,),
        in_specs=[pl.BlockSpec((N_EXPERTS, chunk), lambda i: (0, i))],
        out_specs=pl.BlockSpec((1, N_EXPERTS, COUNT_LANES), lambda i: (i, 0, 0)),
        out_shape=jax.ShapeDtypeStruct((n_tok // chunk, N_EXPERTS, COUNT_LANES), F32),
        compiler_params=params,
        name="moe_count",
    )(gates_t)
    return jnp.swapaxes(cnt[:, :, :per_chunk], 1, 2).reshape(n_tok // MOE_BLOCK, N_EXPERTS).astype(jnp.int32)


def _stream_specs(x1, block_of):
    n_seq, seq_len, _ = x1.shape
    if seq_len < MOE_BLOCK:
        ns, ls = MOE_BLOCK // seq_len, seq_len
        x_map = lambda b, *_: (block_of(b), 0, 0)
        mod_map = x_map
    else:
        ns, ls = 1, MOE_BLOCK
        per_seq = seq_len // MOE_BLOCK
        x_map = lambda b, *_: (block_of(b) // per_seq, block_of(b) % per_seq, 0)
        mod_map = lambda b, *_: (block_of(b) // per_seq, 0, 0)
    return pl.BlockSpec((ns, ls, D_MODEL), x_map), pl.BlockSpec((ns, MOD_ROWS, D_MODEL), mod_map)


def _moe(stream_a, stream_b, w):
    (h2_a, g_a, x1_a, mod_a), (h2_b, g_b, x1_b, mod_b) = stream_a, stream_b
    nb_a, nb_b = h2_a.shape[0] // MOE_BLOCK, h2_b.shape[0] // MOE_BLOCK
    n_blocks = nb_a + nb_b
    n_tok = n_blocks * MOE_BLOCK
    ffn_tile = FFN_TILE
    rows_max = (n_tok * TOP_K + n_blocks * N_EXPERTS * SEG_ALIGN
                + N_EXPERTS * (ffn_tile - SEG_ALIGN))
    n_tiles_max = -(-rows_max // ffn_tile)
    rows_alloc = n_tiles_max * ffn_tile
    params = pltpu.CompilerParams(dimension_semantics=("arbitrary",), vmem_limit_bytes=VMEM_LIMIT)

    cnt = jnp.concatenate([_count_blocks(g_a, params), _count_blocks(g_b, params)], axis=0)
    block_tables, tail_off, tail_len, tile_expert, n_used = _segment_plan(cnt, n_tiles_max, ffn_tile)

    in_a = lambda b: jnp.minimum(b, nb_a - 1)
    in_b = lambda b: jnp.maximum(b - nb_a, 0)
    tok_specs = [pl.BlockSpec((N_EXPERTS, MOE_BLOCK), lambda b, *_: (0, in_a(b))),
                 pl.BlockSpec((N_EXPERTS, MOE_BLOCK), lambda b, *_: (0, in_b(b))),
                 pl.BlockSpec((MOE_BLOCK, D_MODEL), lambda b, *_: (in_a(b), 0)),
                 pl.BlockSpec((MOE_BLOCK, D_MODEL), lambda b, *_: (in_b(b), 0))]
    any_spec = pl.BlockSpec(memory_space=pl.ANY)
    perm_scratch = [pltpu.VMEM((BLOCK_ROWS, MOE_BLOCK), BF16), pltpu.VMEM((N_EXPERTS, MOE_BLOCK), F32),
                    pltpu.VMEM((N_EXPERTS, MOE_BLOCK), F32)]

    x_sorted = pl.pallas_call(
        functools.partial(_dispatch_kernel, n_blocks=n_blocks, n_blocks_a=nb_a),
        grid_spec=pltpu.PrefetchScalarGridSpec(
            num_scalar_prefetch=7, grid=(n_blocks,),
            in_specs=tok_specs, out_specs=any_spec,
            scratch_shapes=perm_scratch + [
                pltpu.VMEM((MOE_BLOCK, D_MODEL), BF16),
                pltpu.VMEM((2, BLOCK_ROWS, D_MODEL), BF16),
                pltpu.VMEM((ffn_tile, D_MODEL), BF16),
                pltpu.SemaphoreType.DMA((2,)), pltpu.SemaphoreType.DMA(())]),
        out_shape=jax.ShapeDtypeStruct((rows_alloc, D_MODEL), BF16),
        compiler_params=params,
        name="moe_dispatch",
    )(*block_tables, tail_off, tail_len, g_a, g_b, h2_a, h2_b)

    last_used = lambda i, nu: jnp.maximum(jnp.minimum(i, nu[0] - 1), 0)
    row_map = lambda i, te, nu: (last_used(i, nu), 0)
    w_map = lambda i, te, nu: (te[last_used(i, nu)], 0, 0)
    y_sorted = pl.pallas_call(
        _ffn_kernel,
        grid_spec=pltpu.PrefetchScalarGridSpec(
            num_scalar_prefetch=2, grid=(n_tiles_max,),
            in_specs=[pl.BlockSpec((ffn_tile, D_MODEL), row_map),
                      pl.BlockSpec((1, D_MODEL, EXPERT_FF), w_map),
                      pl.BlockSpec((1, D_MODEL, EXPERT_FF), w_map),
                      pl.BlockSpec((1, EXPERT_FF, D_MODEL), w_map)],
            out_specs=pl.BlockSpec((ffn_tile, D_MODEL), row_map),
            scratch_shapes=[pltpu.VMEM((D_MODEL, 2 * EXPERT_FF), BF16),
                            pltpu.VMEM((EXPERT_FF, D_MODEL), BF16)]),
        out_shape=jax.ShapeDtypeStruct((rows_alloc, D_MODEL), BF16),
        compiler_params=params,
        name="moe_ffn",
    )(tile_expert, n_used, x_sorted, w["w_e_gate"], w["w_e_up"], w["w_e_down"])

    full2 = lambda b, *_: (0, 0)
    x_spec_a, mod_spec_a = _stream_specs(x1_a, in_a)
    x_spec_b, mod_spec_b = _stream_specs(x1_b, in_b)
    return pl.pallas_call(
        functools.partial(_combine_kernel, n_blocks=n_blocks, n_blocks_a=nb_a),
        grid_spec=pltpu.PrefetchScalarGridSpec(
            num_scalar_prefetch=5, grid=(n_blocks,),
            in_specs=tok_specs + [
                x_spec_a, x_spec_b, mod_spec_a, mod_spec_b,
                pl.BlockSpec((D_MODEL, 2 * SHARED_FF), full2),
                pl.BlockSpec((SHARED_FF, D_MODEL), full2),
                pl.BlockSpec((1, D_MODEL), full2),
                any_spec],
            out_specs=[x_spec_a, x_spec_b],
            scratch_shapes=perm_scratch + [
                pltpu.VMEM((COMBINE_SLOTS, BLOCK_ROWS, D_MODEL), BF16),
                pltpu.VMEM((MOE_BLOCK, D_MODEL), F32),
                pltpu.SemaphoreType.DMA((COMBINE_SLOTS,))]),
        out_shape=[jax.ShapeDtypeStruct(x1_a.shape, F32), jax.ShapeDtypeStruct(x1_b.shape, F32)],
        compiler_params=params,
        name="moe_combine",
    )(*block_tables, g_a, g_b, h2_a, h2_b, x1_a, x1_b, mod_a, mod_b,
      w["w_sh_gu"], w["w_sh_down"], w["g_post2"], y_sorted)


def _prep_weights(g_pre1, g_post1, w_in, w_gk_up, b_gk, g_gmlp_v, w_s, b_s, g_gla_o, w_out, g_pre2,
                  g_post2, w_router, b_router, w_e_gate, w_e_up, w_e_down, w_sh_gate, w_sh_up,
                  w_sh_down):
    row = lambda v: v.reshape(1, -1)
    wr = jnp.pad(w_router, ((0, 0), (0, LANES - N_EXPERTS)))
    wr_top = lax.bitcast_convert_type(
        lax.bitcast_convert_type(wr, jnp.uint32) & jnp.uint32(0xFFFF0000), F32)
    wr_hi = wr_top.astype(BF16)
    wr_lo = (wr - wr_top).astype(BF16)
    wr_split = jnp.concatenate([jnp.concatenate([wr_hi, wr_lo], axis=1),
                                jnp.concatenate([wr_hi, jnp.zeros_like(wr_lo)], axis=1)], axis=0)
    return {
        "g_pre1": row(g_pre1), "g_post1": row(g_post1), "g_pre2": row(g_pre2), "g_post2": row(g_post2),
        "w_in": jnp.pad(w_in, ((0, 0), (0, GK_PAD - GLA_GATE_RANK))).astype(BF16),
        "w_gk_up": jnp.pad(w_gk_up, ((0, GK_PAD - GLA_GATE_RANK), (0, 0))).astype(BF16),
        "b_gk": row(b_gk),
        "g_gmlp_v": row(g_gmlp_v),
        "w_s": w_s,
        "b_s_full": jnp.repeat(b_s.T, GMLP_HEAD, axis=1),
        "g_gla_o": g_gla_o,
        "w_out": w_out.astype(BF16),
        "w_router": wr_split,
        "b_router": b_router.reshape(N_EXPERTS, 1),
        "w_e_gate": w_e_gate, "w_e_up": w_e_up, "w_e_down": w_e_down,
        "w_sh_gu": jnp.concatenate([w_sh_gate, w_sh_up], axis=-1).astype(BF16),
        "w_sh_down": w_sh_down.astype(BF16),
    }


PROMPT_TILE = 512
SAMPLE_TILE = 256


def _mix(x, mod, s0, w, *, emit_v):
    n_seq, seq_len, _ = x.shape
    s0_t = jnp.swapaxes(s0, -1, -2)
    if seq_len >= GMLP_CHUNK:
        outs = _mixer(x, mod, s0_t, w, seq_tile=PROMPT_TILE, chunk=GMLP_CHUNK, emit_v=emit_v)
    else:
        outs = _mixer(x, mod, s0_t, w, seq_tile=SAMPLE_TILE, chunk=seq_len, emit_v=emit_v)
    x1, h2, gates_t, st = outs[:4]
    v = outs[4].reshape(n_seq, seq_len, GMLP_GROUPS, GMLP_HEAD) if emit_v else None
    return (h2, gates_t, x1, mod), jnp.swapaxes(st, -1, -2), v


def kernel(x_prompt, x_sample, state_gla, c_prompt, c_sample, w_ada, b_ada, g_pre1, g_post1, w_in, w_gk_up, b_gk, g_gmlp_v, w_s, b_s, g_gla_o, w_out, g_pre2, g_post2, w_router, b_router, w_e_gate, w_e_up, w_e_down, w_sh_gate, w_sh_up, w_sh_down):
    depth = w_ada.shape[0]
    n_p, n_s = x_prompt.shape[0], x_sample.shape[0]
    c_all = jnp.concatenate([c_prompt, c_sample], axis=0)
    x_p, x_s = x_prompt, x_sample
    sp_list, ss_list, vs_list = [], [], []
    for l in range(depth):
        w = _prep_weights(g_pre1[l], g_post1[l], w_in[l], w_gk_up[l], b_gk[l], g_gmlp_v[l], w_s[l],
                          b_s[l], g_gla_o[l], w_out[l], g_pre2[l], g_post2[l], w_router[l],
                          b_router[l], w_e_gate[l], w_e_up[l], w_e_down[l], w_sh_gate[l], w_sh_up[l],
                          w_sh_down[l])
        mod = _ada(c_all, w_ada[l], b_ada[l])
        mod = jnp.pad(jnp.swapaxes(mod, 0, 1), ((0, 0), (0, MOD_ROWS - 6), (0, 0)))
        s0_p = jnp.zeros((n_p, GLA_HEADS, GLA_DK, GLA_DV), F32)
        moe_p, s_p, _ = _mix(x_p, mod[:n_p], s0_p, w, emit_v=False)
        moe_s, s_s, v_s = _mix(x_s, mod[n_p:], state_gla[l], w, emit_v=True)
        x_p, x_s = _moe(moe_p, moe_s, w)
        sp_list.append(s_p)
        ss_list.append(s_s)
        vs_list.append(v_s)
    return (x_p, x_s, jnp.stack(sp_list), jnp.stack(ss_list), jnp.stack(vs_list))
```

```python
import functools

import jax
import jax.numpy as jnp
from jax import lax
from jax.experimental import pallas as pl
from jax.experimental.pallas import tpu as pltpu

D_MODEL = 1024
GMLP_WIDTH = 512
GMLP_GROUPS = 4
GMLP_HEAD = 128
GMLP_CHUNK = 128
CAUSAL_BLOCK = 64
GLA_WIDTH = 512
GLA_HEADS = 4
GLA_DV = 128
GLA_DK = 64
GLA_KEY_WIDTH = 256
GLA_GATE_RANK = 16
GLA_GATE_NORMALIZER = 16.0
N_EXPERTS = 64
TOP_K = 8
N_EXPERT_GROUPS = 8
GROUP_SIZE = N_EXPERTS // N_EXPERT_GROUPS
TOPK_GROUPS = 4
EXPERT_FF = 256
SHARED_FF = 256
ROUTED_SCALE = 2.5
EPS = 1e-6

LANES = 128
GK_PAD = LANES
IN_WIDTH_PAD = 2 * GMLP_WIDTH + 2 * GLA_KEY_WIDTH + 2 * GLA_WIDTH + GK_PAD
OFF_U = 0
OFF_VG = GMLP_WIDTH
OFF_Q = 2 * GMLP_WIDTH
OFF_K = OFF_Q + GLA_KEY_WIDTH
OFF_VL = OFF_K + GLA_KEY_WIDTH
OFF_R = OFF_VL + GLA_WIDTH
OFF_GK = OFF_R + GLA_WIDTH
MOD_ROWS = 8
VMEM_LIMIT = 56 * 1024 * 1024

F32 = jnp.float32
BF16 = jnp.bfloat16
NT_DIMS = (((1,), (1,)), ((), ()))
TN_DIMS = (((0,), (0,)), ((), ()))


def _rms(x, g):
    return x * lax.rsqrt(jnp.mean(x * x, axis=-1, keepdims=True) + EPS) * g


def _gelu(x):
    return 0.5 * x * (1.0 + jnp.tanh(0.7978845608028654 * (x + 0.044715 * (x * x * x))))


def _sigmoid(x):
    return 1.0 / (1.0 + jnp.exp(-x))


def _silu(x):
    return x * _sigmoid(x)


def _ada_kernel(c_ref, w_ref, b_ref, o_ref):
    a = _silu(c_ref[...])
    o_ref[0] = jnp.dot(a, w_ref[...], precision=lax.Precision.HIGHEST,
                       preferred_element_type=F32) + b_ref[0]


def _ada(c_all, w_ada, b_ada):
    n = c_all.shape[0]
    return pl.pallas_call(
        _ada_kernel,
        grid=(6,),
        in_specs=[pl.BlockSpec((n, D_MODEL), lambda j: (0, 0)),
                  pl.BlockSpec((D_MODEL, D_MODEL), lambda j: (0, j)),
                  pl.BlockSpec((1, 1, D_MODEL), lambda j: (j, 0, 0))],
        out_specs=pl.BlockSpec((1, n, D_MODEL), lambda j: (j, 0, 0)),
        out_shape=jax.ShapeDtypeStruct((6, n, D_MODEL), F32),
        compiler_params=pltpu.CompilerParams(vmem_limit_bytes=VMEM_LIMIT),
        name="ada",
    )(c_all, w_ada, b_ada.reshape(6, 1, D_MODEL))


def _route(logits_t, bias_t):
    t = logits_t.shape[1]
    scores = _sigmoid(logits_t)
    sel = scores + bias_t
    sub = lax.broadcasted_iota(jnp.int32, (GROUP_SIZE, t), 0)
    gscore = []
    for g in range(N_EXPERT_GROUPS):
        blk = sel[g * GROUP_SIZE:(g + 1) * GROUP_SIZE]
        m1 = jnp.max(blk, axis=0, keepdims=True)
        first = jnp.min(jnp.where(blk == m1, sub, GROUP_SIZE), axis=0, keepdims=True)
        m2 = jnp.max(jnp.where(sub == first, -jnp.inf, blk), axis=0, keepdims=True)
        gscore.append(m1 + m2)
    neg = jnp.full((GROUP_SIZE, t), -jnp.inf, F32)
    masked = []
    for g in range(N_EXPERT_GROUPS):
        rank = jnp.zeros((1, t), jnp.int32)
        for o in range(N_EXPERT_GROUPS):
            if o == g:
                continue
            ahead = (gscore[o] >= gscore[g]) if o < g else (gscore[o] > gscore[g])
            rank = rank + jnp.where(ahead, 1, 0)
        keep = jnp.broadcast_to(rank < TOPK_GROUPS, (GROUP_SIZE, t))
        masked.append(jnp.where(keep, sel[g * GROUP_SIZE:(g + 1) * GROUP_SIZE], neg))
    selm = jnp.concatenate(masked, axis=0)
    eidx = lax.broadcasted_iota(jnp.int32, (N_EXPERTS, t), 0)
    picked = jnp.zeros((N_EXPERTS, t), F32)
    for _ in range(TOP_K):
        best = jnp.max(selm, axis=0, keepdims=True)
        first = jnp.min(jnp.where(selm == best, eidx, N_EXPERTS), axis=0, keepdims=True)
        hit = eidx == first
        picked = jnp.where(hit, 1.0, picked)
        selm = jnp.where(hit, -jnp.inf, selm)
    chosen = jnp.where(picked > 0.0, scores, 0.0)
    denom = jnp.sum(chosen, axis=0, keepdims=True)
    return chosen * (ROUTED_SCALE / denom)


def _mixer_kernel(x_ref, mod_ref, s0_ref, gpre1_ref, win_ref, wgk_ref, bgk_ref, ggv_ref,
                  ws_ref, bsf_ref, ggo_ref, wout_ref, gpost1_ref, gpre2_ref, wr_ref, br_ref,
                  x1_ref, h2_ref, gates_ref, st_ref, *rest,
                  chunk, n_chunks, chunk_is_seq, emit_v):
    if emit_v:
        v_ref, *rest = rest
    proj, mix_scr = rest
    ns, ls, _ = x_ref.shape
    tm = ns * ls

    mod_row = lambda i: mod_ref[:, i:i + 1, :]

    h = _rms(x_ref[...], gpre1_ref[...]) * (1.0 + mod_row(1)) + mod_row(0)
    proj[...] = jnp.dot(h.reshape(tm, D_MODEL).astype(BF16), win_ref[...],
                        preferred_element_type=F32)

    if chunk_is_seq:
        st_ref[...] = s0_ref[...]
    else:
        @pl.when(pl.program_id(1) == 0)
        def _():
            st_ref[...] = s0_ref[...]

    row = lax.broadcasted_iota(jnp.int32, (chunk, chunk), 0)
    col = lax.broadcasted_iota(jnp.int32, (chunk, chunk), 1)
    tri = jnp.where(row >= col, 1.0, 0.0).astype(BF16)
    tri2 = jnp.concatenate([tri, tri], axis=1)
    causal = row >= col
    block_causal = (row // CAUSAL_BLOCK) >= (col // CAUSAL_BLOCK)
    wmix = [jnp.where(block_causal, ws_ref[g, 0:chunk, 0:chunk], 0.0).astype(BF16)
            for g in range(GMLP_GROUPS)]
    mid = chunk // 2

    def chunk_body(c, carry):
        rows = pl.ds(pl.multiple_of(c * chunk, chunk), chunk)
        sidx = c if chunk_is_seq else 0

        vg = _gelu(proj[rows, OFF_VG:OFF_VG + GMLP_WIDTH])
        vn = _rms(vg, ggv_ref[...])
        if emit_v:
            v_ref[rows, :] = vn
        vnb = vn.astype(BF16)
        mixed = jnp.concatenate(
            [jnp.dot(wmix[g], vnb[:, g * GMLP_HEAD:(g + 1) * GMLP_HEAD], preferred_element_type=F32)
             for g in range(GMLP_GROUPS)], axis=-1) + bsf_ref[0:chunk, :]
        u = _gelu(proj[rows, OFF_U:OFF_U + GMLP_WIDTH])
        mix_scr[rows, 0:GMLP_WIDTH] = (u * mixed).astype(BF16)

        q = proj[rows, OFF_Q:OFF_Q + GLA_KEY_WIDTH] * (GLA_DK ** -0.5)
        k = proj[rows, OFF_K:OFF_K + GLA_KEY_WIDTH]
        gk = proj[rows, OFF_GK:OFF_GK + GK_PAD]
        z = jnp.dot(gk.astype(BF16), wgk_ref[...], preferred_element_type=F32) + bgk_ref[...]
        la = (jnp.minimum(z, 0.0) - jnp.log(1.0 + jnp.exp(-jnp.abs(z)))) * (1.0 / GLA_GATE_NORMALIZER)
        la_hi = la.astype(BF16)
        la_lo = (la - la_hi.astype(F32)).astype(BF16)
        b = jnp.dot(tri2, jnp.concatenate([la_hi, la_lo], axis=0), preferred_element_type=F32)
        b_mid = b[mid - 1:mid]
        b_end = b[chunk - 1:chunk]
        qt = (q * jnp.exp(b - b_mid)).astype(BF16)
        kt = (k * jnp.exp(b_mid - b)).astype(BF16)
        qs = (q * jnp.exp(b)).astype(BF16)
        kd = (k * jnp.exp(b_end - b)).astype(BF16)
        decay = jnp.exp(b_end)
        for hd in range(GLA_HEADS):
            ksl = slice(hd * GLA_DK, (hd + 1) * GLA_DK)
            vsl = slice(OFF_VL + hd * GLA_DV, OFF_VL + (hd + 1) * GLA_DV)
            rsl = slice(OFF_R + hd * GLA_DV, OFF_R + (hd + 1) * GLA_DV)
            att = lax.dot_general(qt[:, ksl], kt[:, ksl], NT_DIMS, preferred_element_type=F32)
            att = jnp.where(causal, att, 0.0).astype(BF16)
            vh = proj[rows, vsl].astype(BF16)
            st = st_ref[sidx, hd]
            o = jnp.dot(att, vh, preferred_element_type=F32) + lax.dot_general(
                qs[:, ksl], st.astype(BF16), NT_DIMS, preferred_element_type=F32)
            on = _rms(o, ggo_ref[hd:hd + 1, :])
            mix_scr[rows, GMLP_WIDTH + hd * GLA_DV:GMLP_WIDTH + (hd + 1) * GLA_DV] = (
                on * _silu(proj[rows, rsl])).astype(BF16)
            st_ref[sidx, hd] = st * decay[:, ksl] + lax.dot_general(
                vh, kd[:, ksl], TN_DIMS, preferred_element_type=F32)
        return carry

    lax.fori_loop(0, n_chunks, chunk_body, 0, unroll=4)

    mixo = jnp.dot(mix_scr[...], wout_ref[...], preferred_element_type=F32).reshape(ns, ls, D_MODEL)
    x1 = x_ref[...] + mod_row(2) * _rms(mixo, gpost1_ref[...])
    x1_ref[...] = x1
    h2 = (_rms(x1, gpre2_ref[...]) * (1.0 + mod_row(4)) + mod_row(3)).reshape(tm, D_MODEL)
    h2_hi = h2.astype(BF16)
    h2_ref[...] = h2_hi

    h2_lo = (h2 - h2_hi.astype(F32)).astype(BF16)
    parts = jnp.dot(jnp.concatenate([h2_hi, h2_lo], axis=-1), wr_ref[...], preferred_element_type=F32)
    logits = parts[:, 0:LANES] + parts[:, LANES:2 * LANES]
    gates_ref[...] = _route(logits.T[0:N_EXPERTS], br_ref[...])


def _mixer(x, mod, s0_t, w, *, seq_tile, chunk, emit_v):
    n_seq, seq_len, _ = x.shape
    chunk_is_seq = seq_len == chunk
    if chunk_is_seq:
        ns, ls = seq_tile // chunk, chunk
        grid = (1, n_seq // ns)
        x_map = lambda b, t: (t, 0, 0)
        seq_map3 = lambda b, t: (t, 0, 0)
        seq_map4 = lambda b, t: (t, 0, 0, 0)
    else:
        ns, ls = 1, seq_tile
        grid = (n_seq, seq_len // seq_tile)
        x_map = lambda b, t: (b, t, 0)
        seq_map3 = lambda b, t: (b, 0, 0)
        seq_map4 = lambda b, t: (b, 0, 0, 0)
    tm = ns * ls
    n_tok = n_seq * seq_len
    tok_block = lambda b, t: b * grid[1] + t
    tok_map = lambda b, t: (tok_block(b, t), 0)
    full2 = lambda b, t: (0, 0)
    full3 = lambda b, t: (0, 0, 0)

    in_specs = [
        pl.BlockSpec((ns, ls, D_MODEL), x_map),
        pl.BlockSpec((ns, MOD_ROWS, D_MODEL), seq_map3),
        pl.BlockSpec((ns, GLA_HEADS, GLA_DV, GLA_DK), seq_map4),
        pl.BlockSpec((1, D_MODEL), full2),
        pl.BlockSpec((D_MODEL, IN_WIDTH_PAD), full2),
        pl.BlockSpec((GK_PAD, GLA_KEY_WIDTH), full2),
        pl.BlockSpec((1, GLA_KEY_WIDTH), full2),
        pl.BlockSpec((1, GMLP_WIDTH), full2),
        pl.BlockSpec((GMLP_GROUPS, GMLP_CHUNK, GMLP_CHUNK), full3),
        pl.BlockSpec((GMLP_CHUNK, GMLP_WIDTH), full2),
        pl.BlockSpec((GLA_HEADS, GLA_DV), full2),
        pl.BlockSpec((D_MODEL, D_MODEL), full2),
        pl.BlockSpec((1, D_MODEL), full2),
        pl.BlockSpec((1, D_MODEL), full2),
        pl.BlockSpec((2 * D_MODEL, 2 * LANES), full2),
        pl.BlockSpec((N_EXPERTS, 1), full2),
    ]
    out_specs = [
        pl.BlockSpec((ns, ls, D_MODEL), x_map),
        pl.BlockSpec((tm, D_MODEL), tok_map),
        pl.BlockSpec((N_EXPERTS, tm), lambda b, t: (0, tok_block(b, t))),
        pl.BlockSpec((ns, GLA_HEADS, GLA_DV, GLA_DK), seq_map4),
    ]
    out_shape = [
        jax.ShapeDtypeStruct((n_seq, seq_len, D_MODEL), F32),
        jax.ShapeDtypeStruct((n_tok, D_MODEL), BF16),
        jax.ShapeDtypeStruct((N_EXPERTS, n_tok), F32),
        jax.ShapeDtypeStruct((n_seq, GLA_HEADS, GLA_DV, GLA_DK), F32),
    ]
    if emit_v:
        out_specs.append(pl.BlockSpec((tm, GMLP_WIDTH), tok_map))
        out_shape.append(jax.ShapeDtypeStruct((n_tok, GMLP_WIDTH), F32))

    kern = functools.partial(_mixer_kernel, chunk=chunk, n_chunks=tm // chunk,
                             chunk_is_seq=chunk_is_seq, emit_v=emit_v)
    return pl.pallas_call(
        kern,
        grid=grid,
        in_specs=in_specs,
        out_specs=out_specs,
        out_shape=out_shape,
        scratch_shapes=[pltpu.VMEM((tm, IN_WIDTH_PAD), F32), pltpu.VMEM((tm, D_MODEL), BF16)],
        compiler_params=pltpu.CompilerParams(
            dimension_semantics=("arbitrary", "arbitrary"), vmem_limit_bytes=VMEM_LIMIT),
        name="mixer_sample" if chunk_is_seq else "mixer_prompt",
    )(x, mod, s0_t, w["g_pre1"], w["w_in"], w["w_gk_up"], w["b_gk"], w["g_gmlp_v"], w["w_s"],
      w["b_s_full"], w["g_gla_o"], w["w_out"], w["g_post1"], w["g_pre2"], w["w_router"],
      w["b_router"])


MOE_BLOCK = 256
SEG_ALIGN = 16
WIN = 64
FFN_TILE = 1024
BLOCK_ROWS = MOE_BLOCK * TOP_K + SEG_ALIGN * N_EXPERTS + WIN
MAIN_ROWS = 2688
COMBINE_SLOTS = 3
COL_CHUNK = 256
COUNT_LANES = LANES


def _count_kernel(g_ref, o_ref):
    n = g_ref.shape[1]
    selb = jnp.where(g_ref[...] > 0.0, 1.0, 0.0).astype(BF16)
    tok = lax.broadcasted_iota(jnp.int32, (n, COUNT_LANES), 0)
    lane = lax.broadcasted_iota(jnp.int32, (n, COUNT_LANES), 1)
    ind = jnp.where(tok // MOE_BLOCK == lane, 1.0, 0.0).astype(BF16)
    o_ref[0] = jnp.dot(selb, ind, preferred_element_type=F32)


def _fill_permutation(p_scr, rk_scr, g_ref, cntp_ref, loff_ref, long_ref, blk, weighted):
    tb = g_ref.shape[1]
    g = g_ref[...]
    sel = g > 0.0
    r_i = lax.broadcasted_iota(jnp.int32, (tb, tb), 0)
    c_i = lax.broadcasted_iota(jnp.int32, (tb, tb), 1)
    earlier = jnp.where(r_i < c_i, 1.0, 0.0).astype(BF16)
    rank = jnp.dot(jnp.where(sel, 1.0, 0.0).astype(BF16), earlier, preferred_element_type=F32)
    rk = jnp.where(sel, rank, -1.0)
    p_scr[...] = jnp.zeros_like(p_scr)
    jwin = lax.broadcasted_iota(jnp.int32, (WIN, tb), 0).astype(F32)

    for e in range(N_EXPERTS):
        off = pl.multiple_of(loff_ref[blk * N_EXPERTS + e], SEG_ALIGN)
        val = g[e:e + 1] if weighted else 1.0
        p_scr[pl.ds(off, WIN), :] = jnp.where(rk[e:e + 1] == jwin, val, 0.0).astype(BF16)

    @pl.when(long_ref[blk] > 0)
    def _():
        rk_scr[...] = rk

        def expert_body(e, c):
            n = cntp_ref[blk * N_EXPERTS + e]
            off = loff_ref[blk * N_EXPERTS + e]
            rke = rk_scr[pl.ds(e, 1), :]
            val = g_ref[pl.ds(e, 1), :] if weighted else 1.0

            def win_body(wi, c2):
                rows = pl.ds(pl.multiple_of(off + wi * WIN, SEG_ALIGN), WIN)
                j = jwin + (wi * WIN).astype(F32)
                new = jnp.where(rke == j, val, 0.0)
                p_scr[rows, :] = jnp.where(j < n.astype(F32), new, p_scr[rows, :].astype(F32)).astype(BF16)
                return c2

            lax.fori_loop(1, (n + WIN - 1) // WIN, win_body, 0)
            return c

        lax.fori_loop(0, N_EXPERTS, expert_body, 0)


def _start_segments(cntp_ref, loff_ref, blk, make_copy):
    for e in range(N_EXPERTS):
        make_copy(e, pl.multiple_of(loff_ref[blk * N_EXPERTS + e], SEG_ALIGN),
                  pl.multiple_of(cntp_ref[blk * N_EXPERTS + e], SEG_ALIGN)).start()


def _dispatch_kernel(cntp_ref, loff_ref, goff_ref, rows_ref, long_ref, toff_ref, tlen_ref,
                     ga_ref, gb_ref, h2a_ref, h2b_ref, xs_hbm,
                     p_scr, rk_scr, g_scr, h2_scr, xs_scr, zero_scr, sems, tail_sem,
                     *, n_blocks, n_blocks_a):
    b = pl.program_id(0)
    slot = lax.rem(b, 2)
    from_a = b < n_blocks_a
    g_scr[...] = jnp.where(from_a, ga_ref[...], gb_ref[...])
    h2_scr[...] = jnp.where(from_a, h2a_ref[...], h2b_ref[...])
    g_ref, h2_ref = g_scr, h2_scr

    def seg_copy(blk, s):
        def make(e, off, n):
            dst = pl.multiple_of(goff_ref[blk * N_EXPERTS + e], SEG_ALIGN)
            return pltpu.make_async_copy(xs_scr.at[s, pl.ds(off, n)], xs_hbm.at[pl.ds(dst, n)], sems.at[s])
        return make

    def wait_block(blk, s):
        n = pl.multiple_of(rows_ref[blk], SEG_ALIGN)
        pltpu.make_async_copy(xs_scr.at[s, pl.ds(0, n)], xs_hbm.at[pl.ds(0, n)], sems.at[s]).wait()

    def for_tails(action):
        def body(e, c):
            n = tlen_ref[e]

            @pl.when(n > 0)
            def _():
                dst = pl.multiple_of(toff_ref[e], SEG_ALIGN)
                nn = pl.multiple_of(n, SEG_ALIGN)
                action(pltpu.make_async_copy(zero_scr.at[pl.ds(0, nn)], xs_hbm.at[pl.ds(dst, nn)], tail_sem))

            return c

        lax.fori_loop(0, N_EXPERTS, body, 0)

    @pl.when(b == 0)
    def _():
        zero_scr[...] = jnp.zeros_like(zero_scr)
        for_tails(lambda cp: cp.start())

    @pl.when(b >= 2)
    def _():
        wait_block(b - 2, slot)

    _fill_permutation(p_scr, rk_scr, g_ref, cntp_ref, loff_ref, long_ref, b, weighted=False)
    def sort_rows(rows):
        for c in range(D_MODEL // COL_CHUNK):
            cols = slice(c * COL_CHUNK, (c + 1) * COL_CHUNK)
            xs = jnp.dot(p_scr[rows, :], h2_ref[:, cols], preferred_element_type=F32)
            xs_scr[slot, rows, cols] = xs.astype(BF16)

    sort_rows(slice(0, MAIN_ROWS))

    @pl.when(rows_ref[b] > MAIN_ROWS)
    def _():
        sort_rows(slice(MAIN_ROWS, BLOCK_ROWS))

    _start_segments(cntp_ref, loff_ref, b, seg_copy(b, slot))

    @pl.when(b == n_blocks - 1)
    def _():
        for_tails(lambda cp: cp.wait())
        if n_blocks >= 2:
            wait_block(b - 1, 1 - slot)
        wait_block(b, slot)


def _ffn_kernel(te_ref, nu_ref, x_ref, wg_ref, wu_ref, wd_ref, y_ref, wgu_scr, wd_scr):
    i = pl.program_id(0)

    @pl.when(i < nu_ref[0])
    def _():
        @pl.when(jnp.logical_or(i == 0, te_ref[i] != te_ref[jnp.maximum(i - 1, 0)]))
        def _():
            wgu_scr[:, 0:EXPERT_FF] = wg_ref[0].astype(BF16)
            wgu_scr[:, EXPERT_FF:2 * EXPERT_FF] = wu_ref[0].astype(BF16)
            wd_scr[...] = wd_ref[0].astype(BF16)

        gu = jnp.dot(x_ref[...], wgu_scr[...], preferred_element_type=F32)
        hid = _silu(gu[:, 0:EXPERT_FF]) * gu[:, EXPERT_FF:2 * EXPERT_FF]
        y_ref[...] = jnp.dot(hid.astype(BF16), wd_scr[...], preferred_element_type=F32).astype(BF16)


def _combine_kernel(cntp_ref, loff_ref, goff_ref, rows_ref, long_ref,
                    ga_ref, gb_ref, h2a_ref, h2b_ref, x1a_ref, x1b_ref, moda_ref, modb_ref,
                    wsgu_ref, wsd_ref, gpost2_ref, ys_hbm, outa_ref, outb_ref,
                    p_scr, rk_scr, g_scr, ys_scr, routed_scr, sems, *, n_blocks, n_blocks_a):
    b = pl.program_id(0)
    slot = lax.rem(b, COMBINE_SLOTS)
    from_a = b < n_blocks_a
    g_scr[...] = jnp.where(from_a, ga_ref[...], gb_ref[...])

    def seg_copy(blk, s):
        def make(e, off, n):
            src = pl.multiple_of(goff_ref[blk * N_EXPERTS + e], SEG_ALIGN)
            return pltpu.make_async_copy(ys_hbm.at[pl.ds(src, n)], ys_scr.at[s, pl.ds(off, n)], sems.at[s])
        return make

    @pl.when(b == 0)
    def _():
        ys_scr[...] = jnp.zeros_like(ys_scr)
        for ahead in range(min(COMBINE_SLOTS - 1, n_blocks)):
            _start_segments(cntp_ref, loff_ref, ahead, seg_copy(ahead, ahead))

    ahead = b + COMBINE_SLOTS - 1

    @pl.when(ahead < n_blocks)
    def _():
        _start_segments(cntp_ref, loff_ref, ahead, seg_copy(ahead, lax.rem(ahead, COMBINE_SLOTS)))

    _fill_permutation(p_scr, rk_scr, g_scr, cntp_ref, loff_ref, long_ref, b, weighted=True)
    n_rows = pl.multiple_of(rows_ref[b], SEG_ALIGN)
    pltpu.make_async_copy(ys_hbm.at[pl.ds(0, n_rows)], ys_scr.at[slot, pl.ds(0, n_rows)], sems.at[slot]).wait()

    def unsort_rows(rows):
        return lax.dot_general(p_scr[rows, :], ys_scr[slot, rows, :], TN_DIMS, preferred_element_type=F32)

    routed_scr[...] = unsort_rows(slice(0, MAIN_ROWS))

    @pl.when(rows_ref[b] > MAIN_ROWS)
    def _():
        routed_scr[...] += unsort_rows(slice(MAIN_ROWS, BLOCK_ROWS))

    hx = jnp.where(from_a, h2a_ref[...], h2b_ref[...])
    sgu = jnp.dot(hx, wsgu_ref[...], preferred_element_type=F32)
    shid = _silu(sgu[:, 0:SHARED_FF]) * sgu[:, SHARED_FF:2 * SHARED_FF]
    f = routed_scr[...] + jnp.dot(shid.astype(BF16), wsd_ref[...], preferred_element_type=F32)
    routed_scr[...] = _rms(f, gpost2_ref[...])

    def finish(x1_ref, mod_ref, out_ref):
        gate2 = mod_ref[:, 5:6, :]
        out_ref[...] = x1_ref[...] + gate2 * routed_scr[...].reshape(x1_ref.shape)

    @pl.when(from_a)
    def _():
        finish(x1a_ref, moda_ref, outa_ref)

    @pl.when(jnp.logical_not(from_a))
    def _():
        finish(x1b_ref, modb_ref, outb_ref)


def _segment_plan(cnt, n_tiles_max, ffn_tile):
    cntp = jnp.maximum((cnt + SEG_ALIGN - 1) // SEG_ALIGN, 1) * SEG_ALIGN
    loff = jnp.cumsum(cntp, axis=1) - cntp
    block_rows = jnp.sum(cntp, axis=1)
    has_long = jnp.max(cntp, axis=1) > WIN
    tot = jnp.sum(cntp, axis=0)
    reg = (tot + ffn_tile - 1) // ffn_tile * ffn_tile
    base = jnp.cumsum(reg) - reg
    goff = base[None, :] + jnp.cumsum(cntp, axis=0) - cntp
    tile_end = jnp.cumsum(reg // ffn_tile)
    tile = jnp.arange(n_tiles_max, dtype=jnp.int32)
    tile_expert = jnp.minimum(jnp.sum(tile_end[None, :] <= tile[:, None], axis=1), N_EXPERTS - 1)
    i32 = lambda a: a.astype(jnp.int32)
    block_tables = (i32(cntp.reshape(-1)), i32(loff.reshape(-1)), i32(goff.reshape(-1)),
                    i32(block_rows), i32(has_long))
    return block_tables, i32(base + tot), i32(reg - tot), i32(tile_expert), i32(tile_end[-1:])


def _count_blocks(gates_t, params):
    n_tok = gates_t.shape[1]
    chunk = min(n_tok, COUNT_LANES * MOE_BLOCK // 16)
    per_chunk = chunk // MOE_BLOCK
    cnt = pl.pallas_call(
        _count_kernel,
        grid=(n_tok // chunk,),
        in_specs=[pl.BlockSpec((N_EXPERTS, chunk), lambda i: (0, i))],
        out_specs=pl.BlockSpec((1, N_EXPERTS, COUNT_LANES), lambda i: (i, 0, 0)),
        out_shape=jax.ShapeDtypeStruct((n_tok // chunk, N_EXPERTS, COUNT_LANES), F32),
        compiler_params=params,
        name="moe_count",
    )(gates_t)
    return jnp.swapaxes(cnt[:, :, :per_chunk], 1, 2).reshape(n_tok // MOE_BLOCK, N_EXPERTS).astype(jnp.int32)


def _stream_specs(x1, block_of):
    n_seq, seq_len, _ = x1.shape
    if seq_len < MOE_BLOCK:
        ns, ls = MOE_BLOCK // seq_len, seq_len
        x_map = lambda b, *_: (block_of(b), 0, 0)
        mod_map = x_map
    else:
        ns, ls = 1, MOE_BLOCK
        per_seq = seq_len // MOE_BLOCK
        x_map = lambda b, *_: (block_of(b) // per_seq, block_of(b) % per_seq, 0)
        mod_map = lambda b, *_: (block_of(b) // per_seq, 0, 0)
    return pl.BlockSpec((ns, ls, D_MODEL), x_map), pl.BlockSpec((ns, MOD_ROWS, D_MODEL), mod_map)


def _moe(stream_a, stream_b, w):
    (h2_a, g_a, x1_a, mod_a), (h2_b, g_b, x1_b, mod_b) = stream_a, stream_b
    nb_a, nb_b = h2_a.shape[0] // MOE_BLOCK, h2_b.shape[0] // MOE_BLOCK
    n_blocks = nb_a + nb_b
    n_tok = n_blocks * MOE_BLOCK
    ffn_tile = FFN_TILE
    rows_max = (n_tok * TOP_K + n_blocks * N_EXPERTS * SEG_ALIGN
                + N_EXPERTS * (ffn_tile - SEG_ALIGN))
    n_tiles_max = -(-rows_max // ffn_tile)
    rows_alloc = n_tiles_max * ffn_tile
    params = pltpu.CompilerParams(dimension_semantics=("arbitrary",), vmem_limit_bytes=VMEM_LIMIT)

    cnt = jnp.concatenate([_count_blocks(g_a, params), _count_blocks(g_b, params)], axis=0)
    block_tables, tail_off, tail_len, tile_expert, n_used = _segment_plan(cnt, n_tiles_max, ffn_tile)

    in_a = lambda b: jnp.minimum(b, nb_a - 1)
    in_b = lambda b: jnp.maximum(b - nb_a, 0)
    tok_specs = [pl.BlockSpec((N_EXPERTS, MOE_BLOCK), lambda b, *_: (0, in_a(b))),
                 pl.BlockSpec((N_EXPERTS, MOE_BLOCK), lambda b, *_: (0, in_b(b))),
                 pl.BlockSpec((MOE_BLOCK, D_MODEL), lambda b, *_: (in_a(b), 0)),
                 pl.BlockSpec((MOE_BLOCK, D_MODEL), lambda b, *_: (in_b(b), 0))]
    any_spec = pl.BlockSpec(memory_space=pl.ANY)
    perm_scratch = [pltpu.VMEM((BLOCK_ROWS, MOE_BLOCK), BF16), pltpu.VMEM((N_EXPERTS, MOE_BLOCK), F32),
                    pltpu.VMEM((N_EXPERTS, MOE_BLOCK), F32)]

    x_sorted = pl.pallas_call(
        functools.partial(_dispatch_kernel, n_blocks=n_blocks, n_blocks_a=nb_a),
        grid_spec=pltpu.PrefetchScalarGridSpec(
            num_scalar_prefetch=7, grid=(n_blocks,),
            in_specs=tok_specs, out_specs=any_spec,
            scratch_shapes=perm_scratch + [
                pltpu.VMEM((MOE_BLOCK, D_MODEL), BF16),
                pltpu.VMEM((2, BLOCK_ROWS, D_MODEL), BF16),
                pltpu.VMEM((ffn_tile, D_MODEL), BF16),
                pltpu.SemaphoreType.DMA((2,)), pltpu.SemaphoreType.DMA(())]),
        out_shape=jax.ShapeDtypeStruct((rows_alloc, D_MODEL), BF16),
        compiler_params=params,
        name="moe_dispatch",
    )(*block_tables, tail_off, tail_len, g_a, g_b, h2_a, h2_b)

    last_used = lambda i, nu: jnp.maximum(jnp.minimum(i, nu[0] - 1), 0)
    row_map = lambda i, te, nu: (last_used(i, nu), 0)
    w_map = lambda i, te, nu: (te[last_used(i, nu)], 0, 0)
    y_sorted = pl.pallas_call(
        _ffn_kernel,
        grid_spec=pltpu.PrefetchScalarGridSpec(
            num_scalar_prefetch=2, grid=(n_tiles_max,),
            in_specs=[pl.BlockSpec((ffn_tile, D_MODEL), row_map),
                      pl.BlockSpec((1, D_MODEL, EXPERT_FF), w_map),
                      pl.BlockSpec((1, D_MODEL, EXPERT_FF), w_map),
                      pl.BlockSpec((1, EXPERT_FF, D_MODEL), w_map)],
            out_specs=pl.BlockSpec((ffn_tile, D_MODEL), row_map),
            scratch_shapes=[pltpu.VMEM((D_MODEL, 2 * EXPERT_FF), BF16),
                            pltpu.VMEM((EXPERT_FF, D_MODEL), BF16)]),
        out_shape=jax.ShapeDtypeStruct((rows_alloc, D_MODEL), BF16),
        compiler_params=params,
        name="moe_ffn",
    )(tile_expert, n_used, x_sorted, w["w_e_gate"], w["w_e_up"], w["w_e_down"])

    full2 = lambda b, *_: (0, 0)
    x_spec_a, mod_spec_a = _stream_specs(x1_a, in_a)
    x_spec_b, mod_spec_b = _stream_specs(x1_b, in_b)
    return pl.pallas_call(
        functools.partial(_combine_kernel, n_blocks=n_blocks, n_blocks_a=nb_a),
        grid_spec=pltpu.PrefetchScalarGridSpec(
            num_scalar_prefetch=5, grid=(n_blocks,),
            in_specs=tok_specs + [
                x_spec_a, x_spec_b, mod_spec_a, mod_spec_b,
                pl.BlockSpec((D_MODEL, 2 * SHARED_FF), full2),
                pl.BlockSpec((SHARED_FF, D_MODEL), full2),
                pl.BlockSpec((1, D_MODEL), full2),
                any_spec],
            out_specs=[x_spec_a, x_spec_b],
            scratch_shapes=perm_scratch + [
                pltpu.VMEM((COMBINE_SLOTS, BLOCK_ROWS, D_MODEL), BF16),
                pltpu.VMEM((MOE_BLOCK, D_MODEL), F32),
                pltpu.SemaphoreType.DMA((COMBINE_SLOTS,))]),
        out_shape=[jax.ShapeDtypeStruct(x1_a.shape, F32), jax.ShapeDtypeStruct(x1_b.shape, F32)],
        compiler_params=params,
        name="moe_combine",
    )(*block_tables, g_a, g_b, h2_a, h2_b, x1_a, x1_b, mod_a, mod_b,
      w["w_sh_gu"], w["w_sh_down"], w["g_post2"], y_sorted)


def _prep_weights(g_pre1, g_post1, w_in, w_gk_up, b_gk, g_gmlp_v, w_s, b_s, g_gla_o, w_out, g_pre2,
                  g_post2, w_router, b_router, w_e_gate, w_e_up, w_e_down, w_sh_gate, w_sh_up,
                  w_sh_down):
    row = lambda v: v.reshape(1, -1)
    wr = jnp.pad(w_router, ((0, 0), (0, LANES - N_EXPERTS)))
    wr_top = lax.bitcast_convert_type(
        lax.bitcast_convert_type(wr, jnp.uint32) & jnp.uint32(0xFFFF0000), F32)
    wr_hi = wr_top.astype(BF16)
    wr_lo = (wr - wr_top).astype(BF16)
    wr_split = jnp.concatenate([jnp.concatenate([wr_hi, wr_lo], axis=1),
                                jnp.concatenate([wr_hi, jnp.zeros_like(wr_lo)], axis=1)], axis=0)
    return {
        "g_pre1": row(g_pre1), "g_post1": row(g_post1), "g_pre2": row(g_pre2), "g_post2": row(g_post2),
        "w_in": jnp.pad(w_in, ((0, 0), (0, GK_PAD - GLA_GATE_RANK))).astype(BF16),
        "w_gk_up": jnp.pad(w_gk_up, ((0, GK_PAD - GLA_GATE_RANK), (0, 0))).astype(BF16),
        "b_gk": row(b_gk),
        "g_gmlp_v": row(g_gmlp_v),
        "w_s": w_s,
        "b_s_full": jnp.repeat(b_s.T, GMLP_HEAD, axis=1),
        "g_gla_o": g_gla_o,
        "w_out": w_out.astype(BF16),
        "w_router": wr_split,
        "b_router": b_router.reshape(N_EXPERTS, 1),
        "w_e_gate": w_e_gate, "w_e_up": w_e_up, "w_e_down": w_e_down,
        "w_sh_gu": jnp.concatenate([w_sh_gate, w_sh_up], axis=-1).astype(BF16),
        "w_sh_down": w_sh_down.astype(BF16),
    }


PROMPT_TILE = 1024
SAMPLE_TILE = 256


def _mix(x, mod, s0, w, *, emit_v):
    n_seq, seq_len, _ = x.shape
    s0_t = jnp.swapaxes(s0, -1, -2)
    if seq_len >= GMLP_CHUNK:
        outs = _mixer(x, mod, s0_t, w, seq_tile=PROMPT_TILE, chunk=GMLP_CHUNK, emit_v=emit_v)
    else:
        outs = _mixer(x, mod, s0_t, w, seq_tile=SAMPLE_TILE, chunk=seq_len, emit_v=emit_v)
    x1, h2, gates_t, st = outs[:4]
    v = outs[4].reshape(n_seq, seq_len, GMLP_GROUPS, GMLP_HEAD) if emit_v else None
    return (h2, gates_t, x1, mod), jnp.swapaxes(st, -1, -2), v


def kernel(x_prompt, x_sample, state_gla, c_prompt, c_sample, w_ada, b_ada, g_pre1, g_post1, w_in, w_gk_up, b_gk, g_gmlp_v, w_s, b_s, g_gla_o, w_out, g_pre2, g_post2, w_router, b_router, w_e_gate, w_e_up, w_e_down, w_sh_gate, w_sh_up, w_sh_down):
    depth = w_ada.shape[0]
    n_p, n_s = x_prompt.shape[0], x_sample.shape[0]
    c_all = jnp.concatenate([c_prompt, c_sample], axis=0)
    x_p, x_s = x_prompt, x_sample
    sp_list, ss_list, vs_list = [], [], []
    for l in range(depth):
        w = _prep_weights(g_pre1[l], g_post1[l], w_in[l], w_gk_up[l], b_gk[l], g_gmlp_v[l], w_s[l],
                          b_s[l], g_gla_o[l], w_out[l], g_pre2[l], g_post2[l], w_router[l],
                          b_router[l], w_e_gate[l], w_e_up[l], w_e_down[l], w_sh_gate[l], w_sh_up[l],
                          w_sh_down[l])
        mod = _ada(c_all, w_ada[l], b_ada[l])
        mod = jnp.pad(jnp.swapaxes(mod, 0, 1), ((0, 0), (0, MOD_ROWS - 6), (0, 0)))
        s0_p = jnp.zeros((n_p, GLA_HEADS, GLA_DK, GLA_DV), F32)
        moe_p, s_p, _ = _mix(x_p, mod[:n_p], s0_p, w, emit_v=False)
        moe_s, s_s, v_s = _mix(x_s, mod[n_p:], state_gla[l], w, emit_v=True)
        x_p, x_s = _moe(moe_p, moe_s, w)
        sp_list.append(s_p)
        ss_list.append(s_s)
        vs_list.append(v_s)
    return (x_p, x_s, jnp.stack(sp_list), jnp.stack(ss_list), jnp.stack(vs_list))
```

```python
import functools

import jax
import jax.numpy as jnp
from jax import lax
from jax.experimental import pallas as pl
from jax.experimental.pallas import tpu as pltpu

D_MODEL = 1024
GMLP_WIDTH = 512
GMLP_GROUPS = 4
GMLP_HEAD = 128
GMLP_CHUNK = 128
CAUSAL_BLOCK = 64
GLA_WIDTH = 512
GLA_HEADS = 4
GLA_DV = 128
GLA_DK = 64
GLA_KEY_WIDTH = 256
GLA_GATE_RANK = 16
GLA_GATE_NORMALIZER = 16.0
N_EXPERTS = 64
TOP_K = 8
N_EXPERT_GROUPS = 8
GROUP_SIZE = N_EXPERTS // N_EXPERT_GROUPS
TOPK_GROUPS = 4
EXPERT_FF = 256
SHARED_FF = 256
ROUTED_SCALE = 2.5
EPS = 1e-6

LANES = 128
GK_PAD = LANES
IN_WIDTH_PAD = 2 * GMLP_WIDTH + 2 * GLA_KEY_WIDTH + 2 * GLA_WIDTH + GK_PAD
OFF_U = 0
OFF_VG = GMLP_WIDTH
OFF_Q = 2 * GMLP_WIDTH
OFF_K = OFF_Q + GLA_KEY_WIDTH
OFF_VL = OFF_K + GLA_KEY_WIDTH
OFF_R = OFF_VL + GLA_WIDTH
OFF_GK = OFF_R + GLA_WIDTH
MOD_ROWS = 8
VMEM_LIMIT = 56 * 1024 * 1024

F32 = jnp.float32
BF16 = jnp.bfloat16
NT_DIMS = (((1,), (1,)), ((), ()))
TN_DIMS = (((0,), (0,)), ((), ()))


def _rms(x, g):
    return x * lax.rsqrt(jnp.mean(x * x, axis=-1, keepdims=True) + EPS) * g


def _gelu(x):
    return 0.5 * x * (1.0 + jnp.tanh(0.7978845608028654 * (x + 0.044715 * (x * x * x))))


def _sigmoid(x):
    return 1.0 / (1.0 + jnp.exp(-x))


def _silu(x):
    return x * _sigmoid(x)


def _ada_kernel(c_ref, w_ref, b_ref, o_ref):
    a = _silu(c_ref[...])
    o_ref[0] = jnp.dot(a, w_ref[...], precision=lax.Precision.HIGHEST,
                       preferred_element_type=F32) + b_ref[0]


def _ada(c_all, w_ada, b_ada):
    n = c_all.shape[0]
    return pl.pallas_call(
        _ada_kernel,
        grid=(6,),
        in_specs=[pl.BlockSpec((n, D_MODEL), lambda j: (0, 0)),
                  pl.BlockSpec((D_MODEL, D_MODEL), lambda j: (0, j)),
                  pl.BlockSpec((1, 1, D_MODEL), lambda j: (j, 0, 0))],
        out_specs=pl.BlockSpec((1, n, D_MODEL), lambda j: (j, 0, 0)),
        out_shape=jax.ShapeDtypeStruct((6, n, D_MODEL), F32),
        compiler_params=pltpu.CompilerParams(vmem_limit_bytes=VMEM_LIMIT),
        name="ada",
    )(c_all, w_ada, b_ada.reshape(6, 1, D_MODEL))


def _route(logits_t, bias_t):
    t = logits_t.shape[1]
    scores = _sigmoid(logits_t)
    sel = scores + bias_t
    sub = lax.broadcasted_iota(jnp.int32, (GROUP_SIZE, t), 0)
    gscore = []
    for g in range(N_EXPERT_GROUPS):
        blk = sel[g * GROUP_SIZE:(g + 1) * GROUP_SIZE]
        m1 = jnp.max(blk, axis=0, keepdims=True)
        first = jnp.min(jnp.where(blk == m1, sub, GROUP_SIZE), axis=0, keepdims=True)
        m2 = jnp.max(jnp.where(sub == first, -jnp.inf, blk), axis=0, keepdims=True)
        gscore.append(m1 + m2)
    neg = jnp.full((GROUP_SIZE, t), -jnp.inf, F32)
    masked = []
    for g in range(N_EXPERT_GROUPS):
        rank = jnp.zeros((1, t), jnp.int32)
        for o in range(N_EXPERT_GROUPS):
            if o == g:
                continue
            ahead = (gscore[o] >= gscore[g]) if o < g else (gscore[o] > gscore[g])
            rank = rank + jnp.where(ahead, 1, 0)
        keep = jnp.broadcast_to(rank < TOPK_GROUPS, (GROUP_SIZE, t))
        masked.append(jnp.where(keep, sel[g * GROUP_SIZE:(g + 1) * GROUP_SIZE], neg))
    selm = jnp.concatenate(masked, axis=0)
    eidx = lax.broadcasted_iota(jnp.int32, (N_EXPERTS, t), 0)
    picked = jnp.zeros((N_EXPERTS, t), F32)
    for _ in range(TOP_K):
        best = jnp.max(selm, axis=0, keepdims=True)
        first = jnp.min(jnp.where(selm == best, eidx, N_EXPERTS), axis=0, keepdims=True)
        hit = eidx == first
        picked = jnp.where(hit, 1.0, picked)
        selm = jnp.where(hit, -jnp.inf, selm)
    chosen = jnp.where(picked > 0.0, scores, 0.0)
    denom = jnp.sum(chosen, axis=0, keepdims=True)
    return chosen * (ROUTED_SCALE / denom)


def _mixer_kernel(x_ref, mod_ref, s0_ref, gpre1_ref, win_ref, wgk_ref, bgk_ref, ggv_ref,
                  ws_ref, bsf_ref, ggo_ref, wout_ref, gpost1_ref, gpre2_ref, wr_ref, br_ref,
                  x1_ref, h2_ref, gates_ref, st_ref, *rest,
                  chunk, n_chunks, chunk_is_seq, emit_v):
    if emit_v:
        v_ref, *rest = rest
    proj, mix_scr = rest
    ns, ls, _ = x_ref.shape
    tm = ns * ls

    mod_row = lambda i: mod_ref[:, i:i + 1, :]

    h = _rms(x_ref[...], gpre1_ref[...]) * (1.0 + mod_row(1)) + mod_row(0)
    proj[...] = jnp.dot(h.reshape(tm, D_MODEL).astype(BF16), win_ref[...],
                        preferred_element_type=F32)

    if chunk_is_seq:
        st_ref[...] = s0_ref[...]
    else:
        @pl.when(pl.program_id(1) == 0)
        def _():
            st_ref[...] = s0_ref[...]

    row = lax.broadcasted_iota(jnp.int32, (chunk, chunk), 0)
    col = lax.broadcasted_iota(jnp.int32, (chunk, chunk), 1)
    tri = jnp.where(row >= col, 1.0, 0.0).astype(BF16)
    tri2 = jnp.concatenate([tri, tri], axis=1)
    causal = row >= col
    block_causal = (row // CAUSAL_BLOCK) >= (col // CAUSAL_BLOCK)
    wmix = [jnp.where(block_causal, ws_ref[g, 0:chunk, 0:chunk], 0.0).astype(BF16)
            for g in range(GMLP_GROUPS)]
    mid = chunk // 2

    def chunk_body(c, carry):
        rows = pl.ds(pl.multiple_of(c * chunk, chunk), chunk)
        sidx = c if chunk_is_seq else 0

        vg = _gelu(proj[rows, OFF_VG:OFF_VG + GMLP_WIDTH])
        vn = _rms(vg, ggv_ref[...])
        if emit_v:
            v_ref[rows, :] = vn
        vnb = vn.astype(BF16)
        mixed = jnp.concatenate(
            [jnp.dot(wmix[g], vnb[:, g * GMLP_HEAD:(g + 1) * GMLP_HEAD], preferred_element_type=F32)
             for g in range(GMLP_GROUPS)], axis=-1) + bsf_ref[0:chunk, :]
        u = _gelu(proj[rows, OFF_U:OFF_U + GMLP_WIDTH])
        mix_scr[rows, 0:GMLP_WIDTH] = (u * mixed).astype(BF16)

        q = proj[rows, OFF_Q:OFF_Q + GLA_KEY_WIDTH] * (GLA_DK ** -0.5)
        k = proj[rows, OFF_K:OFF_K + GLA_KEY_WIDTH]
        gk = proj[rows, OFF_GK:OFF_GK + GK_PAD]
        z = jnp.dot(gk.astype(BF16), wgk_ref[...], preferred_element_type=F32) + bgk_ref[...]
        la = (jnp.minimum(z, 0.0) - jnp.log(1.0 + jnp.exp(-jnp.abs(z)))) * (1.0 / GLA_GATE_NORMALIZER)
        la_hi = la.astype(BF16)
        la_lo = (la - la_hi.astype(F32)).astype(BF16)
        b = jnp.dot(tri2, jnp.concatenate([la_hi, la_lo], axis=0), preferred_element_type=F32)
        b_mid = b[mid - 1:mid]
        b_end = b[chunk - 1:chunk]
        qt = (q * jnp.exp(b - b_mid)).astype(BF16)
        kt = (k * jnp.exp(b_mid - b)).astype(BF16)
        qs = (q * jnp.exp(b)).astype(BF16)
        kd = (k * jnp.exp(b_end - b)).astype(BF16)
        decay = jnp.exp(b_end)
        for hd in range(GLA_HEADS):
            ksl = slice(hd * GLA_DK, (hd + 1) * GLA_DK)
            vsl = slice(OFF_VL + hd * GLA_DV, OFF_VL + (hd + 1) * GLA_DV)
            rsl = slice(OFF_R + hd * GLA_DV, OFF_R + (hd + 1) * GLA_DV)
            att = lax.dot_general(qt[:, ksl], kt[:, ksl], NT_DIMS, preferred_element_type=F32)
            att = jnp.where(causal, att, 0.0).astype(BF16)
            vh = proj[rows, vsl].astype(BF16)
            st = st_ref[sidx, hd]
            o = jnp.dot(att, vh, preferred_element_type=F32) + lax.dot_general(
                qs[:, ksl], st.astype(BF16), NT_DIMS, preferred_element_type=F32)
            on = _rms(o, ggo_ref[hd:hd + 1, :])
            mix_scr[rows, GMLP_WIDTH + hd * GLA_DV:GMLP_WIDTH + (hd + 1) * GLA_DV] = (
                on * _silu(proj[rows, rsl])).astype(BF16)
            st_ref[sidx, hd] = st * decay[:, ksl] + lax.dot_general(
                vh, kd[:, ksl], TN_DIMS, preferred_element_type=F32)
        return carry

    lax.fori_loop(0, n_chunks, chunk_body, 0, unroll=4)

    mixo = jnp.dot(mix_scr[...], wout_ref[...], preferred_element_type=F32).reshape(ns, ls, D_MODEL)
    x1 = x_ref[...] + mod_row(2) * _rms(mixo, gpost1_ref[...])
    x1_ref[...] = x1
    h2 = (_rms(x1, gpre2_ref[...]) * (1.0 + mod_row(4)) + mod_row(3)).reshape(tm, D_MODEL)
    h2_hi = h2.astype(BF16)
    h2_ref[...] = h2_hi

    h2_lo = (h2 - h2_hi.astype(F32)).astype(BF16)
    parts = jnp.dot(jnp.concatenate([h2_hi, h2_lo], axis=-1), wr_ref[...], preferred_element_type=F32)
    logits = parts[:, 0:LANES] + parts[:, LANES:2 * LANES]
    gates_ref[...] = _route(logits.T[0:N_EXPERTS], br_ref[...])


def _mixer(x, mod, s0_t, w, *, seq_tile, chunk, emit_v):
    n_seq, seq_len, _ = x.shape
    chunk_is_seq = seq_len == chunk
    if chunk_is_seq:
        ns, ls = seq_tile // chunk, chunk
        grid = (1, n_seq // ns)
        x_map = lambda b, t: (t, 0, 0)
        seq_map3 = lambda b, t: (t, 0, 0)
        seq_map4 = lambda b, t: (t, 0, 0, 0)
    else:
        ns, ls = 1, seq_tile
        grid = (n_seq, seq_len // seq_tile)
        x_map = lambda b, t: (b, t, 0)
        seq_map3 = lambda b, t: (b, 0, 0)
        seq_map4 = lambda b, t: (b, 0, 0, 0)
    tm = ns * ls
    n_tok = n_seq * seq_len
    tok_block = lambda b, t: b * grid[1] + t
    tok_map = lambda b, t: (tok_block(b, t), 0)
    full2 = lambda b, t: (0, 0)
    full3 = lambda b, t: (0, 0, 0)

    in_specs = [
        pl.BlockSpec((ns, ls, D_MODEL), x_map),
        pl.BlockSpec((ns, MOD_ROWS, D_MODEL), seq_map3),
        pl.BlockSpec((ns, GLA_HEADS, GLA_DV, GLA_DK), seq_map4),
        pl.BlockSpec((1, D_MODEL), full2),
        pl.BlockSpec((D_MODEL, IN_WIDTH_PAD), full2),
        pl.BlockSpec((GK_PAD, GLA_KEY_WIDTH), full2),
        pl.BlockSpec((1, GLA_KEY_WIDTH), full2),
        pl.BlockSpec((1, GMLP_WIDTH), full2),
        pl.BlockSpec((GMLP_GROUPS, GMLP_CHUNK, GMLP_CHUNK), full3),
        pl.BlockSpec((GMLP_CHUNK, GMLP_WIDTH), full2),
        pl.BlockSpec((GLA_HEADS, GLA_DV), full2),
        pl.BlockSpec((D_MODEL, D_MODEL), full2),
        pl.BlockSpec((1, D_MODEL), full2),
        pl.BlockSpec((1, D_MODEL), full2),
        pl.BlockSpec((2 * D_MODEL, 2 * LANES), full2),
        pl.BlockSpec((N_EXPERTS, 1), full2),
    ]
    out_specs = [
        pl.BlockSpec((ns, ls, D_MODEL), x_map),
        pl.BlockSpec((tm, D_MODEL), tok_map),
        pl.BlockSpec((N_EXPERTS, tm), lambda b, t: (0, tok_block(b, t))),
        pl.BlockSpec((ns, GLA_HEADS, GLA_DV, GLA_DK), seq_map4),
    ]
    out_shape = [
        jax.ShapeDtypeStruct((n_seq, seq_len, D_MODEL), F32),
        jax.ShapeDtypeStruct((n_tok, D_MODEL), BF16),
        jax.ShapeDtypeStruct((N_EXPERTS, n_tok), F32),
        jax.ShapeDtypeStruct((n_seq, GLA_HEADS, GLA_DV, GLA_DK), F32),
    ]
    if emit_v:
        out_specs.append(pl.BlockSpec((tm, GMLP_WIDTH), tok_map))
        out_shape.append(jax.ShapeDtypeStruct((n_tok, GMLP_WIDTH), F32))

    kern = functools.partial(_mixer_kernel, chunk=chunk, n_chunks=tm // chunk,
                             chunk_is_seq=chunk_is_seq, emit_v=emit_v)
    return pl.pallas_call(
        kern,
        grid=grid,
        in_specs=in_specs,
        out_specs=out_specs,
        out_shape=out_shape,
        scratch_shapes=[pltpu.VMEM((tm, IN_WIDTH_PAD), F32), pltpu.VMEM((tm, D_MODEL), BF16)],
        compiler_params=pltpu.CompilerParams(
            dimension_semantics=("arbitrary", "arbitrary"), vmem_limit_bytes=VMEM_LIMIT),
        name="mixer_sample" if chunk_is_seq else "mixer_prompt",
    )(x, mod, s0_t, w["g_pre1"], w["w_in"], w["w_gk_up"], w["b_gk"], w["g_gmlp_v"], w["w_s"],
      w["b_s_full"], w["g_gla_o"], w["w_out"], w["g_post1"], w["g_pre2"], w["w_router"],
      w["b_router"])


MOE_BLOCK = 256
SEG_ALIGN = 16
WIN = 64
FFN_TILE = 1024
BLOCK_ROWS = MOE_BLOCK * TOP_K + SEG_ALIGN * N_EXPERTS + WIN
MAIN_ROWS = 2688
COMBINE_SLOTS = 3
COL_CHUNK = 256
COUNT_LANES = LANES


def _count_kernel(g_ref, o_ref):
    n = g_ref.shape[1]
    selb = jnp.where(g_ref[...] > 0.0, 1.0, 0.0).astype(BF16)
    tok = lax.broadcasted_iota(jnp.int32, (n, COUNT_LANES), 0)
    lane = lax.broadcasted_iota(jnp.int32, (n, COUNT_LANES), 1)
    ind = jnp.where(tok // MOE_BLOCK == lane, 1.0, 0.0).astype(BF16)
    o_ref[0] = jnp.dot(selb, ind, preferred_element_type=F32)


def _fill_permutation(p_scr, rk_scr, g_ref, cntp_ref, loff_ref, long_ref, blk, weighted):
    tb = g_ref.shape[1]
    g = g_ref[...]
    sel = g > 0.0
    r_i = lax.broadcasted_iota(jnp.int32, (tb, tb), 0)
    c_i = lax.broadcasted_iota(jnp.int32, (tb, tb), 1)
    earlier = jnp.where(r_i < c_i, 1.0, 0.0).astype(BF16)
    rank = jnp.dot(jnp.where(sel, 1.0, 0.0).astype(BF16), earlier, preferred_element_type=F32)
    rk = jnp.where(sel, rank, -1.0)
    p_scr[...] = jnp.zeros_like(p_scr)
    jwin = lax.broadcasted_iota(jnp.int32, (WIN, tb), 0).astype(F32)

    for e in range(N_EXPERTS):
        off = pl.multiple_of(loff_ref[blk * N_EXPERTS + e], SEG_ALIGN)
        val = g[e:e + 1] if weighted else 1.0
        p_scr[pl.ds(off, WIN), :] = jnp.where(rk[e:e + 1] == jwin, val, 0.0).astype(BF16)

    @pl.when(long_ref[blk] > 0)
    def _():
        rk_scr[...] = rk

        def expert_body(e, c):
            n = cntp_ref[blk * N_EXPERTS + e]
            off = loff_ref[blk * N_EXPERTS + e]
            rke = rk_scr[pl.ds(e, 1), :]
            val = g_ref[pl.ds(e, 1), :] if weighted else 1.0

            def win_body(wi, c2):
                rows = pl.ds(pl.multiple_of(off + wi * WIN, SEG_ALIGN), WIN)
                j = jwin + (wi * WIN).astype(F32)
                new = jnp.where(rke == j, val, 0.0)
                p_scr[rows, :] = jnp.where(j < n.astype(F32), new, p_scr[rows, :].astype(F32)).astype(BF16)
                return c2

            lax.fori_loop(1, (n + WIN - 1) // WIN, win_body, 0)
            return c

        lax.fori_loop(0, N_EXPERTS, expert_body, 0)


def _start_segments(cntp_ref, loff_ref, blk, make_copy):
    for e in range(N_EXPERTS):
        make_copy(e, pl.multiple_of(loff_ref[blk * N_EXPERTS + e], SEG_ALIGN),
                  pl.multiple_of(cntp_ref[blk * N_EXPERTS + e], SEG_ALIGN)).start()


def _dispatch_kernel(cntp_ref, loff_ref, goff_ref, rows_ref, long_ref, toff_ref, tlen_ref,
                     ga_ref, gb_ref, h2a_ref, h2b_ref, xs_hbm,
                     p_scr, rk_scr, g_scr, h2_scr, xs_scr, zero_scr, sems, tail_sem,
                     *, n_blocks, n_blocks_a):
    b = pl.program_id(0)
    slot = lax.rem(b, 2)
    from_a = b < n_blocks_a
    g_scr[...] = jnp.where(from_a, ga_ref[...], gb_ref[...])
    h2_scr[...] = jnp.where(from_a, h2a_ref[...], h2b_ref[...])
    g_ref, h2_ref = g_scr, h2_scr

    def seg_copy(blk, s):
        def make(e, off, n):
            dst = pl.multiple_of(goff_ref[blk * N_EXPERTS + e], SEG_ALIGN)
            return pltpu.make_async_copy(xs_scr.at[s, pl.ds(off, n)], xs_hbm.at[pl.ds(dst, n)], sems.at[s])
        return make

    def wait_block(blk, s):
        n = pl.multiple_of(rows_ref[blk], SEG_ALIGN)
        pltpu.make_async_copy(xs_scr.at[s, pl.ds(0, n)], xs_hbm.at[pl.ds(0, n)], sems.at[s]).wait()

    def for_tails(action):
        def body(e, c):
            n = tlen_ref[e]

            @pl.when(n > 0)
            def _():
                dst = pl.multiple_of(toff_ref[e], SEG_ALIGN)
                nn = pl.multiple_of(n, SEG_ALIGN)
                action(pltpu.make_async_copy(zero_scr.at[pl.ds(0, nn)], xs_hbm.at[pl.ds(dst, nn)], tail_sem))

            return c

        lax.fori_loop(0, N_EXPERTS, body, 0)

    @pl.when(b == 0)
    def _():
        zero_scr[...] = jnp.zeros_like(zero_scr)
        for_tails(lambda cp: cp.start())

    @pl.when(b >= 2)
    def _():
        wait_block(b - 2, slot)

    _fill_permutation(p_scr, rk_scr, g_ref, cntp_ref, loff_ref, long_ref, b, weighted=False)
    def sort_rows(rows):
        for c in range(D_MODEL // COL_CHUNK):
            cols = slice(c * COL_CHUNK, (c + 1) * COL_CHUNK)
            xs = jnp.dot(p_scr[rows, :], h2_ref[:, cols], preferred_element_type=F32)
            xs_scr[slot, rows, cols] = xs.astype(BF16)

    sort_rows(slice(0, MAIN_ROWS))

    @pl.when(rows_ref[b] > MAIN_ROWS)
    def _():
        sort_rows(slice(MAIN_ROWS, BLOCK_ROWS))

    _start_segments(cntp_ref, loff_ref, b, seg_copy(b, slot))

    @pl.when(b == n_blocks - 1)
    def _():
        for_tails(lambda cp: cp.wait())
        if n_blocks >= 2:
            wait_block(b - 1, 1 - slot)
        wait_block(b, slot)


def _ffn_kernel(te_ref, tr_ref, nu_ref, x_ref, wg_ref, wu_ref, wd_ref, y_ref, wgu_scr, wd_scr):
    i = pl.program_id(0)
    half = x_ref.shape[0] // 2

    def ffn(rows):
        gu = jnp.dot(x_ref[rows, :], wgu_scr[...], preferred_element_type=F32)
        hid = _silu(gu[:, 0:EXPERT_FF]) * gu[:, EXPERT_FF:2 * EXPERT_FF]
        y_ref[rows, :] = jnp.dot(hid.astype(BF16), wd_scr[...], preferred_element_type=F32).astype(BF16)

    @pl.when(i < nu_ref[0])
    def _():
        @pl.when(jnp.logical_or(i == 0, te_ref[i] != te_ref[jnp.maximum(i - 1, 0)]))
        def _():
            wgu_scr[:, 0:EXPERT_FF] = wg_ref[0].astype(BF16)
            wgu_scr[:, EXPERT_FF:2 * EXPERT_FF] = wu_ref[0].astype(BF16)
            wd_scr[...] = wd_ref[0].astype(BF16)

        @pl.when(tr_ref[i] > half)
        def _():
            ffn(slice(0, 2 * half))

        @pl.when(tr_ref[i] <= half)
        def _():
            ffn(slice(0, half))
            y_ref[half:, :] = jnp.zeros((half, y_ref.shape[1]), y_ref.dtype)


def _combine_kernel(cntp_ref, loff_ref, goff_ref, rows_ref, long_ref,
                    ga_ref, gb_ref, h2a_ref, h2b_ref, x1a_ref, x1b_ref, moda_ref, modb_ref,
                    wsgu_ref, wsd_ref, gpost2_ref, ys_hbm, outa_ref, outb_ref,
                    p_scr, rk_scr, g_scr, ys_scr, routed_scr, sems, *, n_blocks, n_blocks_a):
    b = pl.program_id(0)
    slot = lax.rem(b, COMBINE_SLOTS)
    from_a = b < n_blocks_a
    g_scr[...] = jnp.where(from_a, ga_ref[...], gb_ref[...])

    def seg_copy(blk, s):
        def make(e, off, n):
            src = pl.multiple_of(goff_ref[blk * N_EXPERTS + e], SEG_ALIGN)
            return pltpu.make_async_copy(ys_hbm.at[pl.ds(src, n)], ys_scr.at[s, pl.ds(off, n)], sems.at[s])
        return make

    @pl.when(b == 0)
    def _():
        ys_scr[...] = jnp.zeros_like(ys_scr)
        for ahead in range(min(COMBINE_SLOTS - 1, n_blocks)):
            _start_segments(cntp_ref, loff_ref, ahead, seg_copy(ahead, ahead))

    ahead = b + COMBINE_SLOTS - 1

    @pl.when(ahead < n_blocks)
    def _():
        _start_segments(cntp_ref, loff_ref, ahead, seg_copy(ahead, lax.rem(ahead, COMBINE_SLOTS)))

    _fill_permutation(p_scr, rk_scr, g_scr, cntp_ref, loff_ref, long_ref, b, weighted=True)
    n_rows = pl.multiple_of(rows_ref[b], SEG_ALIGN)
    pltpu.make_async_copy(ys_hbm.at[pl.ds(0, n_rows)], ys_scr.at[slot, pl.ds(0, n_rows)], sems.at[slot]).wait()

    def unsort_rows(rows):
        return lax.dot_general(p_scr[rows, :], ys_scr[slot, rows, :], TN_DIMS, preferred_element_type=F32)

    routed_scr[...] = unsort_rows(slice(0, MAIN_ROWS))

    @pl.when(rows_ref[b] > MAIN_ROWS)
    def _():
        routed_scr[...] += unsort_rows(slice(MAIN_ROWS, BLOCK_ROWS))

    hx = jnp.where(from_a, h2a_ref[...], h2b_ref[...])
    sgu = jnp.dot(hx, wsgu_ref[...], preferred_element_type=F32)
    shid = _silu(sgu[:, 0:SHARED_FF]) * sgu[:, SHARED_FF:2 * SHARED_FF]
    f = routed_scr[...] + jnp.dot(shid.astype(BF16), wsd_ref[...], preferred_element_type=F32)
    routed_scr[...] = _rms(f, gpost2_ref[...])

    def finish(x1_ref, mod_ref, out_ref):
        gate2 = mod_ref[:, 5:6, :]
        out_ref[...] = x1_ref[...] + gate2 * routed_scr[...].reshape(x1_ref.shape)

    @pl.when(from_a)
    def _():
        finish(x1a_ref, moda_ref, outa_ref)

    @pl.when(jnp.logical_not(from_a))
    def _():
        finish(x1b_ref, modb_ref, outb_ref)


def _segment_plan(cnt, n_tiles_max, ffn_tile):
    cntp = jnp.maximum((cnt + SEG_ALIGN - 1) // SEG_ALIGN, 1) * SEG_ALIGN
    loff = jnp.cumsum(cntp, axis=1) - cntp
    block_rows = jnp.sum(cntp, axis=1)
    has_long = jnp.max(cntp, axis=1) > WIN
    tot = jnp.sum(cntp, axis=0)
    reg = (tot + ffn_tile - 1) // ffn_tile * ffn_tile
    base = jnp.cumsum(reg) - reg
    goff = base[None, :] + jnp.cumsum(cntp, axis=0) - cntp
    tile_end = jnp.cumsum(reg // ffn_tile)
    tile = jnp.arange(n_tiles_max, dtype=jnp.int32)
    tile_expert = jnp.minimum(jnp.sum(tile_end[None, :] <= tile[:, None], axis=1), N_EXPERTS - 1)
    tile_rows = jnp.clip((base + tot)[tile_expert] - tile * ffn_tile, 0, ffn_tile)
    i32 = lambda a: a.astype(jnp.int32)
    block_tables = (i32(cntp.reshape(-1)), i32(loff.reshape(-1)), i32(goff.reshape(-1)),
                    i32(block_rows), i32(has_long))
    tile_tables = (i32(tile_expert), i32(tile_rows), i32(tile_end[-1:]))
    return block_tables, i32(base + tot), i32(reg - tot), tile_tables


def _count_blocks(gates_t, params):
    n_tok = gates_t.shape[1]
    chunk = min(n_tok, COUNT_LANES * MOE_BLOCK // 16)
    per_chunk = chunk // MOE_BLOCK
    cnt = pl.pallas_call(
        _count_kernel,
        grid=(n_tok // chunk,),
        in_specs=[pl.BlockSpec((N_EXPERTS, chunk), lambda i: (0, i))],
        out_specs=pl.BlockSpec((1, N_EXPERTS, COUNT_LANES), lambda i: (i, 0, 0)),
        out_shape=jax.ShapeDtypeStruct((n_tok // chunk, N_EXPERTS, COUNT_LANES), F32),
        compiler_params=params,
        name="moe_count",
    )(gates_t)
    return jnp.swapaxes(cnt[:, :, :per_chunk], 1, 2).reshape(n_tok // MOE_BLOCK, N_EXPERTS).astype(jnp.int32)


def _stream_specs(x1, block_of):
    n_seq, seq_len, _ = x1.shape
    if seq_len < MOE_BLOCK:
        ns, ls = MOE_BLOCK // seq_len, seq_len
        x_map = lambda b, *_: (block_of(b), 0, 0)
        mod_map = x_map
    else:
        ns, ls = 1, MOE_BLOCK
        per_seq = seq_len // MOE_BLOCK
        x_map = lambda b, *_: (block_of(b) // per_seq, block_of(b) % per_seq, 0)
        mod_map = lambda b, *_: (block_of(b) // per_seq, 0, 0)
    return pl.BlockSpec((ns, ls, D_MODEL), x_map), pl.BlockSpec((ns, MOD_ROWS, D_MODEL), mod_map)


def _moe(stream_a, stream_b, w):
    (h2_a, g_a, x1_a, mod_a), (h2_b, g_b, x1_b, mod_b) = stream_a, stream_b
    nb_a, nb_b = h2_a.shape[0] // MOE_BLOCK, h2_b.shape[0] // MOE_BLOCK
    n_blocks = nb_a + nb_b
    n_tok = n_blocks * MOE_BLOCK
    ffn_tile = FFN_TILE
    rows_max = (n_tok * TOP_K + n_blocks * N_EXPERTS * SEG_ALIGN
                + N_EXPERTS * (ffn_tile - SEG_ALIGN))
    n_tiles_max = -(-rows_max // ffn_tile)
    rows_alloc = n_tiles_max * ffn_tile
    params = pltpu.CompilerParams(dimension_semantics=("arbitrary",), vmem_limit_bytes=VMEM_LIMIT)

    cnt = jnp.concatenate([_count_blocks(g_a, params), _count_blocks(g_b, params)], axis=0)
    block_tables, tail_off, tail_len, tile_tables = _segment_plan(cnt, n_tiles_max, ffn_tile)

    in_a = lambda b: jnp.minimum(b, nb_a - 1)
    in_b = lambda b: jnp.maximum(b - nb_a, 0)
    tok_specs = [pl.BlockSpec((N_EXPERTS, MOE_BLOCK), lambda b, *_: (0, in_a(b))),
                 pl.BlockSpec((N_EXPERTS, MOE_BLOCK), lambda b, *_: (0, in_b(b))),
                 pl.BlockSpec((MOE_BLOCK, D_MODEL), lambda b, *_: (in_a(b), 0)),
                 pl.BlockSpec((MOE_BLOCK, D_MODEL), lambda b, *_: (in_b(b), 0))]
    any_spec = pl.BlockSpec(memory_space=pl.ANY)
    perm_scratch = [pltpu.VMEM((BLOCK_ROWS, MOE_BLOCK), BF16), pltpu.VMEM((N_EXPERTS, MOE_BLOCK), F32),
                    pltpu.VMEM((N_EXPERTS, MOE_BLOCK), F32)]

    x_sorted = pl.pallas_call(
        functools.partial(_dispatch_kernel, n_blocks=n_blocks, n_blocks_a=nb_a),
        grid_spec=pltpu.PrefetchScalarGridSpec(
            num_scalar_prefetch=7, grid=(n_blocks,),
            in_specs=tok_specs, out_specs=any_spec,
            scratch_shapes=perm_scratch + [
                pltpu.VMEM((MOE_BLOCK, D_MODEL), BF16),
                pltpu.VMEM((2, BLOCK_ROWS, D_MODEL), BF16),
                pltpu.VMEM((ffn_tile, D_MODEL), BF16),
                pltpu.SemaphoreType.DMA((2,)), pltpu.SemaphoreType.DMA(())]),
        out_shape=jax.ShapeDtypeStruct((rows_alloc, D_MODEL), BF16),
        compiler_params=params,
        name="moe_dispatch",
    )(*block_tables, tail_off, tail_len, g_a, g_b, h2_a, h2_b)

    last_used = lambda i, nu: jnp.maximum(jnp.minimum(i, nu[0] - 1), 0)
    row_map = lambda i, te, tr, nu: (last_used(i, nu), 0)
    w_map = lambda i, te, tr, nu: (te[last_used(i, nu)], 0, 0)
    y_sorted = pl.pallas_call(
        _ffn_kernel,
        grid_spec=pltpu.PrefetchScalarGridSpec(
            num_scalar_prefetch=3, grid=(n_tiles_max,),
            in_specs=[pl.BlockSpec((ffn_tile, D_MODEL), row_map),
                      pl.BlockSpec((1, D_MODEL, EXPERT_FF), w_map),
                      pl.BlockSpec((1, D_MODEL, EXPERT_FF), w_map),
                      pl.BlockSpec((1, EXPERT_FF, D_MODEL), w_map)],
            out_specs=pl.BlockSpec((ffn_tile, D_MODEL), row_map),
            scratch_shapes=[pltpu.VMEM((D_MODEL, 2 * EXPERT_FF), BF16),
                            pltpu.VMEM((EXPERT_FF, D_MODEL), BF16)]),
        out_shape=jax.ShapeDtypeStruct((rows_alloc, D_MODEL), BF16),
        compiler_params=params,
        name="moe_ffn",
    )(*tile_tables, x_sorted, w["w_e_gate"], w["w_e_up"], w["w_e_down"])

    full2 = lambda b, *_: (0, 0)
    x_spec_a, mod_spec_a = _stream_specs(x1_a, in_a)
    x_spec_b, mod_spec_b = _stream_specs(x1_b, in_b)
    return pl.pallas_call(
        functools.partial(_combine_kernel, n_blocks=n_blocks, n_blocks_a=nb_a),
        grid_spec=pltpu.PrefetchScalarGridSpec(
            num_scalar_prefetch=5, grid=(n_blocks,),
            in_specs=tok_specs + [
                x_spec_a, x_spec_b, mod_spec_a, mod_spec_b,
                pl.BlockSpec((D_MODEL, 2 * SHARED_FF), full2),
                pl.BlockSpec((SHARED_FF, D_MODEL), full2),
                pl.BlockSpec((1, D_MODEL), full2),
                any_spec],
            out_specs=[x_spec_a, x_spec_b],
            scratch_shapes=perm_scratch + [
                pltpu.VMEM((COMBINE_SLOTS, BLOCK_ROWS, D_MODEL), BF16),
                pltpu.VMEM((MOE_BLOCK, D_MODEL), F32),
                pltpu.SemaphoreType.DMA((COMBINE_SLOTS,))]),
        out_shape=[jax.ShapeDtypeStruct(x1_a.shape, F32), jax.ShapeDtypeStruct(x1_b.shape, F32)],
        compiler_params=params,
        name="moe_combine",
    )(*block_tables, g_a, g_b, h2_a, h2_b, x1_a, x1_b, mod_a, mod_b,
      w["w_sh_gu"], w["w_sh_down"], w["g_post2"], y_sorted)


def _prep_weights(g_pre1, g_post1, w_in, w_gk_up, b_gk, g_gmlp_v, w_s, b_s, g_gla_o, w_out, g_pre2,
                  g_post2, w_router, b_router, w_e_gate, w_e_up, w_e_down, w_sh_gate, w_sh_up,
                  w_sh_down):
    row = lambda v: v.reshape(1, -1)
    wr = jnp.pad(w_router, ((0, 0), (0, LANES - N_EXPERTS)))
    wr_top = lax.bitcast_convert_type(
        lax.bitcast_convert_type(wr, jnp.uint32) & jnp.uint32(0xFFFF0000), F32)
    wr_hi = wr_top.astype(BF16)
    wr_lo = (wr - wr_top).astype(BF16)
    wr_split = jnp.concatenate([jnp.concatenate([wr_hi, wr_lo], axis=1),
                                jnp.concatenate([wr_hi, jnp.zeros_like(wr_lo)], axis=1)], axis=0)
    return {
        "g_pre1": row(g_pre1), "g_post1": row(g_post1), "g_pre2": row(g_pre2), "g_post2": row(g_post2),
        "w_in": jnp.pad(w_in, ((0, 0), (0, GK_PAD - GLA_GATE_RANK))).astype(BF16),
        "w_gk_up": jnp.pad(w_gk_up, ((0, GK_PAD - GLA_GATE_RANK), (0, 0))).astype(BF16),
        "b_gk": row(b_gk),
        "g_gmlp_v": row(g_gmlp_v),
        "w_s": w_s,
        "b_s_full": jnp.repeat(b_s.T, GMLP_HEAD, axis=1),
        "g_gla_o": g_gla_o,
        "w_out": w_out.astype(BF16),
        "w_router": wr_split,
        "b_router": b_router.reshape(N_EXPERTS, 1),
        "w_e_gate": w_e_gate, "w_e_up": w_e_up, "w_e_down": w_e_down,
        "w_sh_gu": jnp.concatenate([w_sh_gate, w_sh_up], axis=-1).astype(BF16),
        "w_sh_down": w_sh_down.astype(BF16),
    }


PROMPT_TILE = 1024
SAMPLE_TILE = 512


def _mix(x, mod, s0, w, *, emit_v):
    n_seq, seq_len, _ = x.shape
    s0_t = jnp.swapaxes(s0, -1, -2)
    if seq_len >= GMLP_CHUNK:
        outs = _mixer(x, mod, s0_t, w, seq_tile=PROMPT_TILE, chunk=GMLP_CHUNK, emit_v=emit_v)
    else:
        outs = _mixer(x, mod, s0_t, w, seq_tile=SAMPLE_TILE, chunk=seq_len, emit_v=emit_v)
    x1, h2, gates_t, st = outs[:4]
    v = outs[4].reshape(n_seq, seq_len, GMLP_GROUPS, GMLP_HEAD) if emit_v else None
    return (h2, gates_t, x1, mod), jnp.swapaxes(st, -1, -2), v


def kernel(x_prompt, x_sample, state_gla, c_prompt, c_sample, w_ada, b_ada, g_pre1, g_post1, w_in, w_gk_up, b_gk, g_gmlp_v, w_s, b_s, g_gla_o, w_out, g_pre2, g_post2, w_router, b_router, w_e_gate, w_e_up, w_e_down, w_sh_gate, w_sh_up, w_sh_down):
    depth = w_ada.shape[0]
    n_p, n_s = x_prompt.shape[0], x_sample.shape[0]
    c_all = jnp.concatenate([c_prompt, c_sample], axis=0)
    x_p, x_s = x_prompt, x_sample
    sp_list, ss_list, vs_list = [], [], []
    for l in range(depth):
        w = _prep_weights(g_pre1[l], g_post1[l], w_in[l], w_gk_up[l], b_gk[l], g_gmlp_v[l], w_s[l],
                          b_s[l], g_gla_o[l], w_out[l], g_pre2[l], g_post2[l], w_router[l],
                          b_router[l], w_e_gate[l], w_e_up[l], w_e_down[l], w_sh_gate[l], w_sh_up[l],
                          w_sh_down[l])
        mod = _ada(c_all, w_ada[l], b_ada[l])
        mod = jnp.pad(jnp.swapaxes(mod, 0, 1), ((0, 0), (0, MOD_ROWS - 6), (0, 0)))
        s0_p = jnp.zeros((n_p, GLA_HEADS, GLA_DK, GLA_DV), F32)
        moe_p, s_p, _ = _mix(x_p, mod[:n_p], s0_p, w, emit_v=False)
        moe_s, s_s, v_s = _mix(x_s, mod[n_p:], state_gla[l], w, emit_v=True)
        x_p, x_s = _moe(moe_p, moe_s, w)
        sp_list.append(s_p)
        ss_list.append(s_s)
        vs_list.append(v_s)
    return (x_p, x_s, jnp.stack(sp_list), jnp.stack(ss_list), jnp.stack(vs_list))
```

```python
import functools

import jax
import jax.numpy as jnp
from jax import lax
from jax.experimental import pallas as pl
from jax.experimental.pallas import tpu as pltpu

D_MODEL = 1024
GMLP_WIDTH = 512
GMLP_GROUPS = 4
GMLP_HEAD = 128
GMLP_CHUNK = 128
CAUSAL_BLOCK = 64
GLA_WIDTH = 512
GLA_HEADS = 4
GLA_DV = 128
GLA_DK = 64
GLA_KEY_WIDTH = 256
GLA_GATE_RANK = 16
GLA_GATE_NORMALIZER = 16.0
N_EXPERTS = 64
TOP_K = 8
N_EXPERT_GROUPS = 8
GROUP_SIZE = N_EXPERTS // N_EXPERT_GROUPS
TOPK_GROUPS = 4
EXPERT_FF = 256
SHARED_FF = 256
ROUTED_SCALE = 2.5
EPS = 1e-6

LANES = 128
GK_PAD = LANES
IN_WIDTH_PAD = 2 * GMLP_WIDTH + 2 * GLA_KEY_WIDTH + 2 * GLA_WIDTH + GLA_KEY_WIDTH
OFF_U = 0
OFF_VG = GMLP_WIDTH
OFF_Q = 2 * GMLP_WIDTH
OFF_K = OFF_Q + GLA_KEY_WIDTH
OFF_VL = OFF_K + GLA_KEY_WIDTH
OFF_R = OFF_VL + GLA_WIDTH
OFF_Z = OFF_R + GLA_WIDTH
MOD_ROWS = 8
VMEM_LIMIT = 56 * 1024 * 1024

F32 = jnp.float32
BF16 = jnp.bfloat16
NT_DIMS = (((1,), (1,)), ((), ()))
TN_DIMS = (((0,), (0,)), ((), ()))


def _rms(x, g):
    return x * lax.rsqrt(jnp.mean(x * x, axis=-1, keepdims=True) + EPS) * g


def _gelu(x):
    return 0.5 * x * (1.0 + jnp.tanh(0.7978845608028654 * (x + 0.044715 * (x * x * x))))


def _sigmoid(x):
    return 1.0 / (1.0 + jnp.exp(-x))


def _silu(x):
    return x * _sigmoid(x)


def _ada_kernel(c_ref, w_ref, b_ref, o_ref):
    a = _silu(c_ref[...])
    o_ref[0] = jnp.dot(a, w_ref[...], precision=lax.Precision.HIGHEST,
                       preferred_element_type=F32) + b_ref[0]


def _ada(c_all, w_ada, b_ada):
    n = c_all.shape[0]
    return pl.pallas_call(
        _ada_kernel,
        grid=(6,),
        in_specs=[pl.BlockSpec((n, D_MODEL), lambda j: (0, 0)),
                  pl.BlockSpec((D_MODEL, D_MODEL), lambda j: (0, j)),
                  pl.BlockSpec((1, 1, D_MODEL), lambda j: (j, 0, 0))],
        out_specs=pl.BlockSpec((1, n, D_MODEL), lambda j: (j, 0, 0)),
        out_shape=jax.ShapeDtypeStruct((6, n, D_MODEL), F32),
        compiler_params=pltpu.CompilerParams(vmem_limit_bytes=VMEM_LIMIT),
        name="ada",
    )(c_all, w_ada, b_ada.reshape(6, 1, D_MODEL))


def _fold_gate_kernel(a_ref, b_ref, o_ref):
    o_ref[...] = jnp.dot(a_ref[...], b_ref[...], precision=lax.Precision.HIGHEST,
                         preferred_element_type=F32)


def _fold_gate(w_in_gate, w_gk_up):
    a = jnp.pad(w_in_gate, ((0, 0), (0, GK_PAD - GLA_GATE_RANK)))
    b = jnp.pad(w_gk_up, ((0, GK_PAD - GLA_GATE_RANK), (0, 0)))
    return pl.pallas_call(
        _fold_gate_kernel,
        out_shape=jax.ShapeDtypeStruct((D_MODEL, GLA_KEY_WIDTH), F32),
        name="fold_gate",
    )(a, b)


def _route(logits_t, bias_t):
    t = logits_t.shape[1]
    scores = _sigmoid(logits_t)
    sel = scores + bias_t
    sub = lax.broadcasted_iota(jnp.int32, (GROUP_SIZE, t), 0)
    gscore = []
    for g in range(N_EXPERT_GROUPS):
        blk = sel[g * GROUP_SIZE:(g + 1) * GROUP_SIZE]
        m1 = jnp.max(blk, axis=0, keepdims=True)
        first = jnp.min(jnp.where(blk == m1, sub, GROUP_SIZE), axis=0, keepdims=True)
        m2 = jnp.max(jnp.where(sub == first, -jnp.inf, blk), axis=0, keepdims=True)
        gscore.append(m1 + m2)
    neg = jnp.full((GROUP_SIZE, t), -jnp.inf, F32)
    masked = []
    for g in range(N_EXPERT_GROUPS):
        rank = jnp.zeros((1, t), jnp.int32)
        for o in range(N_EXPERT_GROUPS):
            if o == g:
                continue
            ahead = (gscore[o] >= gscore[g]) if o < g else (gscore[o] > gscore[g])
            rank = rank + jnp.where(ahead, 1, 0)
        keep = jnp.broadcast_to(rank < TOPK_GROUPS, (GROUP_SIZE, t))
        masked.append(jnp.where(keep, sel[g * GROUP_SIZE:(g + 1) * GROUP_SIZE], neg))
    selm = jnp.concatenate(masked, axis=0)
    eidx = lax.broadcasted_iota(jnp.int32, (N_EXPERTS, t), 0)
    picked = jnp.zeros((N_EXPERTS, t), F32)
    for _ in range(TOP_K):
        best = jnp.max(selm, axis=0, keepdims=True)
        first = jnp.min(jnp.where(selm == best, eidx, N_EXPERTS), axis=0, keepdims=True)
        hit = eidx == first
        picked = jnp.where(hit, 1.0, picked)
        selm = jnp.where(hit, -jnp.inf, selm)
    chosen = jnp.where(picked > 0.0, scores, 0.0)
    denom = jnp.sum(chosen, axis=0, keepdims=True)
    return chosen * (ROUTED_SCALE / denom)


def _mixer_kernel(x_ref, mod_ref, s0_ref, gpre1_ref, win_ref, bgk_ref, ggv_ref,
                  ws_ref, bsf_ref, ggo_ref, wout_ref, gpost1_ref, gpre2_ref, wr_ref, br_ref,
                  x1_ref, h2_ref, gates_ref, st_ref, *rest,
                  chunk, n_chunks, chunk_is_seq, emit_v):
    if emit_v:
        v_ref, *rest = rest
    proj, mix_scr = rest
    ns, ls, _ = x_ref.shape
    tm = ns * ls

    mod_row = lambda i: mod_ref[:, i:i + 1, :]

    h = _rms(x_ref[...], gpre1_ref[...]) * (1.0 + mod_row(1)) + mod_row(0)
    proj[...] = jnp.dot(h.reshape(tm, D_MODEL).astype(BF16), win_ref[...],
                        preferred_element_type=F32)

    if chunk_is_seq:
        st_ref[...] = s0_ref[...]
    else:
        @pl.when(pl.program_id(1) == 0)
        def _():
            st_ref[...] = s0_ref[...]

    row = lax.broadcasted_iota(jnp.int32, (chunk, chunk), 0)
    col = lax.broadcasted_iota(jnp.int32, (chunk, chunk), 1)
    tri = jnp.where(row >= col, 1.0, 0.0).astype(BF16)
    tri2 = jnp.concatenate([tri, tri], axis=1)
    causal = row >= col
    block_causal = (row // CAUSAL_BLOCK) >= (col // CAUSAL_BLOCK)
    wmix = [jnp.where(block_causal, ws_ref[g, 0:chunk, 0:chunk], 0.0).astype(BF16)
            for g in range(GMLP_GROUPS)]
    mid = chunk // 2

    for c in range(n_chunks):
        rows = slice(c * chunk, (c + 1) * chunk)
        z = proj[rows, OFF_Z:OFF_Z + GLA_KEY_WIDTH] + bgk_ref[...]
        la = (jnp.minimum(z, 0.0) - jnp.log(1.0 + jnp.exp(-jnp.abs(z)))) * (1.0 / GLA_GATE_NORMALIZER)
        la_hi = la.astype(BF16)
        la_lo = (la - la_hi.astype(F32)).astype(BF16)
        proj[rows, OFF_Z:OFF_Z + GLA_KEY_WIDTH] = jnp.dot(
            tri2, jnp.concatenate([la_hi, la_lo], axis=0), preferred_element_type=F32)

    def chunk_body(c, carry):
        rows = pl.ds(pl.multiple_of(c * chunk, chunk), chunk)
        sidx = c if chunk_is_seq else 0

        vg = _gelu(proj[rows, OFF_VG:OFF_VG + GMLP_WIDTH])
        vn = _rms(vg, ggv_ref[...])
        if emit_v:
            v_ref[rows, :] = vn
        vnb = vn.astype(BF16)
        mixed = jnp.concatenate(
            [jnp.dot(wmix[g], vnb[:, g * GMLP_HEAD:(g + 1) * GMLP_HEAD], preferred_element_type=F32)
             for g in range(GMLP_GROUPS)], axis=-1) + bsf_ref[0:chunk, :]
        u = _gelu(proj[rows, OFF_U:OFF_U + GMLP_WIDTH])
        mix_scr[rows, 0:GMLP_WIDTH] = (u * mixed).astype(BF16)

        q = proj[rows, OFF_Q:OFF_Q + GLA_KEY_WIDTH] * (GLA_DK ** -0.5)
        k = proj[rows, OFF_K:OFF_K + GLA_KEY_WIDTH]
        b = proj[rows, OFF_Z:OFF_Z + GLA_KEY_WIDTH]
        b_mid = b[mid - 1:mid]
        b_end = b[chunk - 1:chunk]
        qt = (q * jnp.exp(b - b_mid)).astype(BF16)
        kt = (k * jnp.exp(b_mid - b)).astype(BF16)
        qs = (q * jnp.exp(b)).astype(BF16)
        kd = (k * jnp.exp(b_end - b)).astype(BF16)
        decay = jnp.exp(b_end)
        for hd in range(GLA_HEADS):
            ksl = slice(hd * GLA_DK, (hd + 1) * GLA_DK)
            vsl = slice(OFF_VL + hd * GLA_DV, OFF_VL + (hd + 1) * GLA_DV)
            rsl = slice(OFF_R + hd * GLA_DV, OFF_R + (hd + 1) * GLA_DV)
            att = lax.dot_general(qt[:, ksl], kt[:, ksl], NT_DIMS, preferred_element_type=F32)
            att = jnp.where(causal, att, 0.0).astype(BF16)
            vh = proj[rows, vsl].astype(BF16)
            st = st_ref[sidx, hd]
            o = jnp.dot(att, vh, preferred_element_type=F32) + lax.dot_general(
                qs[:, ksl], st.astype(BF16), NT_DIMS, preferred_element_type=F32)
            on = _rms(o, ggo_ref[hd:hd + 1, :])
            mix_scr[rows, GMLP_WIDTH + hd * GLA_DV:GMLP_WIDTH + (hd + 1) * GLA_DV] = (
                on * _silu(proj[rows, rsl])).astype(BF16)
            st_ref[sidx, hd] = st * decay[:, ksl] + lax.dot_general(
                vh, kd[:, ksl], TN_DIMS, preferred_element_type=F32)
        return carry

    lax.fori_loop(0, n_chunks, chunk_body, 0, unroll=4)

    mixo = jnp.dot(mix_scr[...], wout_ref[...], preferred_element_type=F32).reshape(ns, ls, D_MODEL)
    x1 = x_ref[...] + mod_row(2) * _rms(mixo, gpost1_ref[...])
    x1_ref[...] = x1
    h2 = (_rms(x1, gpre2_ref[...]) * (1.0 + mod_row(4)) + mod_row(3)).reshape(tm, D_MODEL)
    h2_hi = h2.astype(BF16)
    h2_ref[...] = h2_hi

    h2_lo = (h2 - h2_hi.astype(F32)).astype(BF16)
    parts = jnp.dot(jnp.concatenate([h2_hi, h2_lo], axis=-1), wr_ref[...], preferred_element_type=F32)
    logits = parts[:, 0:LANES] + parts[:, LANES:2 * LANES]
    gates_ref[...] = _route(logits.T[0:N_EXPERTS], br_ref[...])


def _mixer(x, mod, s0_t, w, *, seq_tile, chunk, emit_v):
    n_seq, seq_len, _ = x.shape
    chunk_is_seq = seq_len == chunk
    if chunk_is_seq:
        ns, ls = seq_tile // chunk, chunk
        grid = (1, n_seq // ns)
        x_map = lambda b, t: (t, 0, 0)
        seq_map3 = lambda b, t: (t, 0, 0)
        seq_map4 = lambda b, t: (t, 0, 0, 0)
    else:
        ns, ls = 1, seq_tile
        grid = (n_seq, seq_len // seq_tile)
        x_map = lambda b, t: (b, t, 0)
        seq_map3 = lambda b, t: (b, 0, 0)
        seq_map4 = lambda b, t: (b, 0, 0, 0)
    tm = ns * ls
    n_tok = n_seq * seq_len
    tok_block = lambda b, t: b * grid[1] + t
    tok_map = lambda b, t: (tok_block(b, t), 0)
    full2 = lambda b, t: (0, 0)
    full3 = lambda b, t: (0, 0, 0)

    in_specs = [
        pl.BlockSpec((ns, ls, D_MODEL), x_map),
        pl.BlockSpec((ns, MOD_ROWS, D_MODEL), seq_map3),
        pl.BlockSpec((ns, GLA_HEADS, GLA_DV, GLA_DK), seq_map4),
        pl.BlockSpec((1, D_MODEL), full2),
        pl.BlockSpec((D_MODEL, IN_WIDTH_PAD), full2),
        pl.BlockSpec((1, GLA_KEY_WIDTH), full2),
        pl.BlockSpec((1, GMLP_WIDTH), full2),
        pl.BlockSpec((GMLP_GROUPS, GMLP_CHUNK, GMLP_CHUNK), full3),
        pl.BlockSpec((GMLP_CHUNK, GMLP_WIDTH), full2),
        pl.BlockSpec((GLA_HEADS, GLA_DV), full2),
        pl.BlockSpec((D_MODEL, D_MODEL), full2),
        pl.BlockSpec((1, D_MODEL), full2),
        pl.BlockSpec((1, D_MODEL), full2),
        pl.BlockSpec((2 * D_MODEL, 2 * LANES), full2),
        pl.BlockSpec((N_EXPERTS, 1), full2),
    ]
    out_specs = [
        pl.BlockSpec((ns, ls, D_MODEL), x_map),
        pl.BlockSpec((tm, D_MODEL), tok_map),
        pl.BlockSpec((N_EXPERTS, tm), lambda b, t: (0, tok_block(b, t))),
        pl.BlockSpec((ns, GLA_HEADS, GLA_DV, GLA_DK), seq_map4),
    ]
    out_shape = [
        jax.ShapeDtypeStruct((n_seq, seq_len, D_MODEL), F32),
        jax.ShapeDtypeStruct((n_tok, D_MODEL), BF16),
        jax.ShapeDtypeStruct((N_EXPERTS, n_tok), F32),
        jax.ShapeDtypeStruct((n_seq, GLA_HEADS, GLA_DV, GLA_DK), F32),
    ]
    if emit_v:
        out_specs.append(pl.BlockSpec((tm, GMLP_WIDTH), tok_map))
        out_shape.append(jax.ShapeDtypeStruct((n_tok, GMLP_WIDTH), F32))

    kern = functools.partial(_mixer_kernel, chunk=chunk, n_chunks=tm // chunk,
                             chunk_is_seq=chunk_is_seq, emit_v=emit_v)
    return pl.pallas_call(
        kern,
        grid=grid,
        in_specs=in_specs,
        out_specs=out_specs,
        out_shape=out_shape,
        scratch_shapes=[pltpu.VMEM((tm, IN_WIDTH_PAD), F32), pltpu.VMEM((tm, D_MODEL), BF16)],
        compiler_params=pltpu.CompilerParams(
            dimension_semantics=("arbitrary", "arbitrary"), vmem_limit_bytes=VMEM_LIMIT),
        name="mixer_sample" if chunk_is_seq else "mixer_prompt",
    )(x, mod, s0_t, w["g_pre1"], w["w_in"], w["b_gk"], w["g_gmlp_v"], w["w_s"],
      w["b_s_full"], w["g_gla_o"], w["w_out"], w["g_post1"], w["g_pre2"], w["w_router"],
      w["b_router"])


MOE_BLOCK = 256
SEG_ALIGN = 16
WIN = 64
FFN_TILE = 1024
BLOCK_ROWS = MOE_BLOCK * TOP_K + SEG_ALIGN * N_EXPERTS + WIN
MAIN_ROWS = 2688
COMBINE_SLOTS = 3
COL_CHUNK = 256
COUNT_LANES = LANES


def _count_kernel(g_ref, o_ref):
    n = g_ref.shape[1]
    selb = jnp.where(g_ref[...] > 0.0, 1.0, 0.0).astype(BF16)
    tok = lax.broadcasted_iota(jnp.int32, (n, COUNT_LANES), 0)
    lane = lax.broadcasted_iota(jnp.int32, (n, COUNT_LANES), 1)
    ind = jnp.where(tok // MOE_BLOCK == lane, 1.0, 0.0).astype(BF16)
    o_ref[0] = jnp.dot(selb, ind, preferred_element_type=F32)


def _fill_permutation(p_scr, rk_scr, g_ref, cntp_ref, loff_ref, long_ref, blk, weighted):
    tb = g_ref.shape[1]
    g = g_ref[...]
    sel = g > 0.0
    r_i = lax.broadcasted_iota(jnp.int32, (tb, tb), 0)
    c_i = lax.broadcasted_iota(jnp.int32, (tb, tb), 1)
    earlier = jnp.where(r_i < c_i, 1.0, 0.0).astype(BF16)
    rank = jnp.dot(jnp.where(sel, 1.0, 0.0).astype(BF16), earlier, preferred_element_type=F32)
    rk = jnp.where(sel, rank, -1.0)
    p_scr[...] = jnp.zeros_like(p_scr)
    jwin = lax.broadcasted_iota(jnp.int32, (WIN, tb), 0).astype(F32)

    for e in range(N_EXPERTS):
        off = pl.multiple_of(loff_ref[blk * N_EXPERTS + e], SEG_ALIGN)
        val = g[e:e + 1] if weighted else 1.0
        p_scr[pl.ds(off, WIN), :] = jnp.where(rk[e:e + 1] == jwin, val, 0.0).astype(BF16)

    @pl.when(long_ref[blk] > 0)
    def _():
        rk_scr[...] = rk

        def expert_body(e, c):
            n = cntp_ref[blk * N_EXPERTS + e]
            off = loff_ref[blk * N_EXPERTS + e]
            rke = rk_scr[pl.ds(e, 1), :]
            val = g_ref[pl.ds(e, 1), :] if weighted else 1.0

            def win_body(wi, c2):
                rows = pl.ds(pl.multiple_of(off + wi * WIN, SEG_ALIGN), WIN)
                j = jwin + (wi * WIN).astype(F32)
                new = jnp.where(rke == j, val, 0.0)
                p_scr[rows, :] = jnp.where(j < n.astype(F32), new, p_scr[rows, :].astype(F32)).astype(BF16)
                return c2

            lax.fori_loop(1, (n + WIN - 1) // WIN, win_body, 0)
            return c

        lax.fori_loop(0, N_EXPERTS, expert_body, 0)


def _start_segments(cntp_ref, loff_ref, blk, make_copy):
    for e in range(N_EXPERTS):
        make_copy(e, pl.multiple_of(loff_ref[blk * N_EXPERTS + e], SEG_ALIGN),
                  pl.multiple_of(cntp_ref[blk * N_EXPERTS + e], SEG_ALIGN)).start()


def _dispatch_kernel(cntp_ref, loff_ref, goff_ref, rows_ref, long_ref, toff_ref, tlen_ref,
                     ga_ref, gb_ref, h2a_ref, h2b_ref, xs_hbm,
                     p_scr, rk_scr, g_scr, h2_scr, xs_scr, zero_scr, sems, tail_sem,
                     *, n_blocks, n_blocks_a):
    b = pl.program_id(0)
    slot = lax.rem(b, 2)
    from_a = b < n_blocks_a
    g_scr[...] = jnp.where(from_a, ga_ref[...], gb_ref[...])
    h2_scr[...] = jnp.where(from_a, h2a_ref[...], h2b_ref[...])
    g_ref, h2_ref = g_scr, h2_scr

    def seg_copy(blk, s):
        def make(e, off, n):
            dst = pl.multiple_of(goff_ref[blk * N_EXPERTS + e], SEG_ALIGN)
            return pltpu.make_async_copy(xs_scr.at[s, pl.ds(off, n)], xs_hbm.at[pl.ds(dst, n)], sems.at[s])
        return make

    def wait_block(blk, s):
        n = pl.multiple_of(rows_ref[blk], SEG_ALIGN)
        pltpu.make_async_copy(xs_scr.at[s, pl.ds(0, n)], xs_hbm.at[pl.ds(0, n)], sems.at[s]).wait()

    def for_tails(action):
        def body(e, c):
            n = tlen_ref[e]

            @pl.when(n > 0)
            def _():
                dst = pl.multiple_of(toff_ref[e], SEG_ALIGN)
                nn = pl.multiple_of(n, SEG_ALIGN)
                action(pltpu.make_async_copy(zero_scr.at[pl.ds(0, nn)], xs_hbm.at[pl.ds(dst, nn)], tail_sem))

            return c

        lax.fori_loop(0, N_EXPERTS, body, 0)

    @pl.when(b == 0)
    def _():
        zero_scr[...] = jnp.zeros_like(zero_scr)
        for_tails(lambda cp: cp.start())

    @pl.when(b >= 2)
    def _():
        wait_block(b - 2, slot)

    _fill_permutation(p_scr, rk_scr, g_ref, cntp_ref, loff_ref, long_ref, b, weighted=False)
    def sort_rows(rows):
        for c in range(D_MODEL // COL_CHUNK):
            cols = slice(c * COL_CHUNK, (c + 1) * COL_CHUNK)
            xs = jnp.dot(p_scr[rows, :], h2_ref[:, cols], preferred_element_type=F32)
            xs_scr[slot, rows, cols] = xs.astype(BF16)

    sort_rows(slice(0, MAIN_ROWS))

    @pl.when(rows_ref[b] > MAIN_ROWS)
    def _():
        sort_rows(slice(MAIN_ROWS, BLOCK_ROWS))

    _start_segments(cntp_ref, loff_ref, b, seg_copy(b, slot))

    @pl.when(b == n_blocks - 1)
    def _():
        for_tails(lambda cp: cp.wait())
        if n_blocks >= 2:
            wait_block(b - 1, 1 - slot)
        wait_block(b, slot)


def _ffn_kernel(te_ref, nu_ref, x_ref, wg_ref, wu_ref, wd_ref, y_ref, wgu_scr, wd_scr):
    i = pl.program_id(0)

    @pl.when(i < nu_ref[0])
    def _():
        @pl.when(jnp.logical_or(i == 0, te_ref[i] != te_ref[jnp.maximum(i - 1, 0)]))
        def _():
            wgu_scr[:, 0:EXPERT_FF] = wg_ref[0].astype(BF16)
            wgu_scr[:, EXPERT_FF:2 * EXPERT_FF] = wu_ref[0].astype(BF16)
            wd_scr[...] = wd_ref[0].astype(BF16)

        gu = jnp.dot(x_ref[...], wgu_scr[...], preferred_element_type=F32)
        hid = _silu(gu[:, 0:EXPERT_FF]) * gu[:, EXPERT_FF:2 * EXPERT_FF]
        y_ref[...] = jnp.dot(hid.astype(BF16), wd_scr[...], preferred_element_type=F32).astype(BF16)


def _combine_kernel(cntp_ref, loff_ref, goff_ref, rows_ref, long_ref,
                    ga_ref, gb_ref, h2a_ref, h2b_ref, x1a_ref, x1b_ref, moda_ref, modb_ref,
                    wsgu_ref, wsd_ref, gpost2_ref, ys_hbm, outa_ref, outb_ref,
                    p_scr, rk_scr, g_scr, ys_scr, routed_scr, sems, *, n_blocks, n_blocks_a):
    b = pl.program_id(0)
    slot = lax.rem(b, COMBINE_SLOTS)
    from_a = b < n_blocks_a
    g_scr[...] = jnp.where(from_a, ga_ref[...], gb_ref[...])

    def seg_copy(blk, s):
        def make(e, off, n):
            src = pl.multiple_of(goff_ref[blk * N_EXPERTS + e], SEG_ALIGN)
            return pltpu.make_async_copy(ys_hbm.at[pl.ds(src, n)], ys_scr.at[s, pl.ds(off, n)], sems.at[s])
        return make

    @pl.when(b == 0)
    def _():
        ys_scr[...] = jnp.zeros_like(ys_scr)
        for ahead in range(min(COMBINE_SLOTS - 1, n_blocks)):
            _start_segments(cntp_ref, loff_ref, ahead, seg_copy(ahead, ahead))

    ahead = b + COMBINE_SLOTS - 1

    @pl.when(ahead < n_blocks)
    def _():
        _start_segments(cntp_ref, loff_ref, ahead, seg_copy(ahead, lax.rem(ahead, COMBINE_SLOTS)))

    _fill_permutation(p_scr, rk_scr, g_scr, cntp_ref, loff_ref, long_ref, b, weighted=True)
    n_rows = pl.multiple_of(rows_ref[b], SEG_ALIGN)
    pltpu.make_async_copy(ys_hbm.at[pl.ds(0, n_rows)], ys_scr.at[slot, pl.ds(0, n_rows)], sems.at[slot]).wait()

    def unsort_rows(rows):
        return lax.dot_general(p_scr[rows, :], ys_scr[slot, rows, :], TN_DIMS, preferred_element_type=F32)

    routed_scr[...] = unsort_rows(slice(0, MAIN_ROWS))

    @pl.when(rows_ref[b] > MAIN_ROWS)
    def _():
        routed_scr[...] += unsort_rows(slice(MAIN_ROWS, BLOCK_ROWS))

    hx = jnp.where(from_a, h2a_ref[...], h2b_ref[...])
    sgu = jnp.dot(hx, wsgu_ref[...], preferred_element_type=F32)
    shid = _silu(sgu[:, 0:SHARED_FF]) * sgu[:, SHARED_FF:2 * SHARED_FF]
    f = routed_scr[...] + jnp.dot(shid.astype(BF16), wsd_ref[...], preferred_element_type=F32)
    routed_scr[...] = _rms(f, gpost2_ref[...])

    def finish(x1_ref, mod_ref, out_ref):
        gate2 = mod_ref[:, 5:6, :]
        out_ref[...] = x1_ref[...] + gate2 * routed_scr[...].reshape(x1_ref.shape)

    @pl.when(from_a)
    def _():
        finish(x1a_ref, moda_ref, outa_ref)

    @pl.when(jnp.logical_not(from_a))
    def _():
        finish(x1b_ref, modb_ref, outb_ref)


def _segment_plan(cnt, n_tiles_max, ffn_tile):
    cntp = jnp.maximum((cnt + SEG_ALIGN - 1) // SEG_ALIGN, 1) * SEG_ALIGN
    loff = jnp.cumsum(cntp, axis=1) - cntp
    block_rows = jnp.sum(cntp, axis=1)
    has_long = jnp.max(cntp, axis=1) > WIN
    tot = jnp.sum(cntp, axis=0)
    reg = (tot + ffn_tile - 1) // ffn_tile * ffn_tile
    base = jnp.cumsum(reg) - reg
    goff = base[None, :] + jnp.cumsum(cntp, axis=0) - cntp
    tile_end = jnp.cumsum(reg // ffn_tile)
    tile = jnp.arange(n_tiles_max, dtype=jnp.int32)
    tile_expert = jnp.minimum(jnp.sum(tile_end[None, :] <= tile[:, None], axis=1), N_EXPERTS - 1)
    i32 = lambda a: a.astype(jnp.int32)
    block_tables = (i32(cntp.reshape(-1)), i32(loff.reshape(-1)), i32(goff.reshape(-1)),
                    i32(block_rows), i32(has_long))
    return block_tables, i32(base + tot), i32(reg - tot), i32(tile_expert), i32(tile_end[-1:])


def _count_blocks(gates_t, params):
    n_tok = gates_t.shape[1]
    chunk = min(n_tok, COUNT_LANES * MOE_BLOCK // 16)
    per_chunk = chunk // MOE_BLOCK
    cnt = pl.pallas_call(
        _count_kernel,
        grid=(n_tok // chunk,),
        in_specs=[pl.BlockSpec((N_EXPERTS, chunk), lambda i: (0, i))],
        out_specs=pl.BlockSpec((1, N_EXPERTS, COUNT_LANES), lambda i: (i, 0, 0)),
        out_shape=jax.ShapeDtypeStruct((n_tok // chunk, N_EXPERTS, COUNT_LANES), F32),
        compiler_params=params,
        name="moe_count",
    )(gates_t)
    return jnp.swapaxes(cnt[:, :, :per_chunk], 1, 2).reshape(n_tok // MOE_BLOCK, N_EXPERTS).astype(jnp.int32)


def _stream_specs(x1, block_of):
    n_seq, seq_len, _ = x1.shape
    if seq_len < MOE_BLOCK:
        ns, ls = MOE_BLOCK // seq_len, seq_len
        x_map = lambda b, *_: (block_of(b), 0, 0)
        mod_map = x_map
    else:
        ns, ls = 1, MOE_BLOCK
        per_seq = seq_len // MOE_BLOCK
        x_map = lambda b, *_: (block_of(b) // per_seq, block_of(b) % per_seq, 0)
        mod_map = lambda b, *_: (block_of(b) // per_seq, 0, 0)
    return pl.BlockSpec((ns, ls, D_MODEL), x_map), pl.BlockSpec((ns, MOD_ROWS, D_MODEL), mod_map)


def _moe(stream_a, stream_b, w):
    (h2_a, g_a, x1_a, mod_a), (h2_b, g_b, x1_b, mod_b) = stream_a, stream_b
    nb_a, nb_b = h2_a.shape[0] // MOE_BLOCK, h2_b.shape[0] // MOE_BLOCK
    n_blocks = nb_a + nb_b
    n_tok = n_blocks * MOE_BLOCK
    ffn_tile = FFN_TILE
    rows_max = (n_tok * TOP_K + n_blocks * N_EXPERTS * SEG_ALIGN
                + N_EXPERTS * (ffn_tile - SEG_ALIGN))
    n_tiles_max = -(-rows_max // ffn_tile)
    rows_alloc = n_tiles_max * ffn_tile
    params = pltpu.CompilerParams(dimension_semantics=("arbitrary",), vmem_limit_bytes=VMEM_LIMIT)

    cnt = jnp.concatenate([_count_blocks(g_a, params), _count_blocks(g_b, params)], axis=0)
    block_tables, tail_off, tail_len, tile_expert, n_used = _segment_plan(cnt, n_tiles_max, ffn_tile)

    in_a = lambda b: jnp.minimum(b, nb_a - 1)
    in_b = lambda b: jnp.maximum(b - nb_a, 0)
    tok_specs = [pl.BlockSpec((N_EXPERTS, MOE_BLOCK), lambda b, *_: (0, in_a(b))),
                 pl.BlockSpec((N_EXPERTS, MOE_BLOCK), lambda b, *_: (0, in_b(b))),
                 pl.BlockSpec((MOE_BLOCK, D_MODEL), lambda b, *_: (in_a(b), 0)),
                 pl.BlockSpec((MOE_BLOCK, D_MODEL), lambda b, *_: (in_b(b), 0))]
    any_spec = pl.BlockSpec(memory_space=pl.ANY)
    perm_scratch = [pltpu.VMEM((BLOCK_ROWS, MOE_BLOCK), BF16), pltpu.VMEM((N_EXPERTS, MOE_BLOCK), F32),
                    pltpu.VMEM((N_EXPERTS, MOE_BLOCK), F32)]

    x_sorted = pl.pallas_call(
        functools.partial(_dispatch_kernel, n_blocks=n_blocks, n_blocks_a=nb_a),
        grid_spec=pltpu.PrefetchScalarGridSpec(
            num_scalar_prefetch=7, grid=(n_blocks,),
            in_specs=tok_specs, out_specs=any_spec,
            scratch_shapes=perm_scratch + [
                pltpu.VMEM((MOE_BLOCK, D_MODEL), BF16),
                pltpu.VMEM((2, BLOCK_ROWS, D_MODEL), BF16),
                pltpu.VMEM((ffn_tile, D_MODEL), BF16),
                pltpu.SemaphoreType.DMA((2,)), pltpu.SemaphoreType.DMA(())]),
        out_shape=jax.ShapeDtypeStruct((rows_alloc, D_MODEL), BF16),
        compiler_params=params,
        name="moe_dispatch",
    )(*block_tables, tail_off, tail_len, g_a, g_b, h2_a, h2_b)

    last_used = lambda i, nu: jnp.maximum(jnp.minimum(i, nu[0] - 1), 0)
    row_map = lambda i, te, nu: (last_used(i, nu), 0)
    w_map = lambda i, te, nu: (te[last_used(i, nu)], 0, 0)
    y_sorted = pl.pallas_call(
        _ffn_kernel,
        grid_spec=pltpu.PrefetchScalarGridSpec(
            num_scalar_prefetch=2, grid=(n_tiles_max,),
            in_specs=[pl.BlockSpec((ffn_tile, D_MODEL), row_map),
                      pl.BlockSpec((1, D_MODEL, EXPERT_FF), w_map),
                      pl.BlockSpec((1, D_MODEL, EXPERT_FF), w_map),
                      pl.BlockSpec((1, EXPERT_FF, D_MODEL), w_map)],
            out_specs=pl.BlockSpec((ffn_tile, D_MODEL), row_map),
            scratch_shapes=[pltpu.VMEM((D_MODEL, 2 * EXPERT_FF), BF16),
                            pltpu.VMEM((EXPERT_FF, D_MODEL), BF16)]),
        out_shape=jax.ShapeDtypeStruct((rows_alloc, D_MODEL), BF16),
        compiler_params=params,
        name="moe_ffn",
    )(tile_expert, n_used, x_sorted, w["w_e_gate"], w["w_e_up"], w["w_e_down"])

    full2 = lambda b, *_: (0, 0)
    x_spec_a, mod_spec_a = _stream_specs(x1_a, in_a)
    x_spec_b, mod_spec_b = _stream_specs(x1_b, in_b)
    return pl.pallas_call(
        functools.partial(_combine_kernel, n_blocks=n_blocks, n_blocks_a=nb_a),
        grid_spec=pltpu.PrefetchScalarGridSpec(
            num_scalar_prefetch=5, grid=(n_blocks,),
            in_specs=tok_specs + [
                x_spec_a, x_spec_b, mod_spec_a, mod_spec_b,
                pl.BlockSpec((D_MODEL, 2 * SHARED_FF), full2),
                pl.BlockSpec((SHARED_FF, D_MODEL), full2),
                pl.BlockSpec((1, D_MODEL), full2),
                any_spec],
            out_specs=[x_spec_a, x_spec_b],
            scratch_shapes=perm_scratch + [
                pltpu.VMEM((COMBINE_SLOTS, BLOCK_ROWS, D_MODEL), BF16),
                pltpu.VMEM((MOE_BLOCK, D_MODEL), F32),
                pltpu.SemaphoreType.DMA((COMBINE_SLOTS,))]),
        out_shape=[jax.ShapeDtypeStruct(x1_a.shape, F32), jax.ShapeDtypeStruct(x1_b.shape, F32)],
        compiler_params=params,
        name="moe_combine",
    )(*block_tables, g_a, g_b, h2_a, h2_b, x1_a, x1_b, mod_a, mod_b,
      w["w_sh_gu"], w["w_sh_down"], w["g_post2"], y_sorted)


def _prep_weights(g_pre1, g_post1, w_in, w_gk_up, b_gk, g_gmlp_v, w_s, b_s, g_gla_o, w_out, g_pre2,
                  g_post2, w_router, b_router, w_e_gate, w_e_up, w_e_down, w_sh_gate, w_sh_up,
                  w_sh_down):
    row = lambda v: v.reshape(1, -1)
    wr = jnp.pad(w_router, ((0, 0), (0, LANES - N_EXPERTS)))
    wr_top = lax.bitcast_convert_type(
        lax.bitcast_convert_type(wr, jnp.uint32) & jnp.uint32(0xFFFF0000), F32)
    wr_hi = wr_top.astype(BF16)
    wr_lo = (wr - wr_top).astype(BF16)
    wr_split = jnp.concatenate([jnp.concatenate([wr_hi, wr_lo], axis=1),
                                jnp.concatenate([wr_hi, jnp.zeros_like(wr_lo)], axis=1)], axis=0)
    return {
        "g_pre1": row(g_pre1), "g_post1": row(g_post1), "g_pre2": row(g_pre2), "g_post2": row(g_post2),
        "w_in": jnp.concatenate([w_in[:, :OFF_Z], _fold_gate(w_in[:, OFF_Z:], w_gk_up)],
                                axis=1).astype(BF16),
        "b_gk": row(b_gk),
        "g_gmlp_v": row(g_gmlp_v),
        "w_s": w_s,
        "b_s_full": jnp.repeat(b_s.T, GMLP_HEAD, axis=1),
        "g_gla_o": g_gla_o,
        "w_out": w_out.astype(BF16),
        "w_router": wr_split,
        "b_router": b_router.reshape(N_EXPERTS, 1),
        "w_e_gate": w_e_gate, "w_e_up": w_e_up, "w_e_down": w_e_down,
        "w_sh_gu": jnp.concatenate([w_sh_gate, w_sh_up], axis=-1).astype(BF16),
        "w_sh_down": w_sh_down.astype(BF16),
    }


PROMPT_TILE = 1024
SAMPLE_TILE = 256


def _mix(x, mod, s0, w, *, emit_v):
    n_seq, seq_len, _ = x.shape
    s0_t = jnp.swapaxes(s0, -1, -2)
    if seq_len >= GMLP_CHUNK:
        outs = _mixer(x, mod, s0_t, w, seq_tile=PROMPT_TILE, chunk=GMLP_CHUNK, emit_v=emit_v)
    else:
        outs = _mixer(x, mod, s0_t, w, seq_tile=SAMPLE_TILE, chunk=seq_len, emit_v=emit_v)
    x1, h2, gates_t, st = outs[:4]
    v = outs[4].reshape(n_seq, seq_len, GMLP_GROUPS, GMLP_HEAD) if emit_v else None
    return (h2, gates_t, x1, mod), jnp.swapaxes(st, -1, -2), v


def kernel(x_prompt, x_sample, state_gla, c_prompt, c_sample, w_ada, b_ada, g_pre1, g_post1, w_in, w_gk_up, b_gk, g_gmlp_v, w_s, b_s, g_gla_o, w_out, g_pre2, g_post2, w_router, b_router, w_e_gate, w_e_up, w_e_down, w_sh_gate, w_sh_up, w_sh_down):
    depth = w_ada.shape[0]
    n_p, n_s = x_prompt.shape[0], x_sample.shape[0]
    c_all = jnp.concatenate([c_prompt, c_sample], axis=0)
    x_p, x_s = x_prompt, x_sample
    sp_list, ss_list, vs_list = [], [], []
    for l in range(depth):
        w = _prep_weights(g_pre1[l], g_post1[l], w_in[l], w_gk_up[l], b_gk[l], g_gmlp_v[l], w_s[l],
                          b_s[l], g_gla_o[l], w_out[l], g_pre2[l], g_post2[l], w_router[l],
                          b_router[l], w_e_gate[l], w_e_up[l], w_e_down[l], w_sh_gate[l], w_sh_up[l],
                          w_sh_down[l])
        mod = _ada(c_all, w_ada[l], b_ada[l])
        mod = jnp.pad(jnp.swapaxes(mod, 0, 1), ((0, 0), (0, MOD_ROWS - 6), (0, 0)))
        s0_p = jnp.zeros((n_p, GLA_HEADS, GLA_DK, GLA_DV), F32)
        moe_p, s_p, _ = _mix(x_p, mod[:n_p], s0_p, w, emit_v=False)
        moe_s, s_s, v_s = _mix(x_s, mod[n_p:], state_gla[l], w, emit_v=True)
        x_p, x_s = _moe(moe_p, moe_s, w)
        sp_list.append(s_p)
        ss_list.append(s_s)
        vs_list.append(v_s)
    return (x_p, x_s, jnp.stack(sp_list), jnp.stack(ss_list), jnp.stack(vs_list))
```

```python
import functools

import jax
import jax.numpy as jnp
from jax import lax
from jax.experimental import pallas as pl
from jax.experimental.pallas import tpu as pltpu

D_MODEL = 1024
GMLP_WIDTH = 512
GMLP_GROUPS = 4
GMLP_HEAD = 128
GMLP_CHUNK = 128
CAUSAL_BLOCK = 64
GLA_WIDTH = 512
GLA_HEADS = 4
GLA_DV = 128
GLA_DK = 64
GLA_KEY_WIDTH = 256
GLA_GATE_RANK = 16
GLA_GATE_NORMALIZER = 16.0
N_EXPERTS = 64
TOP_K = 8
N_EXPERT_GROUPS = 8
GROUP_SIZE = N_EXPERTS // N_EXPERT_GROUPS
TOPK_GROUPS = 4
EXPERT_FF = 256
SHARED_FF = 256
ROUTED_SCALE = 2.5
EPS = 1e-6

LANES = 128
GK_PAD = LANES
IN_WIDTH_PAD = 2 * GMLP_WIDTH + 2 * GLA_KEY_WIDTH + 2 * GLA_WIDTH + GLA_KEY_WIDTH
OFF_U = 0
OFF_VG = GMLP_WIDTH
OFF_Q = 2 * GMLP_WIDTH
OFF_K = OFF_Q + GLA_KEY_WIDTH
OFF_VL = OFF_K + GLA_KEY_WIDTH
OFF_R = OFF_VL + GLA_WIDTH
OFF_Z = OFF_R + GLA_WIDTH
MOD_ROWS = 8
VMEM_LIMIT = 56 * 1024 * 1024

F32 = jnp.float32
BF16 = jnp.bfloat16
NT_DIMS = (((1,), (1,)), ((), ()))
TN_DIMS = (((0,), (0,)), ((), ()))


def _rms(x, g):
    return x * lax.rsqrt(jnp.mean(x * x, axis=-1, keepdims=True) + EPS) * g


def _gelu(x):
    return 0.5 * x * (1.0 + jnp.tanh(0.7978845608028654 * (x + 0.044715 * (x * x * x))))


def _sigmoid(x):
    return 1.0 / (1.0 + jnp.exp(-x))


def _silu(x):
    return x * _sigmoid(x)


def _ada_kernel(c_ref, w_ref, b_ref, o_ref):
    a = _silu(c_ref[...])
    o_ref[0] = jnp.dot(a, w_ref[...], precision=lax.Precision.HIGHEST,
                       preferred_element_type=F32) + b_ref[0]


def _ada(c_all, w_ada, b_ada):
    n = c_all.shape[0]
    return pl.pallas_call(
        _ada_kernel,
        grid=(6,),
        in_specs=[pl.BlockSpec((n, D_MODEL), lambda j: (0, 0)),
                  pl.BlockSpec((D_MODEL, D_MODEL), lambda j: (0, j)),
                  pl.BlockSpec((1, 1, D_MODEL), lambda j: (j, 0, 0))],
        out_specs=pl.BlockSpec((1, n, D_MODEL), lambda j: (j, 0, 0)),
        out_shape=jax.ShapeDtypeStruct((6, n, D_MODEL), F32),
        compiler_params=pltpu.CompilerParams(vmem_limit_bytes=VMEM_LIMIT),
        name="ada",
    )(c_all, w_ada, b_ada.reshape(6, 1, D_MODEL))


def _fold_gate_kernel(a_ref, b_ref, o_ref):
    o_ref[...] = jnp.dot(a_ref[...], b_ref[...], precision=lax.Precision.HIGHEST,
                         preferred_element_type=F32)


def _fold_gate(w_in_gate, w_gk_up):
    a = jnp.pad(w_in_gate, ((0, 0), (0, GK_PAD - GLA_GATE_RANK)))
    b = jnp.pad(w_gk_up, ((0, GK_PAD - GLA_GATE_RANK), (0, 0)))
    return pl.pallas_call(
        _fold_gate_kernel,
        out_shape=jax.ShapeDtypeStruct((D_MODEL, GLA_KEY_WIDTH), F32),
        name="fold_gate",
    )(a, b)


def _route(logits_t, bias_t):
    t = logits_t.shape[1]
    scores = _sigmoid(logits_t)
    sel = scores + bias_t
    sub = lax.broadcasted_iota(jnp.int32, (GROUP_SIZE, t), 0)
    gscore = []
    for g in range(N_EXPERT_GROUPS):
        blk = sel[g * GROUP_SIZE:(g + 1) * GROUP_SIZE]
        m1 = jnp.max(blk, axis=0, keepdims=True)
        first = jnp.min(jnp.where(blk == m1, sub, GROUP_SIZE), axis=0, keepdims=True)
        m2 = jnp.max(jnp.where(sub == first, -jnp.inf, blk), axis=0, keepdims=True)
        gscore.append(m1 + m2)
    neg = jnp.full((GROUP_SIZE, t), -jnp.inf, F32)
    masked = []
    for g in range(N_EXPERT_GROUPS):
        rank = jnp.zeros((1, t), jnp.int32)
        for o in range(N_EXPERT_GROUPS):
            if o == g:
                continue
            ahead = (gscore[o] >= gscore[g]) if o < g else (gscore[o] > gscore[g])
            rank = rank + jnp.where(ahead, 1, 0)
        keep = jnp.broadcast_to(rank < TOPK_GROUPS, (GROUP_SIZE, t))
        masked.append(jnp.where(keep, sel[g * GROUP_SIZE:(g + 1) * GROUP_SIZE], neg))
    selm = jnp.concatenate(masked, axis=0)
    eidx = lax.broadcasted_iota(jnp.int32, (N_EXPERTS, t), 0)
    picked = jnp.zeros((N_EXPERTS, t), F32)
    for _ in range(TOP_K):
        best = jnp.max(selm, axis=0, keepdims=True)
        first = jnp.min(jnp.where(selm == best, eidx, N_EXPERTS), axis=0, keepdims=True)
        hit = eidx == first
        picked = jnp.where(hit, 1.0, picked)
        selm = jnp.where(hit, -jnp.inf, selm)
    chosen = jnp.where(picked > 0.0, scores, 0.0)
    denom = jnp.sum(chosen, axis=0, keepdims=True)
    return chosen * (ROUTED_SCALE / denom)


def _mixer_kernel(x_ref, mod_ref, s0_ref, gpre1_ref, win_ref, bgk_ref, ggv_ref,
                  ws_ref, bsf_ref, ggo_ref, wout_ref, gpost1_ref, gpre2_ref, wr_ref, br_ref,
                  x1_ref, h2_ref, gates_ref, st_ref, *rest,
                  chunk, n_chunks, chunk_is_seq, emit_v):
    if emit_v:
        v_ref, *rest = rest
    proj, mix_scr = rest
    ns, ls, _ = x_ref.shape
    tm = ns * ls

    mod_row = lambda i: mod_ref[:, i:i + 1, :]

    h = _rms(x_ref[...], gpre1_ref[...]) * (1.0 + mod_row(1)) + mod_row(0)
    proj[...] = jnp.dot(h.reshape(tm, D_MODEL).astype(BF16), win_ref[...],
                        preferred_element_type=F32)

    if chunk_is_seq:
        st_ref[...] = s0_ref[...]
    else:
        @pl.when(pl.program_id(1) == 0)
        def _():
            st_ref[...] = s0_ref[...]

    row = lax.broadcasted_iota(jnp.int32, (chunk, chunk), 0)
    col = lax.broadcasted_iota(jnp.int32, (chunk, chunk), 1)
    tri = jnp.where(row >= col, 1.0, 0.0).astype(BF16)
    tri2 = jnp.concatenate([tri, tri], axis=1)
    causal = row >= col
    block_causal = (row // CAUSAL_BLOCK) >= (col // CAUSAL_BLOCK)
    wmix = [jnp.where(block_causal, ws_ref[g, 0:chunk, 0:chunk], 0.0).astype(BF16)
            for g in range(GMLP_GROUPS)]
    mid = chunk // 2

    for c in range(n_chunks):
        rows = slice(c * chunk, (c + 1) * chunk)
        z = proj[rows, OFF_Z:OFF_Z + GLA_KEY_WIDTH] + bgk_ref[...]
        la = (jnp.minimum(z, 0.0) - jnp.log(1.0 + jnp.exp(-jnp.abs(z)))) * (1.0 / GLA_GATE_NORMALIZER)
        la_hi = la.astype(BF16)
        la_lo = (la - la_hi.astype(F32)).astype(BF16)
        proj[rows, OFF_Z:OFF_Z + GLA_KEY_WIDTH] = jnp.dot(
            tri2, jnp.concatenate([la_hi, la_lo], axis=0), preferred_element_type=F32)

    def chunk_body(c, carry):
        rows = pl.ds(pl.multiple_of(c * chunk, chunk), chunk)
        sidx = c if chunk_is_seq else 0

        vg = _gelu(proj[rows, OFF_VG:OFF_VG + GMLP_WIDTH])
        vn = _rms(vg, ggv_ref[...])
        if emit_v:
            v_ref[rows, :] = vn
        vnb = vn.astype(BF16)
        mixed = jnp.concatenate(
            [jnp.dot(wmix[g], vnb[:, g * GMLP_HEAD:(g + 1) * GMLP_HEAD], preferred_element_type=F32)
             for g in range(GMLP_GROUPS)], axis=-1) + bsf_ref[0:chunk, :]
        u = _gelu(proj[rows, OFF_U:OFF_U + GMLP_WIDTH])
        mix_scr[rows, 0:GMLP_WIDTH] = (u * mixed).astype(BF16)

        q = proj[rows, OFF_Q:OFF_Q + GLA_KEY_WIDTH] * (GLA_DK ** -0.5)
        k = proj[rows, OFF_K:OFF_K + GLA_KEY_WIDTH]
        b = proj[rows, OFF_Z:OFF_Z + GLA_KEY_WIDTH]
        b_mid = b[mid - 1:mid]
        b_end = b[chunk - 1:chunk]
        qt = (q * jnp.exp(b - b_mid)).astype(BF16)
        kt = (k * jnp.exp(b_mid - b)).astype(BF16)
        qs = (q * jnp.exp(b)).astype(BF16)
        kd = (k * jnp.exp(b_end - b)).astype(BF16)
        decay = jnp.exp(b_end)
        for hd in range(GLA_HEADS):
            ksl = slice(hd * GLA_DK, (hd + 1) * GLA_DK)
            vsl = slice(OFF_VL + hd * GLA_DV, OFF_VL + (hd + 1) * GLA_DV)
            rsl = slice(OFF_R + hd * GLA_DV, OFF_R + (hd + 1) * GLA_DV)
            att = lax.dot_general(qt[:, ksl], kt[:, ksl], NT_DIMS, preferred_element_type=F32)
            att = jnp.where(causal, att, 0.0).astype(BF16)
            vh = proj[rows, vsl].astype(BF16)
            st = st_ref[sidx, hd]
            o = jnp.dot(att, vh, preferred_element_type=F32) + lax.dot_general(
                qs[:, ksl], st.astype(BF16), NT_DIMS, preferred_element_type=F32)
            on = _rms(o, ggo_ref[hd:hd + 1, :])
            mix_scr[rows, GMLP_WIDTH + hd * GLA_DV:GMLP_WIDTH + (hd + 1) * GLA_DV] = (
                on * _silu(proj[rows, rsl])).astype(BF16)
            st_ref[sidx, hd] = st * decay[:, ksl] + lax.dot_general(
                vh, kd[:, ksl], TN_DIMS, preferred_element_type=F32)
        return carry

    lax.fori_loop(0, n_chunks, chunk_body, 0, unroll=4)

    mixo = jnp.dot(mix_scr[...], wout_ref[...], preferred_element_type=F32).reshape(ns, ls, D_MODEL)
    x1 = x_ref[...] + mod_row(2) * _rms(mixo, gpost1_ref[...])
    x1_ref[...] = x1
    h2 = (_rms(x1, gpre2_ref[...]) * (1.0 + mod_row(4)) + mod_row(3)).reshape(tm, D_MODEL)
    h2_hi = h2.astype(BF16)
    h2_ref[...] = h2_hi

    h2_lo = (h2 - h2_hi.astype(F32)).astype(BF16)
    parts = jnp.dot(jnp.concatenate([h2_hi, h2_lo], axis=-1), wr_ref[...], preferred_element_type=F32)
    logits = parts[:, 0:LANES] + parts[:, LANES:2 * LANES]
    gates_ref[...] = _route(logits.T[0:N_EXPERTS], br_ref[...])


def _mixer(x, mod, s0_t, w, *, seq_tile, chunk, emit_v):
    n_seq, seq_len, _ = x.shape
    chunk_is_seq = seq_len == chunk
    if chunk_is_seq:
        ns, ls = seq_tile // chunk, chunk
        grid = (1, n_seq // ns)
        x_map = lambda b, t: (t, 0, 0)
        seq_map3 = lambda b, t: (t, 0, 0)
        seq_map4 = lambda b, t: (t, 0, 0, 0)
    else:
        ns, ls = 1, seq_tile
        grid = (n_seq, seq_len // seq_tile)
        x_map = lambda b, t: (b, t, 0)
        seq_map3 = lambda b, t: (b, 0, 0)
        seq_map4 = lambda b, t: (b, 0, 0, 0)
    tm = ns * ls
    n_tok = n_seq * seq_len
    tok_block = lambda b, t: b * grid[1] + t
    tok_map = lambda b, t: (tok_block(b, t), 0)
    full2 = lambda b, t: (0, 0)
    full3 = lambda b, t: (0, 0, 0)

    in_specs = [
        pl.BlockSpec((ns, ls, D_MODEL), x_map),
        pl.BlockSpec((ns, MOD_ROWS, D_MODEL), seq_map3),
        pl.BlockSpec((ns, GLA_HEADS, GLA_DV, GLA_DK), seq_map4),
        pl.BlockSpec((1, D_MODEL), full2),
        pl.BlockSpec((D_MODEL, IN_WIDTH_PAD), full2),
        pl.BlockSpec((1, GLA_KEY_WIDTH), full2),
        pl.BlockSpec((1, GMLP_WIDTH), full2),
        pl.BlockSpec((GMLP_GROUPS, GMLP_CHUNK, GMLP_CHUNK), full3),
        pl.BlockSpec((GMLP_CHUNK, GMLP_WIDTH), full2),
        pl.BlockSpec((GLA_HEADS, GLA_DV), full2),
        pl.BlockSpec((D_MODEL, D_MODEL), full2),
        pl.BlockSpec((1, D_MODEL), full2),
        pl.BlockSpec((1, D_MODEL), full2),
        pl.BlockSpec((2 * D_MODEL, 2 * LANES), full2),
        pl.BlockSpec((N_EXPERTS, 1), full2),
    ]
    out_specs = [
        pl.BlockSpec((ns, ls, D_MODEL), x_map),
        pl.BlockSpec((tm, D_MODEL), tok_map),
        pl.BlockSpec((N_EXPERTS, tm), lambda b, t: (0, tok_block(b, t))),
        pl.BlockSpec((ns, GLA_HEADS, GLA_DV, GLA_DK), seq_map4),
    ]
    out_shape = [
        jax.ShapeDtypeStruct((n_seq, seq_len, D_MODEL), F32),
        jax.ShapeDtypeStruct((n_tok, D_MODEL), BF16),
        jax.ShapeDtypeStruct((N_EXPERTS, n_tok), F32),
        jax.ShapeDtypeStruct((n_seq, GLA_HEADS, GLA_DV, GLA_DK), F32),
    ]
    if emit_v:
        out_specs.append(pl.BlockSpec((tm, GMLP_WIDTH), tok_map))
        out_shape.append(jax.ShapeDtypeStruct((n_tok, GMLP_WIDTH), F32))

    kern = functools.partial(_mixer_kernel, chunk=chunk, n_chunks=tm // chunk,
                             chunk_is_seq=chunk_is_seq, emit_v=emit_v)
    return pl.pallas_call(
        kern,
        grid=grid,
        in_specs=in_specs,
        out_specs=out_specs,
        out_shape=out_shape,
        scratch_shapes=[pltpu.VMEM((tm, IN_WIDTH_PAD), F32), pltpu.VMEM((tm, D_MODEL), BF16)],
        compiler_params=pltpu.CompilerParams(
            dimension_semantics=("arbitrary", "arbitrary"), vmem_limit_bytes=VMEM_LIMIT),
        name="mixer_sample" if chunk_is_seq else "mixer_prompt",
    )(x, mod, s0_t, w["g_pre1"], w["w_in"], w["b_gk"], w["g_gmlp_v"], w["w_s"],
      w["b_s_full"], w["g_gla_o"], w["w_out"], w["g_post1"], w["g_pre2"], w["w_router"],
      w["b_router"])


MOE_BLOCK = 256
SEG_ALIGN = 16
WIN = 64
FFN_TILE = 1024
BLOCK_ROWS = MOE_BLOCK * TOP_K + SEG_ALIGN * N_EXPERTS + WIN
MAIN_ROWS = 2688
STEP_BLOCKS = 2
STEP_TOKENS = STEP_BLOCKS * MOE_BLOCK
COL_CHUNK = 256
COUNT_LANES = LANES


def _count_kernel(g_ref, o_ref):
    n = g_ref.shape[1]
    selb = jnp.where(g_ref[...] > 0.0, 1.0, 0.0).astype(BF16)
    tok = lax.broadcasted_iota(jnp.int32, (n, COUNT_LANES), 0)
    lane = lax.broadcasted_iota(jnp.int32, (n, COUNT_LANES), 1)
    ind = jnp.where(tok // MOE_BLOCK == lane, 1.0, 0.0).astype(BF16)
    o_ref[0] = jnp.dot(selb, ind, preferred_element_type=F32)


def _fill_permutation(p_scr, rk_scr, g_ref, cntp_ref, loff_ref, long_ref, blk, weighted):
    tb = g_ref.shape[1]
    g = g_ref[...]
    sel = g > 0.0
    r_i = lax.broadcasted_iota(jnp.int32, (tb, tb), 0)
    c_i = lax.broadcasted_iota(jnp.int32, (tb, tb), 1)
    earlier = jnp.where(r_i < c_i, 1.0, 0.0).astype(BF16)
    rank = jnp.dot(jnp.where(sel, 1.0, 0.0).astype(BF16), earlier, preferred_element_type=F32)
    rk = jnp.where(sel, rank, -1.0)
    p_scr[...] = jnp.zeros_like(p_scr)
    jwin = lax.broadcasted_iota(jnp.int32, (WIN, tb), 0).astype(F32)

    for e in range(N_EXPERTS):
        off = pl.multiple_of(loff_ref[blk * N_EXPERTS + e], SEG_ALIGN)
        val = g[e:e + 1] if weighted else 1.0
        p_scr[pl.ds(off, WIN), :] = jnp.where(rk[e:e + 1] == jwin, val, 0.0).astype(BF16)

    @pl.when(long_ref[blk] > 0)
    def _():
        rk_scr[...] = rk

        def expert_body(e, c):
            n = cntp_ref[blk * N_EXPERTS + e]
            off = loff_ref[blk * N_EXPERTS + e]
            rke = rk_scr[pl.ds(e, 1), :]
            val = g_ref[pl.ds(e, 1), :] if weighted else 1.0

            def win_body(wi, c2):
                rows = pl.ds(pl.multiple_of(off + wi * WIN, SEG_ALIGN), WIN)
                j = jwin + (wi * WIN).astype(F32)
                new = jnp.where(rke == j, val, 0.0)
                p_scr[rows, :] = jnp.where(j < n.astype(F32), new, p_scr[rows, :].astype(F32)).astype(BF16)
                return c2

            lax.fori_loop(1, (n + WIN - 1) // WIN, win_body, 0)
            return c

        lax.fori_loop(0, N_EXPERTS, expert_body, 0)


def _start_segments(cntp_ref, loff_ref, blk, make_copy):
    for e in range(N_EXPERTS):
        make_copy(e, pl.multiple_of(loff_ref[blk * N_EXPERTS + e], SEG_ALIGN),
                  pl.multiple_of(cntp_ref[blk * N_EXPERTS + e], SEG_ALIGN)).start()


def _dispatch_kernel(cntp_ref, loff_ref, goff_ref, rows_ref, long_ref, toff_ref, tlen_ref,
                     ga_ref, gb_ref, h2a_ref, h2b_ref, xs_hbm,
                     p_scr, rk_scr, g_scr, h2_scr, xs_scr, zero_scr, sems, tail_sem,
                     *, n_steps, n_steps_a):
    step = pl.program_id(0)
    parity = lax.rem(step, 2)
    from_a = step < n_steps_a
    g_scr[...] = jnp.where(from_a, ga_ref[...], gb_ref[...])
    h2_scr[...] = jnp.where(from_a, h2a_ref[...], h2b_ref[...])

    def seg_copy(blk, s):
        def make(e, off, n):
            dst = pl.multiple_of(goff_ref[blk * N_EXPERTS + e], SEG_ALIGN)
            return pltpu.make_async_copy(xs_scr.at[s, pl.ds(off, n)], xs_hbm.at[pl.ds(dst, n)], sems.at[s])
        return make

    def wait_block(blk, s):
        n = pl.multiple_of(rows_ref[blk], SEG_ALIGN)
        pltpu.make_async_copy(xs_scr.at[s, pl.ds(0, n)], xs_hbm.at[pl.ds(0, n)], sems.at[s]).wait()

    def for_tails(action):
        def body(e, c):
            n = tlen_ref[e]

            @pl.when(n > 0)
            def _():
                dst = pl.multiple_of(toff_ref[e], SEG_ALIGN)
                nn = pl.multiple_of(n, SEG_ALIGN)
                action(pltpu.make_async_copy(zero_scr.at[pl.ds(0, nn)], xs_hbm.at[pl.ds(dst, nn)], tail_sem))

            return c

        lax.fori_loop(0, N_EXPERTS, body, 0)

    @pl.when(step == 0)
    def _():
        zero_scr[...] = jnp.zeros_like(zero_scr)
        for_tails(lambda cp: cp.start())

    for j in range(STEP_BLOCKS):
        blk = step * STEP_BLOCKS + j
        slot = parity * STEP_BLOCKS + j
        tok = slice(j * MOE_BLOCK, (j + 1) * MOE_BLOCK)

        @pl.when(step >= 2)
        def _():
            wait_block(blk - 2 * STEP_BLOCKS, slot)

        _fill_permutation(p_scr, rk_scr, g_scr.at[:, tok], cntp_ref, loff_ref, long_ref, blk,
                          weighted=False)

        def sort_rows(rows):
            for c in range(D_MODEL // COL_CHUNK):
                cols = slice(c * COL_CHUNK, (c + 1) * COL_CHUNK)
                xs = jnp.dot(p_scr[rows, :], h2_scr[tok, cols], preferred_element_type=F32)
                xs_scr[slot, rows, cols] = xs.astype(BF16)

        sort_rows(slice(0, MAIN_ROWS))

        @pl.when(rows_ref[blk] > MAIN_ROWS)
        def _():
            sort_rows(slice(MAIN_ROWS, BLOCK_ROWS))

        _start_segments(cntp_ref, loff_ref, blk, seg_copy(blk, slot))

    @pl.when(step == n_steps - 1)
    def _():
        for_tails(lambda cp: cp.wait())
        for j in range(STEP_BLOCKS):
            blk = step * STEP_BLOCKS + j
            if n_steps >= 2:
                wait_block(blk - STEP_BLOCKS, (1 - parity) * STEP_BLOCKS + j)
            wait_block(blk, parity * STEP_BLOCKS + j)


def _ffn_kernel(te_ref, nu_ref, x_ref, wg_ref, wu_ref, wd_ref, y_ref, wgu_scr, wd_scr):
    i = pl.program_id(0)

    @pl.when(i < nu_ref[0])
    def _():
        @pl.when(jnp.logical_or(i == 0, te_ref[i] != te_ref[jnp.maximum(i - 1, 0)]))
        def _():
            wgu_scr[:, 0:EXPERT_FF] = wg_ref[0].astype(BF16)
            wgu_scr[:, EXPERT_FF:2 * EXPERT_FF] = wu_ref[0].astype(BF16)
            wd_scr[...] = wd_ref[0].astype(BF16)

        gu = jnp.dot(x_ref[...], wgu_scr[...], preferred_element_type=F32)
        hid = _silu(gu[:, 0:EXPERT_FF]) * gu[:, EXPERT_FF:2 * EXPERT_FF]
        y_ref[...] = jnp.dot(hid.astype(BF16), wd_scr[...], preferred_element_type=F32).astype(BF16)


def _combine_kernel(cntp_ref, loff_ref, goff_ref, rows_ref, long_ref,
                    ga_ref, gb_ref, h2a_ref, h2b_ref, x1a_ref, x1b_ref, moda_ref, modb_ref,
                    wsgu_ref, wsd_ref, gpost2_ref, ys_hbm, outa_ref, outb_ref,
                    p_scr, rk_scr, g_scr, ys_scr, routed_scr, sems, *, n_steps, n_steps_a):
    step = pl.program_id(0)
    parity = lax.rem(step, 2)
    from_a = step < n_steps_a
    g_scr[...] = jnp.where(from_a, ga_ref[...], gb_ref[...])

    def seg_copy(blk, s):
        def make(e, off, n):
            src = pl.multiple_of(goff_ref[blk * N_EXPERTS + e], SEG_ALIGN)
            return pltpu.make_async_copy(ys_hbm.at[pl.ds(src, n)], ys_scr.at[s, pl.ds(off, n)], sems.at[s])
        return make

    def start_step(st, par):
        for j in range(STEP_BLOCKS):
            _start_segments(cntp_ref, loff_ref, st * STEP_BLOCKS + j,
                            seg_copy(st * STEP_BLOCKS + j, par * STEP_BLOCKS + j))

    @pl.when(step == 0)
    def _():
        ys_scr[...] = jnp.zeros_like(ys_scr)
        start_step(0, 0)

    @pl.when(step + 1 < n_steps)
    def _():
        start_step(step + 1, 1 - parity)

    for j in range(STEP_BLOCKS):
        blk = step * STEP_BLOCKS + j
        slot = parity * STEP_BLOCKS + j
        tok = slice(j * MOE_BLOCK, (j + 1) * MOE_BLOCK)
        _fill_permutation(p_scr, rk_scr, g_scr.at[:, tok], cntp_ref, loff_ref, long_ref, blk,
                          weighted=True)
        n_rows = pl.multiple_of(rows_ref[blk], SEG_ALIGN)
        pltpu.make_async_copy(ys_hbm.at[pl.ds(0, n_rows)], ys_scr.at[slot, pl.ds(0, n_rows)],
                              sems.at[slot]).wait()

        def unsort_rows(rows):
            return lax.dot_general(p_scr[rows, :], ys_scr[slot, rows, :], TN_DIMS,
                                   preferred_element_type=F32)

        routed_scr[tok, :] = unsort_rows(slice(0, MAIN_ROWS))

        @pl.when(rows_ref[blk] > MAIN_ROWS)
        def _():
            routed_scr[tok, :] += unsort_rows(slice(MAIN_ROWS, BLOCK_ROWS))

    hx = jnp.where(from_a, h2a_ref[...], h2b_ref[...])
    sgu = jnp.dot(hx, wsgu_ref[...], preferred_element_type=F32)
    shid = _silu(sgu[:, 0:SHARED_FF]) * sgu[:, SHARED_FF:2 * SHARED_FF]
    f = routed_scr[...] + jnp.dot(shid.astype(BF16), wsd_ref[...], preferred_element_type=F32)
    routed_scr[...] = _rms(f, gpost2_ref[...])

    def finish(x1_ref, mod_ref, out_ref):
        gate2 = mod_ref[:, 5:6, :]
        out_ref[...] = x1_ref[...] + gate2 * routed_scr[...].reshape(x1_ref.shape)

    @pl.when(from_a)
    def _():
        finish(x1a_ref, moda_ref, outa_ref)

    @pl.when(jnp.logical_not(from_a))
    def _():
        finish(x1b_ref, modb_ref, outb_ref)


def _segment_plan(cnt, n_tiles_max, ffn_tile):
    cntp = jnp.maximum((cnt + SEG_ALIGN - 1) // SEG_ALIGN, 1) * SEG_ALIGN
    loff = jnp.cumsum(cntp, axis=1) - cntp
    block_rows = jnp.sum(cntp, axis=1)
    has_long = jnp.max(cntp, axis=1) > WIN
    tot = jnp.sum(cntp, axis=0)
    reg = (tot + ffn_tile - 1) // ffn_tile * ffn_tile
    base = jnp.cumsum(reg) - reg
    goff = base[None, :] + jnp.cumsum(cntp, axis=0) - cntp
    tile_end = jnp.cumsum(reg // ffn_tile)
    tile = jnp.arange(n_tiles_max, dtype=jnp.int32)
    tile_expert = jnp.minimum(jnp.sum(tile_end[None, :] <= tile[:, None], axis=1), N_EXPERTS - 1)
    i32 = lambda a: a.astype(jnp.int32)
    block_tables = (i32(cntp.reshape(-1)), i32(loff.reshape(-1)), i32(goff.reshape(-1)),
                    i32(block_rows), i32(has_long))
    return block_tables, i32(base + tot), i32(reg - tot), i32(tile_expert), i32(tile_end[-1:])


def _count_blocks(gates_t, params):
    n_tok = gates_t.shape[1]
    chunk = min(n_tok, COUNT_LANES * MOE_BLOCK // 16)
    per_chunk = chunk // MOE_BLOCK
    cnt = pl.pallas_call(
        _count_kernel,
        grid=(n_tok // chunk,),
        in_specs=[pl.BlockSpec((N_EXPERTS, chunk), lambda i: (0, i))],
        out_specs=pl.BlockSpec((1, N_EXPERTS, COUNT_LANES), lambda i: (i, 0, 0)),
        out_shape=jax.ShapeDtypeStruct((n_tok // chunk, N_EXPERTS, COUNT_LANES), F32),
        compiler_params=params,
        name="moe_count",
    )(gates_t)
    return jnp.swapaxes(cnt[:, :, :per_chunk], 1, 2).reshape(n_tok // MOE_BLOCK, N_EXPERTS).astype(jnp.int32)


def _stream_specs(x1, block_of):
    n_seq, seq_len, _ = x1.shape
    if seq_len < STEP_TOKENS:
        ns, ls = STEP_TOKENS // seq_len, seq_len
        x_map = lambda b, *_: (block_of(b), 0, 0)
        mod_map = x_map
    else:
        ns, ls = 1, STEP_TOKENS
        per_seq = seq_len // STEP_TOKENS
        x_map = lambda b, *_: (block_of(b) // per_seq, block_of(b) % per_seq, 0)
        mod_map = lambda b, *_: (block_of(b) // per_seq, 0, 0)
    return pl.BlockSpec((ns, ls, D_MODEL), x_map), pl.BlockSpec((ns, MOD_ROWS, D_MODEL), mod_map)


def _moe(stream_a, stream_b, w):
    (h2_a, g_a, x1_a, mod_a), (h2_b, g_b, x1_b, mod_b) = stream_a, stream_b
    nb_a, nb_b = h2_a.shape[0] // MOE_BLOCK, h2_b.shape[0] // MOE_BLOCK
    n_blocks = nb_a + nb_b
    n_tok = n_blocks * MOE_BLOCK
    ffn_tile = FFN_TILE
    rows_max = (n_tok * TOP_K + n_blocks * N_EXPERTS * SEG_ALIGN
                + N_EXPERTS * (ffn_tile - SEG_ALIGN))
    n_tiles_max = -(-rows_max // ffn_tile)
    rows_alloc = n_tiles_max * ffn_tile
    params = pltpu.CompilerParams(dimension_semantics=("arbitrary",), vmem_limit_bytes=VMEM_LIMIT)

    cnt = jnp.concatenate([_count_blocks(g_a, params), _count_blocks(g_b, params)], axis=0)
    block_tables, tail_off, tail_len, tile_expert, n_used = _segment_plan(cnt, n_tiles_max, ffn_tile)

    assert nb_a % STEP_BLOCKS == 0 and nb_b % STEP_BLOCKS == 0
    steps_a, n_steps = nb_a // STEP_BLOCKS, n_blocks // STEP_BLOCKS
    in_a = lambda s: jnp.minimum(s, steps_a - 1)
    in_b = lambda s: jnp.maximum(s - steps_a, 0)
    tok_specs = [pl.BlockSpec((N_EXPERTS, STEP_TOKENS), lambda s, *_: (0, in_a(s))),
                 pl.BlockSpec((N_EXPERTS, STEP_TOKENS), lambda s, *_: (0, in_b(s))),
                 pl.BlockSpec((STEP_TOKENS, D_MODEL), lambda s, *_: (in_a(s), 0)),
                 pl.BlockSpec((STEP_TOKENS, D_MODEL), lambda s, *_: (in_b(s), 0))]
    any_spec = pl.BlockSpec(memory_space=pl.ANY)
    perm_scratch = [pltpu.VMEM((BLOCK_ROWS, MOE_BLOCK), BF16), pltpu.VMEM((N_EXPERTS, MOE_BLOCK), F32),
                    pltpu.VMEM((N_EXPERTS, STEP_TOKENS), F32)]
    row_buffers = pltpu.VMEM((2 * STEP_BLOCKS, BLOCK_ROWS, D_MODEL), BF16)

    x_sorted = pl.pallas_call(
        functools.partial(_dispatch_kernel, n_steps=n_steps, n_steps_a=steps_a),
        grid_spec=pltpu.PrefetchScalarGridSpec(
            num_scalar_prefetch=7, grid=(n_steps,),
            in_specs=tok_specs, out_specs=any_spec,
            scratch_shapes=perm_scratch + [
                pltpu.VMEM((STEP_TOKENS, D_MODEL), BF16),
                row_buffers,
                pltpu.VMEM((ffn_tile, D_MODEL), BF16),
                pltpu.SemaphoreType.DMA((2 * STEP_BLOCKS,)), pltpu.SemaphoreType.DMA(())]),
        out_shape=jax.ShapeDtypeStruct((rows_alloc, D_MODEL), BF16),
        compiler_params=params,
        name="moe_dispatch",
    )(*block_tables, tail_off, tail_len, g_a, g_b, h2_a, h2_b)

    last_used = lambda i, nu: jnp.maximum(jnp.minimum(i, nu[0] - 1), 0)
    row_map = lambda i, te, nu: (last_used(i, nu), 0)
    w_map = lambda i, te, nu: (te[last_used(i, nu)], 0, 0)
    y_sorted = pl.pallas_call(
        _ffn_kernel,
        grid_spec=pltpu.PrefetchScalarGridSpec(
            num_scalar_prefetch=2, grid=(n_tiles_max,),
            in_specs=[pl.BlockSpec((ffn_tile, D_MODEL), row_map),
                      pl.BlockSpec((1, D_MODEL, EXPERT_FF), w_map),
                      pl.BlockSpec((1, D_MODEL, EXPERT_FF), w_map),
                      pl.BlockSpec((1, EXPERT_FF, D_MODEL), w_map)],
            out_specs=pl.BlockSpec((ffn_tile, D_MODEL), row_map),
            scratch_shapes=[pltpu.VMEM((D_MODEL, 2 * EXPERT_FF), BF16),
                            pltpu.VMEM((EXPERT_FF, D_MODEL), BF16)]),
        out_shape=jax.ShapeDtypeStruct((rows_alloc, D_MODEL), BF16),
        compiler_params=params,
        name="moe_ffn",
    )(tile_expert, n_used, x_sorted, w["w_e_gate"], w["w_e_up"], w["w_e_down"])

    full2 = lambda b, *_: (0, 0)
    x_spec_a, mod_spec_a = _stream_specs(x1_a, in_a)
    x_spec_b, mod_spec_b = _stream_specs(x1_b, in_b)
    return pl.pallas_call(
        functools.partial(_combine_kernel, n_steps=n_steps, n_steps_a=steps_a),
        grid_spec=pltpu.PrefetchScalarGridSpec(
            num_scalar_prefetch=5, grid=(n_steps,),
            in_specs=tok_specs + [
                x_spec_a, x_spec_b, mod_spec_a, mod_spec_b,
                pl.BlockSpec((D_MODEL, 2 * SHARED_FF), full2),
                pl.BlockSpec((SHARED_FF, D_MODEL), full2),
                pl.BlockSpec((1, D_MODEL), full2),
                any_spec],
            out_specs=[x_spec_a, x_spec_b],
            scratch_shapes=perm_scratch + [
                row_buffers,
                pltpu.VMEM((STEP_TOKENS, D_MODEL), F32),
                pltpu.SemaphoreType.DMA((2 * STEP_BLOCKS,))]),
        out_shape=[jax.ShapeDtypeStruct(x1_a.shape, F32), jax.ShapeDtypeStruct(x1_b.shape, F32)],
        compiler_params=params,
        name="moe_combine",
    )(*block_tables, g_a, g_b, h2_a, h2_b, x1_a, x1_b, mod_a, mod_b,
      w["w_sh_gu"], w["w_sh_down"], w["g_post2"], y_sorted)


def _prep_weights(g_pre1, g_post1, w_in, w_gk_up, b_gk, g_gmlp_v, w_s, b_s, g_gla_o, w_out, g_pre2,
                  g_post2, w_router, b_router, w_e_gate, w_e_up, w_e_down, w_sh_gate, w_sh_up,
                  w_sh_down):
    row = lambda v: v.reshape(1, -1)
    wr = jnp.pad(w_router, ((0, 0), (0, LANES - N_EXPERTS)))
    wr_top = lax.bitcast_convert_type(
        lax.bitcast_convert_type(wr, jnp.uint32) & jnp.uint32(0xFFFF0000), F32)
    wr_hi = wr_top.astype(BF16)
    wr_lo = (wr - wr_top).astype(BF16)
    wr_split = jnp.concatenate([jnp.concatenate([wr_hi, wr_lo], axis=1),
                                jnp.concatenate([wr_hi, jnp.zeros_like(wr_lo)], axis=1)], axis=0)
    return {
        "g_pre1": row(g_pre1), "g_post1": row(g_post1), "g_pre2": row(g_pre2), "g_post2": row(g_post2),
        "w_in": jnp.concatenate([w_in[:, :OFF_Z], _fold_gate(w_in[:, OFF_Z:], w_gk_up)],
                                axis=1).astype(BF16),
        "b_gk": row(b_gk),
        "g_gmlp_v": row(g_gmlp_v),
        "w_s": w_s,
        "b_s_full": jnp.repeat(b_s.T, GMLP_HEAD, axis=1),
        "g_gla_o": g_gla_o,
        "w_out": w_out.astype(BF16),
        "w_router": wr_split,
        "b_router": b_router.reshape(N_EXPERTS, 1),
        "w_e_gate": w_e_gate, "w_e_up": w_e_up, "w_e_down": w_e_down,
        "w_sh_gu": jnp.concatenate([w_sh_gate, w_sh_up], axis=-1).astype(BF16),
        "w_sh_down": w_sh_down.astype(BF16),
    }


PROMPT_TILE = 1024
SAMPLE_TILE = 256


def _mix(x, mod, s0, w, *, emit_v):
    n_seq, seq_len, _ = x.shape
    s0_t = jnp.swapaxes(s0, -1, -2)
    if seq_len >= GMLP_CHUNK:
        outs = _mixer(x, mod, s0_t, w, seq_tile=PROMPT_TILE, chunk=GMLP_CHUNK, emit_v=emit_v)
    else:
        outs = _mixer(x, mod, s0_t, w, seq_tile=SAMPLE_TILE, chunk=seq_len, emit_v=emit_v)
    x1, h2, gates_t, st = outs[:4]
    v = outs[4].reshape(n_seq, seq_len, GMLP_GROUPS, GMLP_HEAD) if emit_v else None
    return (h2, gates_t, x1, mod), jnp.swapaxes(st, -1, -2), v


def kernel(x_prompt, x_sample, state_gla, c_prompt, c_sample, w_ada, b_ada, g_pre1, g_post1, w_in, w_gk_up, b_gk, g_gmlp_v, w_s, b_s, g_gla_o, w_out, g_pre2, g_post2, w_router, b_router, w_e_gate, w_e_up, w_e_down, w_sh_gate, w_sh_up, w_sh_down):
    depth = w_ada.shape[0]
    n_p, n_s = x_prompt.shape[0], x_sample.shape[0]
    c_all = jnp.concatenate([c_prompt, c_sample], axis=0)
    x_p, x_s = x_prompt, x_sample
    sp_list, ss_list, vs_list = [], [], []
    for l in range(depth):
        w = _prep_weights(g_pre1[l], g_post1[l], w_in[l], w_gk_up[l], b_gk[l], g_gmlp_v[l], w_s[l],
                          b_s[l], g_gla_o[l], w_out[l], g_pre2[l], g_post2[l], w_router[l],
                          b_router[l], w_e_gate[l], w_e_up[l], w_e_down[l], w_sh_gate[l], w_sh_up[l],
                          w_sh_down[l])
        mod = _ada(c_all, w_ada[l], b_ada[l])
        mod = jnp.pad(jnp.swapaxes(mod, 0, 1), ((0, 0), (0, MOD_ROWS - 6), (0, 0)))
        s0_p = jnp.zeros((n_p, GLA_HEADS, GLA_DK, GLA_DV), F32)
        moe_p, s_p, _ = _mix(x_p, mod[:n_p], s0_p, w, emit_v=False)
        moe_s, s_s, v_s = _mix(x_s, mod[n_p:], state_gla[l], w, emit_v=True)
        x_p, x_s = _moe(moe_p, moe_s, w)
        sp_list.append(s_p)
        ss_list.append(s_s)
        vs_list.append(v_s)
    return (x_p, x_s, jnp.stack(sp_list), jnp.stack(ss_list), jnp.stack(vs_list))
```

```python
import functools

import jax
import jax.numpy as jnp
from jax import lax
from jax.experimental import pallas as pl
from jax.experimental.pallas import tpu as pltpu

D_MODEL = 1024
GMLP_WIDTH = 512
GMLP_GROUPS = 4
GMLP_HEAD = 128
GMLP_CHUNK = 128
CAUSAL_BLOCK = 64
GLA_WIDTH = 512
GLA_HEADS = 4
GLA_DV = 128
GLA_DK = 64
GLA_KEY_WIDTH = 256
GLA_GATE_RANK = 16
GLA_GATE_NORMALIZER = 16.0
N_EXPERTS = 64
TOP_K = 8
N_EXPERT_GROUPS = 8
GROUP_SIZE = N_EXPERTS // N_EXPERT_GROUPS
TOPK_GROUPS = 4
EXPERT_FF = 256
SHARED_FF = 256
ROUTED_SCALE = 2.5
EPS = 1e-6

LANES = 128
GK_PAD = LANES
IN_WIDTH_PAD = 2 * GMLP_WIDTH + 2 * GLA_KEY_WIDTH + 2 * GLA_WIDTH + GLA_KEY_WIDTH
OFF_U = 0
OFF_VG = GMLP_WIDTH
OFF_Q = 2 * GMLP_WIDTH
OFF_K = OFF_Q + GLA_KEY_WIDTH
OFF_VL = OFF_K + GLA_KEY_WIDTH
OFF_R = OFF_VL + GLA_WIDTH
OFF_Z = OFF_R + GLA_WIDTH
MOD_ROWS = 8
VMEM_LIMIT = 56 * 1024 * 1024

F32 = jnp.float32
BF16 = jnp.bfloat16
NT_DIMS = (((1,), (1,)), ((), ()))
TN_DIMS = (((0,), (0,)), ((), ()))


def _rms(x, g):
    return x * lax.rsqrt(jnp.mean(x * x, axis=-1, keepdims=True) + EPS) * g


def _gelu(x):
    return 0.5 * x * (1.0 + jnp.tanh(0.7978845608028654 * (x + 0.044715 * (x * x * x))))


def _sigmoid(x):
    return 1.0 / (1.0 + jnp.exp(-x))


def _silu(x):
    return x * _sigmoid(x)


def _ada_kernel(c_ref, w_ref, b_ref, o_ref):
    a = _silu(c_ref[...])
    o_ref[0] = jnp.dot(a, w_ref[...], precision=lax.Precision.HIGHEST,
                       preferred_element_type=F32) + b_ref[0]


def _ada(c_all, w_ada, b_ada):
    n = c_all.shape[0]
    return pl.pallas_call(
        _ada_kernel,
        grid=(6,),
        in_specs=[pl.BlockSpec((n, D_MODEL), lambda j: (0, 0)),
                  pl.BlockSpec((D_MODEL, D_MODEL), lambda j: (0, j)),
                  pl.BlockSpec((1, 1, D_MODEL), lambda j: (j, 0, 0))],
        out_specs=pl.BlockSpec((1, n, D_MODEL), lambda j: (j, 0, 0)),
        out_shape=jax.ShapeDtypeStruct((6, n, D_MODEL), F32),
        compiler_params=pltpu.CompilerParams(vmem_limit_bytes=VMEM_LIMIT),
        name="ada",
    )(c_all, w_ada, b_ada.reshape(6, 1, D_MODEL))


def _fold_gate_kernel(a_ref, b_ref, o_ref):
    o_ref[...] = jnp.dot(a_ref[...], b_ref[...], precision=lax.Precision.HIGHEST,
                         preferred_element_type=F32)


def _fold_gate(w_in_gate, w_gk_up):
    a = jnp.pad(w_in_gate, ((0, 0), (0, GK_PAD - GLA_GATE_RANK)))
    b = jnp.pad(w_gk_up, ((0, GK_PAD - GLA_GATE_RANK), (0, 0)))
    return pl.pallas_call(
        _fold_gate_kernel,
        out_shape=jax.ShapeDtypeStruct((D_MODEL, GLA_KEY_WIDTH), F32),
        name="fold_gate",
    )(a, b)


def _route(logits_t, bias_t):
    t = logits_t.shape[1]
    scores = _sigmoid(logits_t)
    sel = scores + bias_t
    sub = lax.broadcasted_iota(jnp.int32, (GROUP_SIZE, t), 0)
    gscore = []
    for g in range(N_EXPERT_GROUPS):
        blk = sel[g * GROUP_SIZE:(g + 1) * GROUP_SIZE]
        m1 = jnp.max(blk, axis=0, keepdims=True)
        first = jnp.min(jnp.where(blk == m1, sub, GROUP_SIZE), axis=0, keepdims=True)
        m2 = jnp.max(jnp.where(sub == first, -jnp.inf, blk), axis=0, keepdims=True)
        gscore.append(m1 + m2)
    neg = jnp.full((GROUP_SIZE, t), -jnp.inf, F32)
    masked = []
    for g in range(N_EXPERT_GROUPS):
        rank = jnp.zeros((1, t), jnp.int32)
        for o in range(N_EXPERT_GROUPS):
            if o == g:
                continue
            ahead = (gscore[o] >= gscore[g]) if o < g else (gscore[o] > gscore[g])
            rank = rank + jnp.where(ahead, 1, 0)
        keep = jnp.broadcast_to(rank < TOPK_GROUPS, (GROUP_SIZE, t))
        masked.append(jnp.where(keep, sel[g * GROUP_SIZE:(g + 1) * GROUP_SIZE], neg))
    selm = jnp.concatenate(masked, axis=0)
    eidx = lax.broadcasted_iota(jnp.int32, (N_EXPERTS, t), 0)
    picked = jnp.zeros((N_EXPERTS, t), F32)
    for _ in range(TOP_K):
        best = jnp.max(selm, axis=0, keepdims=True)
        first = jnp.min(jnp.where(selm == best, eidx, N_EXPERTS), axis=0, keepdims=True)
        hit = eidx == first
        picked = jnp.where(hit, 1.0, picked)
        selm = jnp.where(hit, -jnp.inf, selm)
    chosen = jnp.where(picked > 0.0, scores, 0.0)
    denom = jnp.sum(chosen, axis=0, keepdims=True)
    return chosen * (ROUTED_SCALE / denom)


def _mixer_kernel(x_ref, mod_ref, s0_ref, gpre1_ref, win_ref, bgk_ref, ggv_ref,
                  ws_ref, bsf_ref, ggo_ref, wout_ref, gpost1_ref, gpre2_ref, wr_ref, br_ref,
                  x1_ref, h2_ref, gates_ref, st_ref, *rest,
                  chunk, n_chunks, chunk_is_seq, emit_v):
    if emit_v:
        v_ref, *rest = rest
    proj, mix_scr = rest
    ns, ls, _ = x_ref.shape
    tm = ns * ls

    mod_row = lambda i: mod_ref[:, i:i + 1, :]

    h = _rms(x_ref[...], gpre1_ref[...]) * (1.0 + mod_row(1)) + mod_row(0)
    proj[...] = jnp.dot(h.reshape(tm, D_MODEL).astype(BF16), win_ref[...],
                        preferred_element_type=F32)

    if chunk_is_seq:
        st_ref[...] = s0_ref[...]
    else:
        @pl.when(pl.program_id(1) == 0)
        def _():
            st_ref[...] = s0_ref[...]

    row = lax.broadcasted_iota(jnp.int32, (chunk, chunk), 0)
    col = lax.broadcasted_iota(jnp.int32, (chunk, chunk), 1)
    tri = jnp.where(row >= col, 1.0, 0.0).astype(BF16)
    tri2 = jnp.concatenate([tri, tri], axis=1)
    causal = row >= col
    block_causal = (row // CAUSAL_BLOCK) >= (col // CAUSAL_BLOCK)
    wmix = [jnp.where(block_causal, ws_ref[g, 0:chunk, 0:chunk], 0.0).astype(BF16)
            for g in range(GMLP_GROUPS)]
    mid = chunk // 2

    for c in range(n_chunks):
        rows = slice(c * chunk, (c + 1) * chunk)
        z = proj[rows, OFF_Z:OFF_Z + GLA_KEY_WIDTH] + bgk_ref[...]
        la = (jnp.minimum(z, 0.0) - jnp.log(1.0 + jnp.exp(-jnp.abs(z)))) * (1.0 / GLA_GATE_NORMALIZER)
        la_hi = la.astype(BF16)
        la_lo = (la - la_hi.astype(F32)).astype(BF16)
        proj[rows, OFF_Z:OFF_Z + GLA_KEY_WIDTH] = jnp.dot(
            tri2, jnp.concatenate([la_hi, la_lo], axis=0), preferred_element_type=F32)

    def chunk_body(c, carry):
        rows = pl.ds(pl.multiple_of(c * chunk, chunk), chunk)
        sidx = c if chunk_is_seq else 0

        vg = _gelu(proj[rows, OFF_VG:OFF_VG + GMLP_WIDTH])
        vn = _rms(vg, ggv_ref[...])
        if emit_v:
            v_ref[rows, :] = vn
        vnb = vn.astype(BF16)
        mixed = jnp.concatenate(
            [jnp.dot(wmix[g], vnb[:, g * GMLP_HEAD:(g + 1) * GMLP_HEAD], preferred_element_type=F32)
             for g in range(GMLP_GROUPS)], axis=-1) + bsf_ref[0:chunk, :]
        u = _gelu(proj[rows, OFF_U:OFF_U + GMLP_WIDTH])
        mix_scr[rows, 0:GMLP_WIDTH] = (u * mixed).astype(BF16)

        q = proj[rows, OFF_Q:OFF_Q + GLA_KEY_WIDTH] * (GLA_DK ** -0.5)
        k = proj[rows, OFF_K:OFF_K + GLA_KEY_WIDTH]
        b = proj[rows, OFF_Z:OFF_Z + GLA_KEY_WIDTH]
        b_mid = b[mid - 1:mid]
        b_end = b[chunk - 1:chunk]
        qt = (q * jnp.exp(b - b_mid)).astype(BF16)
        kt = (k * jnp.exp(b_mid - b)).astype(BF16)
        qs = (q * jnp.exp(b)).astype(BF16)
        kd = (k * jnp.exp(b_end - b)).astype(BF16)
        decay = jnp.exp(b_end)
        for hd in range(GLA_HEADS):
            ksl = slice(hd * GLA_DK, (hd + 1) * GLA_DK)
            vsl = slice(OFF_VL + hd * GLA_DV, OFF_VL + (hd + 1) * GLA_DV)
            rsl = slice(OFF_R + hd * GLA_DV, OFF_R + (hd + 1) * GLA_DV)
            att = lax.dot_general(qt[:, ksl], kt[:, ksl], NT_DIMS, preferred_element_type=F32)
            att = jnp.where(causal, att, 0.0).astype(BF16)
            vh = proj[rows, vsl].astype(BF16)
            st = st_ref[sidx, hd]
            o = jnp.dot(att, vh, preferred_element_type=F32) + lax.dot_general(
                qs[:, ksl], st.astype(BF16), NT_DIMS, preferred_element_type=F32)
            on = _rms(o, ggo_ref[hd:hd + 1, :])
            mix_scr[rows, GMLP_WIDTH + hd * GLA_DV:GMLP_WIDTH + (hd + 1) * GLA_DV] = (
                on * _silu(proj[rows, rsl])).astype(BF16)
            st_ref[sidx, hd] = st * decay[:, ksl] + lax.dot_general(
                vh, kd[:, ksl], TN_DIMS, preferred_element_type=F32)
        return carry

    lax.fori_loop(0, n_chunks, chunk_body, 0, unroll=4)

    mixo = jnp.dot(mix_scr[...], wout_ref[...], preferred_element_type=F32).reshape(ns, ls, D_MODEL)
    x1 = x_ref[...] + mod_row(2) * _rms(mixo, gpost1_ref[...])
    x1_ref[...] = x1
    h2 = (_rms(x1, gpre2_ref[...]) * (1.0 + mod_row(4)) + mod_row(3)).reshape(tm, D_MODEL)
    h2_hi = h2.astype(BF16)
    h2_ref[...] = h2_hi

    h2_lo = (h2 - h2_hi.astype(F32)).astype(BF16)
    parts = jnp.dot(jnp.concatenate([h2_hi, h2_lo], axis=-1), wr_ref[...], preferred_element_type=F32)
    logits = parts[:, 0:LANES] + parts[:, LANES:2 * LANES]
    gates_ref[...] = _route(logits.T[0:N_EXPERTS], br_ref[...])


def _mixer(x, mod, s0_t, w, *, seq_tile, chunk, emit_v):
    n_seq, seq_len, _ = x.shape
    chunk_is_seq = seq_len == chunk
    if chunk_is_seq:
        ns, ls = seq_tile // chunk, chunk
        grid = (1, n_seq // ns)
        x_map = lambda b, t: (t, 0, 0)
        seq_map3 = lambda b, t: (t, 0, 0)
        seq_map4 = lambda b, t: (t, 0, 0, 0)
    else:
        ns, ls = 1, seq_tile
        grid = (n_seq, seq_len // seq_tile)
        x_map = lambda b, t: (b, t, 0)
        seq_map3 = lambda b, t: (b, 0, 0)
        seq_map4 = lambda b, t: (b, 0, 0, 0)
    tm = ns * ls
    n_tok = n_seq * seq_len
    tok_block = lambda b, t: b * grid[1] + t
    tok_map = lambda b, t: (tok_block(b, t), 0)
    full2 = lambda b, t: (0, 0)
    full3 = lambda b, t: (0, 0, 0)

    in_specs = [
        pl.BlockSpec((ns, ls, D_MODEL), x_map),
        pl.BlockSpec((ns, MOD_ROWS, D_MODEL), seq_map3),
        pl.BlockSpec((ns, GLA_HEADS, GLA_DV, GLA_DK), seq_map4),
        pl.BlockSpec((1, D_MODEL), full2),
        pl.BlockSpec((D_MODEL, IN_WIDTH_PAD), full2),
        pl.BlockSpec((1, GLA_KEY_WIDTH), full2),
        pl.BlockSpec((1, GMLP_WIDTH), full2),
        pl.BlockSpec((GMLP_GROUPS, GMLP_CHUNK, GMLP_CHUNK), full3),
        pl.BlockSpec((GMLP_CHUNK, GMLP_WIDTH), full2),
        pl.BlockSpec((GLA_HEADS, GLA_DV), full2),
        pl.BlockSpec((D_MODEL, D_MODEL), full2),
        pl.BlockSpec((1, D_MODEL), full2),
        pl.BlockSpec((1, D_MODEL), full2),
        pl.BlockSpec((2 * D_MODEL, 2 * LANES), full2),
        pl.BlockSpec((N_EXPERTS, 1), full2),
    ]
    out_specs = [
        pl.BlockSpec((ns, ls, D_MODEL), x_map),
        pl.BlockSpec((tm, D_MODEL), tok_map),
        pl.BlockSpec((N_EXPERTS, tm), lambda b, t: (0, tok_block(b, t))),
        pl.BlockSpec((ns, GLA_HEADS, GLA_DV, GLA_DK), seq_map4),
    ]
    out_shape = [
        jax.ShapeDtypeStruct((n_seq, seq_len, D_MODEL), F32),
        jax.ShapeDtypeStruct((n_tok, D_MODEL), BF16),
        jax.ShapeDtypeStruct((N_EXPERTS, n_tok), F32),
        jax.ShapeDtypeStruct((n_seq, GLA_HEADS, GLA_DV, GLA_DK), F32),
    ]
    if emit_v:
        out_specs.append(pl.BlockSpec((tm, GMLP_WIDTH), tok_map))
        out_shape.append(jax.ShapeDtypeStruct((n_tok, GMLP_WIDTH), F32))

    kern = functools.partial(_mixer_kernel, chunk=chunk, n_chunks=tm // chunk,
                             chunk_is_seq=chunk_is_seq, emit_v=emit_v)
    return pl.pallas_call(
        kern,
        grid=grid,
        in_specs=in_specs,
        out_specs=out_specs,
        out_shape=out_shape,
        scratch_shapes=[pltpu.VMEM((tm, IN_WIDTH_PAD), F32), pltpu.VMEM((tm, D_MODEL), BF16)],
        compiler_params=pltpu.CompilerParams(
            dimension_semantics=("arbitrary", "arbitrary"), vmem_limit_bytes=VMEM_LIMIT),
        name="mixer_sample" if chunk_is_seq else "mixer_prompt",
    )(x, mod, s0_t, w["g_pre1"], w["w_in"], w["b_gk"], w["g_gmlp_v"], w["w_s"],
      w["b_s_full"], w["g_gla_o"], w["w_out"], w["g_post1"], w["g_pre2"], w["w_router"],
      w["b_router"])


MOE_BLOCK = 256
SEG_ALIGN = 16
WIN = 64
FFN_TILE = 1024
BLOCK_ROWS = MOE_BLOCK * TOP_K + SEG_ALIGN * N_EXPERTS + WIN
MAIN_ROWS = 2688
STEP_BLOCKS = 2
STEP_TOKENS = STEP_BLOCKS * MOE_BLOCK
COL_CHUNK = 256
COUNT_LANES = LANES


def _count_kernel(g_ref, o_ref):
    n = g_ref.shape[1]
    selb = jnp.where(g_ref[...] > 0.0, 1.0, 0.0).astype(BF16)
    tok = lax.broadcasted_iota(jnp.int32, (n, COUNT_LANES), 0)
    lane = lax.broadcasted_iota(jnp.int32, (n, COUNT_LANES), 1)
    ind = jnp.where(tok // MOE_BLOCK == lane, 1.0, 0.0).astype(BF16)
    o_ref[0] = jnp.dot(selb, ind, preferred_element_type=F32)


def _fill_permutation(p_scr, rk_scr, g_ref, cntp_ref, loff_ref, long_ref, blk, weighted):
    tb = g_ref.shape[1]
    g = g_ref[...]
    sel = g > 0.0
    r_i = lax.broadcasted_iota(jnp.int32, (tb, tb), 0)
    c_i = lax.broadcasted_iota(jnp.int32, (tb, tb), 1)
    earlier = jnp.where(r_i < c_i, 1.0, 0.0).astype(BF16)
    rank = jnp.dot(jnp.where(sel, 1.0, 0.0).astype(BF16), earlier, preferred_element_type=F32)
    rk = jnp.where(sel, rank, -1.0)
    p_scr[...] = jnp.zeros_like(p_scr)
    jwin = lax.broadcasted_iota(jnp.int32, (WIN, tb), 0).astype(F32)

    for e in range(N_EXPERTS):
        off = pl.multiple_of(loff_ref[blk * N_EXPERTS + e], SEG_ALIGN)
        val = g[e:e + 1] if weighted else 1.0
        p_scr[pl.ds(off, WIN), :] = jnp.where(rk[e:e + 1] == jwin, val, 0.0).astype(BF16)

    @pl.when(long_ref[blk] > 0)
    def _():
        rk_scr[...] = rk

        def expert_body(e, c):
            n = cntp_ref[blk * N_EXPERTS + e]
            off = loff_ref[blk * N_EXPERTS + e]
            rke = rk_scr[pl.ds(e, 1), :]
            val = g_ref[pl.ds(e, 1), :] if weighted else 1.0

            def win_body(wi, c2):
                rows = pl.ds(pl.multiple_of(off + wi * WIN, SEG_ALIGN), WIN)
                j = jwin + (wi * WIN).astype(F32)
                new = jnp.where(rke == j, val, 0.0)
                p_scr[rows, :] = jnp.where(j < n.astype(F32), new, p_scr[rows, :].astype(F32)).astype(BF16)
                return c2

            lax.fori_loop(1, (n + WIN - 1) // WIN, win_body, 0)
            return c

        lax.fori_loop(0, N_EXPERTS, expert_body, 0)


def _start_segments(cntp_ref, loff_ref, blk, make_copy):
    for e in range(N_EXPERTS):
        make_copy(e, pl.multiple_of(loff_ref[blk * N_EXPERTS + e], SEG_ALIGN),
                  pl.multiple_of(cntp_ref[blk * N_EXPERTS + e], SEG_ALIGN)).start()


def _dispatch_kernel(cntp_ref, loff_ref, goff_ref, rows_ref, long_ref, toff_ref, tlen_ref,
                     ga_ref, gb_ref, h2a_ref, h2b_ref, xs_hbm,
                     p_scr, rk_scr, g_scr, h2_scr, xs_scr, zero_scr, sems, tail_sem,
                     *, n_steps, n_steps_a):
    step = pl.program_id(0)
    parity = lax.rem(step, 2)
    from_a = step < n_steps_a
    g_scr[...] = jnp.where(from_a, ga_ref[...], gb_ref[...])
    h2_scr[...] = jnp.where(from_a, h2a_ref[...], h2b_ref[...])

    def seg_copy(blk, s):
        def make(e, off, n):
            dst = pl.multiple_of(goff_ref[blk * N_EXPERTS + e], SEG_ALIGN)
            return pltpu.make_async_copy(xs_scr.at[s, pl.ds(off, n)], xs_hbm.at[pl.ds(dst, n)], sems.at[s])
        return make

    def wait_block(blk, s):
        n = pl.multiple_of(rows_ref[blk], SEG_ALIGN)
        pltpu.make_async_copy(xs_scr.at[s, pl.ds(0, n)], xs_hbm.at[pl.ds(0, n)], sems.at[s]).wait()

    def for_tails(action):
        def body(e, c):
            n = tlen_ref[e]

            @pl.when(n > 0)
            def _():
                dst = pl.multiple_of(toff_ref[e], SEG_ALIGN)
                nn = pl.multiple_of(n, SEG_ALIGN)
                action(pltpu.make_async_copy(zero_scr.at[pl.ds(0, nn)], xs_hbm.at[pl.ds(dst, nn)], tail_sem))

            return c

        lax.fori_loop(0, N_EXPERTS, body, 0)

    @pl.when(step == 0)
    def _():
        zero_scr[...] = jnp.zeros_like(zero_scr)
        for_tails(lambda cp: cp.start())

    for j in range(STEP_BLOCKS):
        blk = step * STEP_BLOCKS + j
        slot = parity * STEP_BLOCKS + j
        tok = slice(j * MOE_BLOCK, (j + 1) * MOE_BLOCK)

        @pl.when(step >= 2)
        def _():
            wait_block(blk - 2 * STEP_BLOCKS, slot)

        _fill_permutation(p_scr, rk_scr, g_scr.at[:, tok], cntp_ref, loff_ref, long_ref, blk,
                          weighted=False)

        def sort_rows(rows):
            for c in range(D_MODEL // COL_CHUNK):
                cols = slice(c * COL_CHUNK, (c + 1) * COL_CHUNK)
                xs = jnp.dot(p_scr[rows, :], h2_scr[tok, cols], preferred_element_type=F32)
                xs_scr[slot, rows, cols] = xs.astype(BF16)

        sort_rows(slice(0, MAIN_ROWS))

        @pl.when(rows_ref[blk] > MAIN_ROWS)
        def _():
            sort_rows(slice(MAIN_ROWS, BLOCK_ROWS))

        _start_segments(cntp_ref, loff_ref, blk, seg_copy(blk, slot))

    @pl.when(step == n_steps - 1)
    def _():
        for_tails(lambda cp: cp.wait())
        for j in range(STEP_BLOCKS):
            blk = step * STEP_BLOCKS + j
            if n_steps >= 2:
                wait_block(blk - STEP_BLOCKS, (1 - parity) * STEP_BLOCKS + j)
            wait_block(blk, parity * STEP_BLOCKS + j)


def _ffn_kernel(te_ref, nu_ref, x_ref, wga_ref, wua_ref, wda_ref, wgb_ref, wub_ref, wdb_ref, y_ref,
                wgu_scr, wd_scr, held_ref):
    i = pl.program_id(0)
    tile = x_ref.shape[0] // 2
    first, second = 2 * i, 2 * i + 1
    e_first, e_second = te_ref[first], te_ref[second]
    one_expert = jnp.logical_and(second < nu_ref[0], e_second == e_first)
    two_experts = jnp.logical_and(second < nu_ref[0], e_second != e_first)

    @pl.when(i == 0)
    def _():
        held_ref[0] = -1
        held_ref[1] = -1

    def hold(k, expert, wg_ref, wu_ref, wd_ref):
        @pl.when(held_ref[k] != expert)
        def _():
            wgu_scr[k, :, 0:EXPERT_FF] = wg_ref[0].astype(BF16)
            wgu_scr[k, :, EXPERT_FF:2 * EXPERT_FF] = wu_ref[0].astype(BF16)
            wd_scr[k] = wd_ref[0].astype(BF16)
            held_ref[k] = expert

    def ffn(rows, k):
        gu = jnp.dot(x_ref[rows, :], wgu_scr[k], preferred_element_type=F32)
        hid = _silu(gu[:, 0:EXPERT_FF]) * gu[:, EXPERT_FF:2 * EXPERT_FF]
        y_ref[rows, :] = jnp.dot(hid.astype(BF16), wd_scr[k], preferred_element_type=F32).astype(BF16)

    @pl.when(first < nu_ref[0])
    def _():
        hold(0, e_first, wga_ref, wua_ref, wda_ref)

        @pl.when(one_expert)
        def _():
            ffn(slice(0, 2 * tile), 0)

        @pl.when(jnp.logical_not(one_expert))
        def _():
            ffn(slice(0, tile), 0)

        @pl.when(two_experts)
        def _():
            hold(1, e_second, wgb_ref, wub_ref, wdb_ref)
            ffn(slice(tile, 2 * tile), 1)


def _combine_kernel(cntp_ref, loff_ref, goff_ref, rows_ref, long_ref,
                    ga_ref, gb_ref, h2a_ref, h2b_ref, x1a_ref, x1b_ref, moda_ref, modb_ref,
                    wsgu_ref, wsd_ref, gpost2_ref, ys_hbm, outa_ref, outb_ref,
                    p_scr, rk_scr, g_scr, ys_scr, routed_scr, sems, *, n_steps, n_steps_a):
    step = pl.program_id(0)
    parity = lax.rem(step, 2)
    from_a = step < n_steps_a
    g_scr[...] = jnp.where(from_a, ga_ref[...], gb_ref[...])

    def seg_copy(blk, s):
        def make(e, off, n):
            src = pl.multiple_of(goff_ref[blk * N_EXPERTS + e], SEG_ALIGN)
            return pltpu.make_async_copy(ys_hbm.at[pl.ds(src, n)], ys_scr.at[s, pl.ds(off, n)], sems.at[s])
        return make

    def start_step(st, par):
        for j in range(STEP_BLOCKS):
            _start_segments(cntp_ref, loff_ref, st * STEP_BLOCKS + j,
                            seg_copy(st * STEP_BLOCKS + j, par * STEP_BLOCKS + j))

    @pl.when(step == 0)
    def _():
        ys_scr[...] = jnp.zeros_like(ys_scr)
        start_step(0, 0)

    @pl.when(step + 1 < n_steps)
    def _():
        start_step(step + 1, 1 - parity)

    for j in range(STEP_BLOCKS):
        blk = step * STEP_BLOCKS + j
        slot = parity * STEP_BLOCKS + j
        tok = slice(j * MOE_BLOCK, (j + 1) * MOE_BLOCK)
        _fill_permutation(p_scr, rk_scr, g_scr.at[:, tok], cntp_ref, loff_ref, long_ref, blk,
                          weighted=True)
        n_rows = pl.multiple_of(rows_ref[blk], SEG_ALIGN)
        pltpu.make_async_copy(ys_hbm.at[pl.ds(0, n_rows)], ys_scr.at[slot, pl.ds(0, n_rows)],
                              sems.at[slot]).wait()

        def unsort_rows(rows):
            return lax.dot_general(p_scr[rows, :], ys_scr[slot, rows, :], TN_DIMS,
                                   preferred_element_type=F32)

        routed_scr[tok, :] = unsort_rows(slice(0, MAIN_ROWS))

        @pl.when(rows_ref[blk] > MAIN_ROWS)
        def _():
            routed_scr[tok, :] += unsort_rows(slice(MAIN_ROWS, BLOCK_ROWS))

    hx = jnp.where(from_a, h2a_ref[...], h2b_ref[...])
    sgu = jnp.dot(hx, wsgu_ref[...], preferred_element_type=F32)
    shid = _silu(sgu[:, 0:SHARED_FF]) * sgu[:, SHARED_FF:2 * SHARED_FF]
    f = routed_scr[...] + jnp.dot(shid.astype(BF16), wsd_ref[...], preferred_element_type=F32)
    routed_scr[...] = _rms(f, gpost2_ref[...])

    def finish(x1_ref, mod_ref, out_ref):
        gate2 = mod_ref[:, 5:6, :]
        out_ref[...] = x1_ref[...] + gate2 * routed_scr[...].reshape(x1_ref.shape)

    @pl.when(from_a)
    def _():
        finish(x1a_ref, moda_ref, outa_ref)

    @pl.when(jnp.logical_not(from_a))
    def _():
        finish(x1b_ref, modb_ref, outb_ref)


def _segment_plan(cnt, n_tiles_max, ffn_tile):
    cntp = jnp.maximum((cnt + SEG_ALIGN - 1) // SEG_ALIGN, 1) * SEG_ALIGN
    loff = jnp.cumsum(cntp, axis=1) - cntp
    block_rows = jnp.sum(cntp, axis=1)
    has_long = jnp.max(cntp, axis=1) > WIN
    tot = jnp.sum(cntp, axis=0)
    reg = (tot + ffn_tile - 1) // ffn_tile * ffn_tile
    base = jnp.cumsum(reg) - reg
    goff = base[None, :] + jnp.cumsum(cntp, axis=0) - cntp
    tile_end = jnp.cumsum(reg // ffn_tile)
    tile = jnp.arange(n_tiles_max, dtype=jnp.int32)
    tile_expert = jnp.minimum(jnp.sum(tile_end[None, :] <= tile[:, None], axis=1), N_EXPERTS - 1)
    i32 = lambda a: a.astype(jnp.int32)
    block_tables = (i32(cntp.reshape(-1)), i32(loff.reshape(-1)), i32(goff.reshape(-1)),
                    i32(block_rows), i32(has_long))
    return block_tables, i32(base + tot), i32(reg - tot), i32(tile_expert), i32(tile_end[-1:])


def _count_blocks(gates_t, params):
    n_tok = gates_t.shape[1]
    chunk = min(n_tok, COUNT_LANES * MOE_BLOCK // 16)
    per_chunk = chunk // MOE_BLOCK
    cnt = pl.pallas_call(
        _count_kernel,
        grid=(n_tok // chunk,),
        in_specs=[pl.BlockSpec((N_EXPERTS, chunk), lambda i: (0, i))],
        out_specs=pl.BlockSpec((1, N_EXPERTS, COUNT_LANES), lambda i: (i, 0, 0)),
        out_shape=jax.ShapeDtypeStruct((n_tok // chunk, N_EXPERTS, COUNT_LANES), F32),
        compiler_params=params,
        name="moe_count",
    )(gates_t)
    return jnp.swapaxes(cnt[:, :, :per_chunk], 1, 2).reshape(n_tok // MOE_BLOCK, N_EXPERTS).astype(jnp.int32)


def _stream_specs(x1, block_of):
    n_seq, seq_len, _ = x1.shape
    if seq_len < STEP_TOKENS:
        ns, ls = STEP_TOKENS // seq_len, seq_len
        x_map = lambda b, *_: (block_of(b), 0, 0)
        mod_map = x_map
    else:
        ns, ls = 1, STEP_TOKENS
        per_seq = seq_len // STEP_TOKENS
        x_map = lambda b, *_: (block_of(b) // per_seq, block_of(b) % per_seq, 0)
        mod_map = lambda b, *_: (block_of(b) // per_seq, 0, 0)
    return pl.BlockSpec((ns, ls, D_MODEL), x_map), pl.BlockSpec((ns, MOD_ROWS, D_MODEL), mod_map)


def _moe(stream_a, stream_b, w):
    (h2_a, g_a, x1_a, mod_a), (h2_b, g_b, x1_b, mod_b) = stream_a, stream_b
    nb_a, nb_b = h2_a.shape[0] // MOE_BLOCK, h2_b.shape[0] // MOE_BLOCK
    n_blocks = nb_a + nb_b
    n_tok = n_blocks * MOE_BLOCK
    ffn_tile = FFN_TILE
    rows_max = (n_tok * TOP_K + n_blocks * N_EXPERTS * SEG_ALIGN
                + N_EXPERTS * (ffn_tile - SEG_ALIGN))
    n_tiles_max = 2 * -(-rows_max // (2 * ffn_tile))
    rows_alloc = n_tiles_max * ffn_tile
    params = pltpu.CompilerParams(dimension_semantics=("arbitrary",), vmem_limit_bytes=VMEM_LIMIT)

    cnt = jnp.concatenate([_count_blocks(g_a, params), _count_blocks(g_b, params)], axis=0)
    block_tables, tail_off, tail_len, tile_expert, n_used = _segment_plan(cnt, n_tiles_max, ffn_tile)

    assert nb_a % STEP_BLOCKS == 0 and nb_b % STEP_BLOCKS == 0
    steps_a, n_steps = nb_a // STEP_BLOCKS, n_blocks // STEP_BLOCKS
    in_a = lambda s: jnp.minimum(s, steps_a - 1)
    in_b = lambda s: jnp.maximum(s - steps_a, 0)
    tok_specs = [pl.BlockSpec((N_EXPERTS, STEP_TOKENS), lambda s, *_: (0, in_a(s))),
                 pl.BlockSpec((N_EXPERTS, STEP_TOKENS), lambda s, *_: (0, in_b(s))),
                 pl.BlockSpec((STEP_TOKENS, D_MODEL), lambda s, *_: (in_a(s), 0)),
                 pl.BlockSpec((STEP_TOKENS, D_MODEL), lambda s, *_: (in_b(s), 0))]
    any_spec = pl.BlockSpec(memory_space=pl.ANY)
    perm_scratch = [pltpu.VMEM((BLOCK_ROWS, MOE_BLOCK), BF16), pltpu.VMEM((N_EXPERTS, MOE_BLOCK), F32),
                    pltpu.VMEM((N_EXPERTS, STEP_TOKENS), F32)]
    row_buffers = pltpu.VMEM((2 * STEP_BLOCKS, BLOCK_ROWS, D_MODEL), BF16)

    x_sorted = pl.pallas_call(
        functools.partial(_dispatch_kernel, n_steps=n_steps, n_steps_a=steps_a),
        grid_spec=pltpu.PrefetchScalarGridSpec(
            num_scalar_prefetch=7, grid=(n_steps,),
            in_specs=tok_specs, out_specs=any_spec,
            scratch_shapes=perm_scratch + [
                pltpu.VMEM((STEP_TOKENS, D_MODEL), BF16),
                row_buffers,
                pltpu.VMEM((ffn_tile, D_MODEL), BF16),
                pltpu.SemaphoreType.DMA((2 * STEP_BLOCKS,)), pltpu.SemaphoreType.DMA(())]),
        out_shape=jax.ShapeDtypeStruct((rows_alloc, D_MODEL), BF16),
        compiler_params=params,
        name="moe_dispatch",
    )(*block_tables, tail_off, tail_len, g_a, g_b, h2_a, h2_b)

    last_used = lambda i, nu: jnp.maximum(jnp.minimum(i, (nu[0] - 1) // 2), 0)
    row_map = lambda i, te, nu: (last_used(i, nu), 0)
    w_first = lambda i, te, nu: (te[2 * last_used(i, nu)], 0, 0)
    w_second = lambda i, te, nu: (te[2 * last_used(i, nu) + 1], 0, 0)
    gate_up, down = (1, D_MODEL, EXPERT_FF), (1, EXPERT_FF, D_MODEL)
    y_sorted = pl.pallas_call(
        _ffn_kernel,
        grid_spec=pltpu.PrefetchScalarGridSpec(
            num_scalar_prefetch=2, grid=(n_tiles_max // 2,),
            in_specs=[pl.BlockSpec((2 * ffn_tile, D_MODEL), row_map),
                      pl.BlockSpec(gate_up, w_first), pl.BlockSpec(gate_up, w_first),
                      pl.BlockSpec(down, w_first),
                      pl.BlockSpec(gate_up, w_second), pl.BlockSpec(gate_up, w_second),
                      pl.BlockSpec(down, w_second)],
            out_specs=pl.BlockSpec((2 * ffn_tile, D_MODEL), row_map),
            scratch_shapes=[pltpu.VMEM((2, D_MODEL, 2 * EXPERT_FF), BF16),
                            pltpu.VMEM((2, EXPERT_FF, D_MODEL), BF16),
                            pltpu.SMEM((2,), jnp.int32)]),
        out_shape=jax.ShapeDtypeStruct((rows_alloc, D_MODEL), BF16),
        compiler_params=params,
        name="moe_ffn",
    )(tile_expert, n_used, x_sorted, w["w_e_gate"], w["w_e_up"], w["w_e_down"],
      w["w_e_gate"], w["w_e_up"], w["w_e_down"])

    full2 = lambda b, *_: (0, 0)
    x_spec_a, mod_spec_a = _stream_specs(x1_a, in_a)
    x_spec_b, mod_spec_b = _stream_specs(x1_b, in_b)
    return pl.pallas_call(
        functools.partial(_combine_kernel, n_steps=n_steps, n_steps_a=steps_a),
        grid_spec=pltpu.PrefetchScalarGridSpec(
            num_scalar_prefetch=5, grid=(n_steps,),
            in_specs=tok_specs + [
                x_spec_a, x_spec_b, mod_spec_a, mod_spec_b,
                pl.BlockSpec((D_MODEL, 2 * SHARED_FF), full2),
                pl.BlockSpec((SHARED_FF, D_MODEL), full2),
                pl.BlockSpec((1, D_MODEL), full2),
                any_spec],
            out_specs=[x_spec_a, x_spec_b],
            scratch_shapes=perm_scratch + [
                row_buffers,
                pltpu.VMEM((STEP_TOKENS, D_MODEL), F32),
                pltpu.SemaphoreType.DMA((2 * STEP_BLOCKS,))]),
        out_shape=[jax.ShapeDtypeStruct(x1_a.shape, F32), jax.ShapeDtypeStruct(x1_b.shape, F32)],
        compiler_params=params,
        name="moe_combine",
    )(*block_tables, g_a, g_b, h2_a, h2_b, x1_a, x1_b, mod_a, mod_b,
      w["w_sh_gu"], w["w_sh_down"], w["g_post2"], y_sorted)


def _prep_weights(g_pre1, g_post1, w_in, w_gk_up, b_gk, g_gmlp_v, w_s, b_s, g_gla_o, w_out, g_pre2,
                  g_post2, w_router, b_router, w_e_gate, w_e_up, w_e_down, w_sh_gate, w_sh_up,
                  w_sh_down):
    row = lambda v: v.reshape(1, -1)
    wr = jnp.pad(w_router, ((0, 0), (0, LANES - N_EXPERTS)))
    wr_top = lax.bitcast_convert_type(
        lax.bitcast_convert_type(wr, jnp.uint32) & jnp.uint32(0xFFFF0000), F32)
    wr_hi = wr_top.astype(BF16)
    wr_lo = (wr - wr_top).astype(BF16)
    wr_split = jnp.concatenate([jnp.concatenate([wr_hi, wr_lo], axis=1),
                                jnp.concatenate([wr_hi, jnp.zeros_like(wr_lo)], axis=1)], axis=0)
    return {
        "g_pre1": row(g_pre1), "g_post1": row(g_post1), "g_pre2": row(g_pre2), "g_post2": row(g_post2),
        "w_in": jnp.concatenate([w_in[:, :OFF_Z], _fold_gate(w_in[:, OFF_Z:], w_gk_up)],
                                axis=1).astype(BF16),
        "b_gk": row(b_gk),
        "g_gmlp_v": row(g_gmlp_v),
        "w_s": w_s,
        "b_s_full": jnp.repeat(b_s.T, GMLP_HEAD, axis=1),
        "g_gla_o": g_gla_o,
        "w_out": w_out.astype(BF16),
        "w_router": wr_split,
        "b_router": b_router.reshape(N_EXPERTS, 1),
        "w_e_gate": w_e_gate, "w_e_up": w_e_up, "w_e_down": w_e_down,
        "w_sh_gu": jnp.concatenate([w_sh_gate, w_sh_up], axis=-1).astype(BF16),
        "w_sh_down": w_sh_down.astype(BF16),
    }


PROMPT_TILE = 1024
SAMPLE_TILE = 256


def _mix(x, mod, s0, w, *, emit_v):
    n_seq, seq_len, _ = x.shape
    s0_t = jnp.swapaxes(s0, -1, -2)
    if seq_len >= GMLP_CHUNK:
        outs = _mixer(x, mod, s0_t, w, seq_tile=PROMPT_TILE, chunk=GMLP_CHUNK, emit_v=emit_v)
    else:
        outs = _mixer(x, mod, s0_t, w, seq_tile=SAMPLE_TILE, chunk=seq_len, emit_v=emit_v)
    x1, h2, gates_t, st = outs[:4]
    v = outs[4].reshape(n_seq, seq_len, GMLP_GROUPS, GMLP_HEAD) if emit_v else None
    return (h2, gates_t, x1, mod), jnp.swapaxes(st, -1, -2), v


def kernel(x_prompt, x_sample, state_gla, c_prompt, c_sample, w_ada, b_ada, g_pre1, g_post1, w_in, w_gk_up, b_gk, g_gmlp_v, w_s, b_s, g_gla_o, w_out, g_pre2, g_post2, w_router, b_router, w_e_gate, w_e_up, w_e_down, w_sh_gate, w_sh_up, w_sh_down):
    depth = w_ada.shape[0]
    n_p, n_s = x_prompt.shape[0], x_sample.shape[0]
    c_all = jnp.concatenate([c_prompt, c_sample], axis=0)
    x_p, x_s = x_prompt, x_sample
    sp_list, ss_list, vs_list = [], [], []
    for l in range(depth):
        w = _prep_weights(g_pre1[l], g_post1[l], w_in[l], w_gk_up[l], b_gk[l], g_gmlp_v[l], w_s[l],
                          b_s[l], g_gla_o[l], w_out[l], g_pre2[l], g_post2[l], w_router[l],
                          b_router[l], w_e_gate[l], w_e_up[l], w_e_down[l], w_sh_gate[l], w_sh_up[l],
                          w_sh_down[l])
        mod = _ada(c_all, w_ada[l], b_ada[l])
        mod = jnp.pad(jnp.swapaxes(mod, 0, 1), ((0, 0), (0, MOD_ROWS - 6), (0, 0)))
        s0_p = jnp.zeros((n_p, GLA_HEADS, GLA_DK, GLA_DV), F32)
        moe_p, s_p, _ = _mix(x_p, mod[:n_p], s0_p, w, emit_v=False)
        moe_s, s_s, v_s = _mix(x_s, mod[n_p:], state_gla[l], w, emit_v=True)
        x_p, x_s = _moe(moe_p, moe_s, w)
        sp_list.append(s_p)
        ss_list.append(s_s)
        vs_list.append(v_s)
    return (x_p, x_s, jnp.stack(sp_list), jnp.stack(ss_list), jnp.stack(vs_list))
```

```python
import functools

import jax
import jax.numpy as jnp
from jax import lax
from jax.experimental import pallas as pl
from jax.experimental.pallas import tpu as pltpu

D_MODEL = 1024
GMLP_WIDTH = 512
GMLP_GROUPS = 4
GMLP_HEAD = 128
GMLP_CHUNK = 128
CAUSAL_BLOCK = 64
GLA_WIDTH = 512
GLA_HEADS = 4
GLA_DV = 128
GLA_DK = 64
GLA_KEY_WIDTH = 256
GLA_GATE_RANK = 16
GLA_GATE_NORMALIZER = 16.0
N_EXPERTS = 64
TOP_K = 8
N_EXPERT_GROUPS = 8
GROUP_SIZE = N_EXPERTS // N_EXPERT_GROUPS
TOPK_GROUPS = 4
EXPERT_FF = 256
SHARED_FF = 256
ROUTED_SCALE = 2.5
EPS = 1e-6

LANES = 128
GK_PAD = LANES
IN_WIDTH_PAD = 2 * GMLP_WIDTH + 2 * GLA_KEY_WIDTH + 2 * GLA_WIDTH + GLA_KEY_WIDTH
OFF_U = 0
OFF_VG = GMLP_WIDTH
OFF_Q = 2 * GMLP_WIDTH
OFF_K = OFF_Q + GLA_KEY_WIDTH
OFF_VL = OFF_K + GLA_KEY_WIDTH
OFF_R = OFF_VL + GLA_WIDTH
OFF_Z = OFF_R + GLA_WIDTH
MOD_ROWS = 8
VMEM_LIMIT = 56 * 1024 * 1024

F32 = jnp.float32
BF16 = jnp.bfloat16
NT_DIMS = (((1,), (1,)), ((), ()))
TN_DIMS = (((0,), (0,)), ((), ()))


def _rms(x, g):
    return x * lax.rsqrt(jnp.mean(x * x, axis=-1, keepdims=True) + EPS) * g


def _gelu(x):
    return 0.5 * x * (1.0 + jnp.tanh(0.7978845608028654 * (x + 0.044715 * (x * x * x))))


def _sigmoid(x):
    return 1.0 / (1.0 + jnp.exp(-x))


def _silu(x):
    return x * _sigmoid(x)


def _ada_kernel(c_ref, w_ref, b_ref, o_ref):
    a = _silu(c_ref[...])
    o_ref[0] = jnp.dot(a, w_ref[...], precision=lax.Precision.HIGHEST,
                       preferred_element_type=F32) + b_ref[0]


def _ada(c_all, w_ada, b_ada):
    n = c_all.shape[0]
    return pl.pallas_call(
        _ada_kernel,
        grid=(6,),
        in_specs=[pl.BlockSpec((n, D_MODEL), lambda j: (0, 0)),
                  pl.BlockSpec((D_MODEL, D_MODEL), lambda j: (0, j)),
                  pl.BlockSpec((1, 1, D_MODEL), lambda j: (j, 0, 0))],
        out_specs=pl.BlockSpec((1, n, D_MODEL), lambda j: (j, 0, 0)),
        out_shape=jax.ShapeDtypeStruct((6, n, D_MODEL), F32),
        compiler_params=pltpu.CompilerParams(vmem_limit_bytes=VMEM_LIMIT),
        name="ada",
    )(c_all, w_ada, b_ada.reshape(6, 1, D_MODEL))


def _fold_gate_kernel(a_ref, b_ref, o_ref):
    o_ref[...] = jnp.dot(a_ref[...], b_ref[...], precision=lax.Precision.HIGHEST,
                         preferred_element_type=F32)


def _fold_gate(w_in_gate, w_gk_up):
    a = jnp.pad(w_in_gate, ((0, 0), (0, GK_PAD - GLA_GATE_RANK)))
    b = jnp.pad(w_gk_up, ((0, GK_PAD - GLA_GATE_RANK), (0, 0)))
    return pl.pallas_call(
        _fold_gate_kernel,
        out_shape=jax.ShapeDtypeStruct((D_MODEL, GLA_KEY_WIDTH), F32),
        name="fold_gate",
    )(a, b)


def _route(logits_t, bias_t):
    t = logits_t.shape[1]
    scores = _sigmoid(logits_t)
    sel = scores + bias_t
    sub = lax.broadcasted_iota(jnp.int32, (GROUP_SIZE, t), 0)
    gscore = []
    for g in range(N_EXPERT_GROUPS):
        blk = sel[g * GROUP_SIZE:(g + 1) * GROUP_SIZE]
        m1 = jnp.max(blk, axis=0, keepdims=True)
        first = jnp.min(jnp.where(blk == m1, sub, GROUP_SIZE), axis=0, keepdims=True)
        m2 = jnp.max(jnp.where(sub == first, -jnp.inf, blk), axis=0, keepdims=True)
        gscore.append(m1 + m2)
    neg = jnp.full((GROUP_SIZE, t), -jnp.inf, F32)
    masked = []
    for g in range(N_EXPERT_GROUPS):
        rank = jnp.zeros((1, t), jnp.int32)
        for o in range(N_EXPERT_GROUPS):
            if o == g:
                continue
            ahead = (gscore[o] >= gscore[g]) if o < g else (gscore[o] > gscore[g])
            rank = rank + jnp.where(ahead, 1, 0)
        keep = jnp.broadcast_to(rank < TOPK_GROUPS, (GROUP_SIZE, t))
        masked.append(jnp.where(keep, sel[g * GROUP_SIZE:(g + 1) * GROUP_SIZE], neg))
    selm = jnp.concatenate(masked, axis=0)
    eidx = lax.broadcasted_iota(jnp.int32, (N_EXPERTS, t), 0)
    picked = jnp.zeros((N_EXPERTS, t), F32)
    for _ in range(TOP_K):
        best = jnp.max(selm, axis=0, keepdims=True)
        first = jnp.min(jnp.where(selm == best, eidx, N_EXPERTS), axis=0, keepdims=True)
        hit = eidx == first
        picked = jnp.where(hit, 1.0, picked)
        selm = jnp.where(hit, -jnp.inf, selm)
    chosen = jnp.where(picked > 0.0, scores, 0.0)
    denom = jnp.sum(chosen, axis=0, keepdims=True)
    return chosen * (ROUTED_SCALE / denom)


def _mixer_kernel(x_ref, mod_ref, s0_ref, gpre1_ref, win_ref, bgk_ref, ggv_ref,
                  ws_ref, bsf_ref, ggo_ref, wout_ref, gpost1_ref, gpre2_ref, wr_ref, br_ref,
                  x1_ref, h2_ref, gates_ref, st_ref, cnt_ref, *rest,
                  chunk, n_chunks, chunk_is_seq, emit_v):
    if emit_v:
        v_ref, *rest = rest
    proj, mix_scr = rest
    ns, ls, _ = x_ref.shape
    tm = ns * ls

    mod_row = lambda i: mod_ref[:, i:i + 1, :]

    h = _rms(x_ref[...], gpre1_ref[...]) * (1.0 + mod_row(1)) + mod_row(0)
    proj[...] = jnp.dot(h.reshape(tm, D_MODEL).astype(BF16), win_ref[...],
                        preferred_element_type=F32)

    if chunk_is_seq:
        st_ref[...] = s0_ref[...]
    else:
        @pl.when(pl.program_id(1) == 0)
        def _():
            st_ref[...] = s0_ref[...]

    row = lax.broadcasted_iota(jnp.int32, (chunk, chunk), 0)
    col = lax.broadcasted_iota(jnp.int32, (chunk, chunk), 1)
    tri = jnp.where(row >= col, 1.0, 0.0).astype(BF16)
    tri2 = jnp.concatenate([tri, tri], axis=1)
    causal = row >= col
    block_causal = (row // CAUSAL_BLOCK) >= (col // CAUSAL_BLOCK)
    wmix = [jnp.where(block_causal, ws_ref[g, 0:chunk, 0:chunk], 0.0).astype(BF16)
            for g in range(GMLP_GROUPS)]
    mid = chunk // 2

    for c in range(n_chunks):
        rows = slice(c * chunk, (c + 1) * chunk)
        z = proj[rows, OFF_Z:OFF_Z + GLA_KEY_WIDTH] + bgk_ref[...]
        la = (jnp.minimum(z, 0.0) - jnp.log(1.0 + jnp.exp(-jnp.abs(z)))) * (1.0 / GLA_GATE_NORMALIZER)
        la_hi = la.astype(BF16)
        la_lo = (la - la_hi.astype(F32)).astype(BF16)
        proj[rows, OFF_Z:OFF_Z + GLA_KEY_WIDTH] = jnp.dot(
            tri2, jnp.concatenate([la_hi, la_lo], axis=0), preferred_element_type=F32)

    def chunk_body(c, carry):
        rows = pl.ds(pl.multiple_of(c * chunk, chunk), chunk)
        sidx = c if chunk_is_seq else 0

        vg = _gelu(proj[rows, OFF_VG:OFF_VG + GMLP_WIDTH])
        vn = _rms(vg, ggv_ref[...])
        if emit_v:
            v_ref[rows, :] = vn
        vnb = vn.astype(BF16)
        mixed = jnp.concatenate(
            [jnp.dot(wmix[g], vnb[:, g * GMLP_HEAD:(g + 1) * GMLP_HEAD], preferred_element_type=F32)
             for g in range(GMLP_GROUPS)], axis=-1) + bsf_ref[0:chunk, :]
        u = _gelu(proj[rows, OFF_U:OFF_U + GMLP_WIDTH])
        mix_scr[rows, 0:GMLP_WIDTH] = (u * mixed).astype(BF16)

        q = proj[rows, OFF_Q:OFF_Q + GLA_KEY_WIDTH] * (GLA_DK ** -0.5)
        k = proj[rows, OFF_K:OFF_K + GLA_KEY_WIDTH]
        b = proj[rows, OFF_Z:OFF_Z + GLA_KEY_WIDTH]
        b_mid = b[mid - 1:mid]
        b_end = b[chunk - 1:chunk]
        qt = (q * jnp.exp(b - b_mid)).astype(BF16)
        kt = (k * jnp.exp(b_mid - b)).astype(BF16)
        qs = (q * jnp.exp(b)).astype(BF16)
        kd = (k * jnp.exp(b_end - b)).astype(BF16)
        decay = jnp.exp(b_end)
        for hd in range(GLA_HEADS):
            ksl = slice(hd * GLA_DK, (hd + 1) * GLA_DK)
            vsl = slice(OFF_VL + hd * GLA_DV, OFF_VL + (hd + 1) * GLA_DV)
            rsl = slice(OFF_R + hd * GLA_DV, OFF_R + (hd + 1) * GLA_DV)
            att = lax.dot_general(qt[:, ksl], kt[:, ksl], NT_DIMS, preferred_element_type=F32)
            att = jnp.where(causal, att, 0.0).astype(BF16)
            vh = proj[rows, vsl].astype(BF16)
            st = st_ref[sidx, hd]
            o = jnp.dot(att, vh, preferred_element_type=F32) + lax.dot_general(
                qs[:, ksl], st.astype(BF16), NT_DIMS, preferred_element_type=F32)
            on = _rms(o, ggo_ref[hd:hd + 1, :])
            mix_scr[rows, GMLP_WIDTH + hd * GLA_DV:GMLP_WIDTH + (hd + 1) * GLA_DV] = (
                on * _silu(proj[rows, rsl])).astype(BF16)
            st_ref[sidx, hd] = st * decay[:, ksl] + lax.dot_general(
                vh, kd[:, ksl], TN_DIMS, preferred_element_type=F32)
        return carry

    lax.fori_loop(0, n_chunks, chunk_body, 0, unroll=4)

    mixo = jnp.dot(mix_scr[...], wout_ref[...], preferred_element_type=F32).reshape(ns, ls, D_MODEL)
    x1 = x_ref[...] + mod_row(2) * _rms(mixo, gpost1_ref[...])
    x1_ref[...] = x1
    h2 = (_rms(x1, gpre2_ref[...]) * (1.0 + mod_row(4)) + mod_row(3)).reshape(tm, D_MODEL)
    h2_hi = h2.astype(BF16)
    h2_ref[...] = h2_hi

    h2_lo = (h2 - h2_hi.astype(F32)).astype(BF16)
    parts = jnp.dot(jnp.concatenate([h2_hi, h2_lo], axis=-1), wr_ref[...], preferred_element_type=F32)
    logits = parts[:, 0:LANES] + parts[:, LANES:2 * LANES]
    gates = _route(logits.T[0:N_EXPERTS], br_ref[...])
    gates_ref[...] = gates
    tok = lax.broadcasted_iota(jnp.int32, (tm, LANES), 0)
    lane = lax.broadcasted_iota(jnp.int32, (tm, LANES), 1)
    in_block = jnp.where(tok // MOE_BLOCK == lane, 1.0, 0.0).astype(BF16)
    cnt_ref[0] = jnp.dot(jnp.where(gates > 0.0, 1.0, 0.0).astype(BF16), in_block,
                         preferred_element_type=F32)


def _mixer(x, mod, s0_t, w, *, seq_tile, chunk, emit_v):
    n_seq, seq_len, _ = x.shape
    chunk_is_seq = seq_len == chunk
    if chunk_is_seq:
        ns, ls = seq_tile // chunk, chunk
        grid = (1, n_seq // ns)
        x_map = lambda b, t: (t, 0, 0)
        seq_map3 = lambda b, t: (t, 0, 0)
        seq_map4 = lambda b, t: (t, 0, 0, 0)
    else:
        ns, ls = 1, seq_tile
        grid = (n_seq, seq_len // seq_tile)
        x_map = lambda b, t: (b, t, 0)
        seq_map3 = lambda b, t: (b, 0, 0)
        seq_map4 = lambda b, t: (b, 0, 0, 0)
    tm = ns * ls
    n_tok = n_seq * seq_len
    tok_block = lambda b, t: b * grid[1] + t
    tok_map = lambda b, t: (tok_block(b, t), 0)
    full2 = lambda b, t: (0, 0)
    full3 = lambda b, t: (0, 0, 0)

    in_specs = [
        pl.BlockSpec((ns, ls, D_MODEL), x_map),
        pl.BlockSpec((ns, MOD_ROWS, D_MODEL), seq_map3),
        pl.BlockSpec((ns, GLA_HEADS, GLA_DV, GLA_DK), seq_map4),
        pl.BlockSpec((1, D_MODEL), full2),
        pl.BlockSpec((D_MODEL, IN_WIDTH_PAD), full2),
        pl.BlockSpec((1, GLA_KEY_WIDTH), full2),
        pl.BlockSpec((1, GMLP_WIDTH), full2),
        pl.BlockSpec((GMLP_GROUPS, GMLP_CHUNK, GMLP_CHUNK), full3),
        pl.BlockSpec((GMLP_CHUNK, GMLP_WIDTH), full2),
        pl.BlockSpec((GLA_HEADS, GLA_DV), full2),
        pl.BlockSpec((D_MODEL, D_MODEL), full2),
        pl.BlockSpec((1, D_MODEL), full2),
        pl.BlockSpec((1, D_MODEL), full2),
        pl.BlockSpec((2 * D_MODEL, 2 * LANES), full2),
        pl.BlockSpec((N_EXPERTS, 1), full2),
    ]
    out_specs = [
        pl.BlockSpec((ns, ls, D_MODEL), x_map),
        pl.BlockSpec((tm, D_MODEL), tok_map),
        pl.BlockSpec((N_EXPERTS, tm), lambda b, t: (0, tok_block(b, t))),
        pl.BlockSpec((ns, GLA_HEADS, GLA_DV, GLA_DK), seq_map4),
        pl.BlockSpec((1, N_EXPERTS, LANES), lambda b, t: (tok_block(b, t), 0, 0)),
    ]
    out_shape = [
        jax.ShapeDtypeStruct((n_seq, seq_len, D_MODEL), F32),
        jax.ShapeDtypeStruct((n_tok, D_MODEL), BF16),
        jax.ShapeDtypeStruct((N_EXPERTS, n_tok), F32),
        jax.ShapeDtypeStruct((n_seq, GLA_HEADS, GLA_DV, GLA_DK), F32),
        jax.ShapeDtypeStruct((n_tok // tm, N_EXPERTS, LANES), F32),
    ]
    if emit_v:
        out_specs.append(pl.BlockSpec((tm, GMLP_WIDTH), tok_map))
        out_shape.append(jax.ShapeDtypeStruct((n_tok, GMLP_WIDTH), F32))

    kern = functools.partial(_mixer_kernel, chunk=chunk, n_chunks=tm // chunk,
                             chunk_is_seq=chunk_is_seq, emit_v=emit_v)
    return pl.pallas_call(
        kern,
        grid=grid,
        in_specs=in_specs,
        out_specs=out_specs,
        out_shape=out_shape,
        scratch_shapes=[pltpu.VMEM((tm, IN_WIDTH_PAD), F32), pltpu.VMEM((tm, D_MODEL), BF16)],
        compiler_params=pltpu.CompilerParams(
            dimension_semantics=("arbitrary", "arbitrary"), vmem_limit_bytes=VMEM_LIMIT),
        name="mixer_sample" if chunk_is_seq else "mixer_prompt",
    )(x, mod, s0_t, w["g_pre1"], w["w_in"], w["b_gk"], w["g_gmlp_v"], w["w_s"],
      w["b_s_full"], w["g_gla_o"], w["w_out"], w["g_post1"], w["g_pre2"], w["w_router"],
      w["b_router"])


MOE_BLOCK = 256
SEG_ALIGN = 16
WIN = 64
FFN_TILE = 1024
BLOCK_ROWS = MOE_BLOCK * TOP_K + SEG_ALIGN * N_EXPERTS + WIN
MAIN_ROWS = 2688
STEP_BLOCKS = 2
STEP_TOKENS = STEP_BLOCKS * MOE_BLOCK
COL_CHUNK = 256


def _fill_permutation(p_scr, rk_scr, g_ref, cntp_ref, loff_ref, long_ref, blk, weighted):
    tb = g_ref.shape[1]
    g = g_ref[...]
    sel = g > 0.0
    r_i = lax.broadcasted_iota(jnp.int32, (tb, tb), 0)
    c_i = lax.broadcasted_iota(jnp.int32, (tb, tb), 1)
    earlier = jnp.where(r_i < c_i, 1.0, 0.0).astype(BF16)
    rank = jnp.dot(jnp.where(sel, 1.0, 0.0).astype(BF16), earlier, preferred_element_type=F32)
    rk = jnp.where(sel, rank, -1.0)
    p_scr[...] = jnp.zeros_like(p_scr)
    jwin = lax.broadcasted_iota(jnp.int32, (WIN, tb), 0).astype(F32)

    for e in range(N_EXPERTS):
        off = pl.multiple_of(loff_ref[blk * N_EXPERTS + e], SEG_ALIGN)
        val = g[e:e + 1] if weighted else 1.0
        p_scr[pl.ds(off, WIN), :] = jnp.where(rk[e:e + 1] == jwin, val, 0.0).astype(BF16)

    @pl.when(long_ref[blk] > 0)
    def _():
        rk_scr[...] = rk

        def expert_body(e, c):
            n = cntp_ref[blk * N_EXPERTS + e]
            off = loff_ref[blk * N_EXPERTS + e]
            rke = rk_scr[pl.ds(e, 1), :]
            val = g_ref[pl.ds(e, 1), :] if weighted else 1.0

            def win_body(wi, c2):
                rows = pl.ds(pl.multiple_of(off + wi * WIN, SEG_ALIGN), WIN)
                j = jwin + (wi * WIN).astype(F32)
                new = jnp.where(rke == j, val, 0.0)
                p_scr[rows, :] = jnp.where(j < n.astype(F32), new, p_scr[rows, :].astype(F32)).astype(BF16)
                return c2

            lax.fori_loop(1, (n + WIN - 1) // WIN, win_body, 0)
            return c

        lax.fori_loop(0, N_EXPERTS, expert_body, 0)


def _start_segments(cntp_ref, loff_ref, blk, make_copy):
    for e in range(N_EXPERTS):
        make_copy(e, pl.multiple_of(loff_ref[blk * N_EXPERTS + e], SEG_ALIGN),
                  pl.multiple_of(cntp_ref[blk * N_EXPERTS + e], SEG_ALIGN)).start()


def _dispatch_kernel(cntp_ref, loff_ref, goff_ref, rows_ref, long_ref, toff_ref, tlen_ref,
                     ga_ref, gb_ref, h2a_ref, h2b_ref, xs_hbm,
                     p_scr, rk_scr, g_scr, h2_scr, xs_scr, zero_scr, sems, tail_sem,
                     *, n_steps, n_steps_a):
    step = pl.program_id(0)
    parity = lax.rem(step, 2)
    from_a = step < n_steps_a
    g_scr[...] = jnp.where(from_a, ga_ref[...], gb_ref[...])
    h2_scr[...] = jnp.where(from_a, h2a_ref[...], h2b_ref[...])

    def seg_copy(blk, s):
        def make(e, off, n):
            dst = pl.multiple_of(goff_ref[blk * N_EXPERTS + e], SEG_ALIGN)
            return pltpu.make_async_copy(xs_scr.at[s, pl.ds(off, n)], xs_hbm.at[pl.ds(dst, n)], sems.at[s])
        return make

    def wait_block(blk, s):
        n = pl.multiple_of(rows_ref[blk], SEG_ALIGN)
        pltpu.make_async_copy(xs_scr.at[s, pl.ds(0, n)], xs_hbm.at[pl.ds(0, n)], sems.at[s]).wait()

    def for_tails(action):
        def body(e, c):
            n = tlen_ref[e]

            @pl.when(n > 0)
            def _():
                dst = pl.multiple_of(toff_ref[e], SEG_ALIGN)
                nn = pl.multiple_of(n, SEG_ALIGN)
                action(pltpu.make_async_copy(zero_scr.at[pl.ds(0, nn)], xs_hbm.at[pl.ds(dst, nn)], tail_sem))

            return c

        lax.fori_loop(0, N_EXPERTS, body, 0)

    @pl.when(step == 0)
    def _():
        zero_scr[...] = jnp.zeros_like(zero_scr)
        for_tails(lambda cp: cp.start())

    for j in range(STEP_BLOCKS):
        blk = step * STEP_BLOCKS + j
        slot = parity * STEP_BLOCKS + j
        tok = slice(j * MOE_BLOCK, (j + 1) * MOE_BLOCK)

        @pl.when(step >= 2)
        def _():
            wait_block(blk - 2 * STEP_BLOCKS, slot)

        _fill_permutation(p_scr, rk_scr, g_scr.at[:, tok], cntp_ref, loff_ref, long_ref, blk,
                          weighted=False)

        def sort_rows(rows):
            for c in range(D_MODEL // COL_CHUNK):
                cols = slice(c * COL_CHUNK, (c + 1) * COL_CHUNK)
                xs = jnp.dot(p_scr[rows, :], h2_scr[tok, cols], preferred_element_type=F32)
                xs_scr[slot, rows, cols] = xs.astype(BF16)

        sort_rows(slice(0, MAIN_ROWS))

        @pl.when(rows_ref[blk] > MAIN_ROWS)
        def _():
            sort_rows(slice(MAIN_ROWS, BLOCK_ROWS))

        _start_segments(cntp_ref, loff_ref, blk, seg_copy(blk, slot))

    @pl.when(step == n_steps - 1)
    def _():
        for_tails(lambda cp: cp.wait())
        for j in range(STEP_BLOCKS):
            blk = step * STEP_BLOCKS + j
            if n_steps >= 2:
                wait_block(blk - STEP_BLOCKS, (1 - parity) * STEP_BLOCKS + j)
            wait_block(blk, parity * STEP_BLOCKS + j)


def _ffn_kernel(te_ref, nu_ref, x_ref, wga_ref, wua_ref, wda_ref, wgb_ref, wub_ref, wdb_ref, y_ref,
                wgu_scr, wd_scr, held_ref):
    i = pl.program_id(0)
    tile = x_ref.shape[0] // 2
    first, second = 2 * i, 2 * i + 1
    e_first, e_second = te_ref[first], te_ref[second]
    one_expert = jnp.logical_and(second < nu_ref[0], e_second == e_first)
    two_experts = jnp.logical_and(second < nu_ref[0], e_second != e_first)

    @pl.when(i == 0)
    def _():
        held_ref[0] = -1
        held_ref[1] = -1

    def hold(k, expert, wg_ref, wu_ref, wd_ref):
        @pl.when(held_ref[k] != expert)
        def _():
            wgu_scr[k, :, 0:EXPERT_FF] = wg_ref[0].astype(BF16)
            wgu_scr[k, :, EXPERT_FF:2 * EXPERT_FF] = wu_ref[0].astype(BF16)
            wd_scr[k] = wd_ref[0].astype(BF16)
            held_ref[k] = expert

    def ffn(rows, k):
        gu = jnp.dot(x_ref[rows, :], wgu_scr[k], preferred_element_type=F32)
        hid = _silu(gu[:, 0:EXPERT_FF]) * gu[:, EXPERT_FF:2 * EXPERT_FF]
        y_ref[rows, :] = jnp.dot(hid.astype(BF16), wd_scr[k], preferred_element_type=F32).astype(BF16)

    @pl.when(first < nu_ref[0])
    def _():
        k_first = jnp.where(held_ref[1] == e_first, 1, 0)
        hold(k_first, e_first, wga_ref, wua_ref, wda_ref)

        @pl.when(one_expert)
        def _():
            ffn(slice(0, 2 * tile), k_first)

        @pl.when(jnp.logical_not(one_expert))
        def _():
            ffn(slice(0, tile), k_first)

        @pl.when(two_experts)
        def _():
            hold(1 - k_first, e_second, wgb_ref, wub_ref, wdb_ref)
            ffn(slice(tile, 2 * tile), 1 - k_first)


def _combine_kernel(cntp_ref, loff_ref, goff_ref, rows_ref, long_ref,
                    ga_ref, gb_ref, h2a_ref, h2b_ref, x1a_ref, x1b_ref, moda_ref, modb_ref,
                    wsgu_ref, wsd_ref, gpost2_ref, ys_hbm, outa_ref, outb_ref,
                    p_scr, rk_scr, g_scr, ys_scr, routed_scr, sems, *, n_steps, n_steps_a):
    step = pl.program_id(0)
    parity = lax.rem(step, 2)
    from_a = step < n_steps_a
    g_scr[...] = jnp.where(from_a, ga_ref[...], gb_ref[...])

    def seg_copy(blk, s):
        def make(e, off, n):
            src = pl.multiple_of(goff_ref[blk * N_EXPERTS + e], SEG_ALIGN)
            return pltpu.make_async_copy(ys_hbm.at[pl.ds(src, n)], ys_scr.at[s, pl.ds(off, n)], sems.at[s])
        return make

    def start_step(st, par):
        for j in range(STEP_BLOCKS):
            _start_segments(cntp_ref, loff_ref, st * STEP_BLOCKS + j,
                            seg_copy(st * STEP_BLOCKS + j, par * STEP_BLOCKS + j))

    @pl.when(step == 0)
    def _():
        ys_scr[...] = jnp.zeros_like(ys_scr)
        start_step(0, 0)

    @pl.when(step + 1 < n_steps)
    def _():
        start_step(step + 1, 1 - parity)

    for j in range(STEP_BLOCKS):
        blk = step * STEP_BLOCKS + j
        slot = parity * STEP_BLOCKS + j
        tok = slice(j * MOE_BLOCK, (j + 1) * MOE_BLOCK)
        _fill_permutation(p_scr, rk_scr, g_scr.at[:, tok], cntp_ref, loff_ref, long_ref, blk,
                          weighted=True)
        n_rows = pl.multiple_of(rows_ref[blk], SEG_ALIGN)
        pltpu.make_async_copy(ys_hbm.at[pl.ds(0, n_rows)], ys_scr.at[slot, pl.ds(0, n_rows)],
                              sems.at[slot]).wait()

        def unsort_rows(rows):
            return lax.dot_general(p_scr[rows, :], ys_scr[slot, rows, :], TN_DIMS,
                                   preferred_element_type=F32)

        routed_scr[tok, :] = unsort_rows(slice(0, MAIN_ROWS))

        @pl.when(rows_ref[blk] > MAIN_ROWS)
        def _():
            routed_scr[tok, :] += unsort_rows(slice(MAIN_ROWS, BLOCK_ROWS))

    hx = jnp.where(from_a, h2a_ref[...], h2b_ref[...])
    sgu = jnp.dot(hx, wsgu_ref[...], preferred_element_type=F32)
    shid = _silu(sgu[:, 0:SHARED_FF]) * sgu[:, SHARED_FF:2 * SHARED_FF]
    f = routed_scr[...] + jnp.dot(shid.astype(BF16), wsd_ref[...], preferred_element_type=F32)
    routed_scr[...] = _rms(f, gpost2_ref[...])

    def finish(x1_ref, mod_ref, out_ref):
        gate2 = mod_ref[:, 5:6, :]
        out_ref[...] = x1_ref[...] + gate2 * routed_scr[...].reshape(x1_ref.shape)

    @pl.when(from_a)
    def _():
        finish(x1a_ref, moda_ref, outa_ref)

    @pl.when(jnp.logical_not(from_a))
    def _():
        finish(x1b_ref, modb_ref, outb_ref)


def _segment_plan(cnt, n_tiles_max, ffn_tile):
    cntp = jnp.maximum((cnt + SEG_ALIGN - 1) // SEG_ALIGN, 1) * SEG_ALIGN
    loff = jnp.cumsum(cntp, axis=1) - cntp
    block_rows = jnp.sum(cntp, axis=1)
    has_long = jnp.max(cntp, axis=1) > WIN
    tot = jnp.sum(cntp, axis=0)
    reg = (tot + ffn_tile - 1) // ffn_tile * ffn_tile
    base = jnp.cumsum(reg) - reg
    goff = base[None, :] + jnp.cumsum(cntp, axis=0) - cntp
    tile_end = jnp.cumsum(reg // ffn_tile)
    tile = jnp.arange(n_tiles_max, dtype=jnp.int32)
    tile_expert = jnp.minimum(jnp.sum(tile_end[None, :] <= tile[:, None], axis=1), N_EXPERTS - 1)
    i32 = lambda a: a.astype(jnp.int32)
    block_tables = (i32(cntp.reshape(-1)), i32(loff.reshape(-1)), i32(goff.reshape(-1)),
                    i32(block_rows), i32(has_long))
    return block_tables, i32(base + tot), i32(reg - tot), i32(tile_expert), i32(tile_end[-1:])


def _stream_specs(x1, block_of):
    n_seq, seq_len, _ = x1.shape
    if seq_len < STEP_TOKENS:
        ns, ls = STEP_TOKENS // seq_len, seq_len
        x_map = lambda b, *_: (block_of(b), 0, 0)
        mod_map = x_map
    else:
        ns, ls = 1, STEP_TOKENS
        per_seq = seq_len // STEP_TOKENS
        x_map = lambda b, *_: (block_of(b) // per_seq, block_of(b) % per_seq, 0)
        mod_map = lambda b, *_: (block_of(b) // per_seq, 0, 0)
    return pl.BlockSpec((ns, ls, D_MODEL), x_map), pl.BlockSpec((ns, MOD_ROWS, D_MODEL), mod_map)


def _moe(stream_a, stream_b, w):
    (h2_a, g_a, x1_a, mod_a, cnt_a), (h2_b, g_b, x1_b, mod_b, cnt_b) = stream_a, stream_b
    nb_a, nb_b = h2_a.shape[0] // MOE_BLOCK, h2_b.shape[0] // MOE_BLOCK
    n_blocks = nb_a + nb_b
    n_tok = n_blocks * MOE_BLOCK
    ffn_tile = FFN_TILE
    rows_max = (n_tok * TOP_K + n_blocks * N_EXPERTS * SEG_ALIGN
                + N_EXPERTS * (ffn_tile - SEG_ALIGN))
    n_tiles_max = 2 * -(-rows_max // (2 * ffn_tile))
    rows_alloc = n_tiles_max * ffn_tile
    params = pltpu.CompilerParams(dimension_semantics=("arbitrary",), vmem_limit_bytes=VMEM_LIMIT)

    cnt = jnp.concatenate([cnt_a, cnt_b], axis=0)
    block_tables, tail_off, tail_len, tile_expert, n_used = _segment_plan(cnt, n_tiles_max, ffn_tile)

    assert nb_a % STEP_BLOCKS == 0 and nb_b % STEP_BLOCKS == 0
    steps_a, n_steps = nb_a // STEP_BLOCKS, n_blocks // STEP_BLOCKS
    in_a = lambda s: jnp.minimum(s, steps_a - 1)
    in_b = lambda s: jnp.maximum(s - steps_a, 0)
    tok_specs = [pl.BlockSpec((N_EXPERTS, STEP_TOKENS), lambda s, *_: (0, in_a(s))),
                 pl.BlockSpec((N_EXPERTS, STEP_TOKENS), lambda s, *_: (0, in_b(s))),
                 pl.BlockSpec((STEP_TOKENS, D_MODEL), lambda s, *_: (in_a(s), 0)),
                 pl.BlockSpec((STEP_TOKENS, D_MODEL), lambda s, *_: (in_b(s), 0))]
    any_spec = pl.BlockSpec(memory_space=pl.ANY)
    perm_scratch = [pltpu.VMEM((BLOCK_ROWS, MOE_BLOCK), BF16), pltpu.VMEM((N_EXPERTS, MOE_BLOCK), F32),
                    pltpu.VMEM((N_EXPERTS, STEP_TOKENS), F32)]
    row_buffers = pltpu.VMEM((2 * STEP_BLOCKS, BLOCK_ROWS, D_MODEL), BF16)

    x_sorted = pl.pallas_call(
        functools.partial(_dispatch_kernel, n_steps=n_steps, n_steps_a=steps_a),
        grid_spec=pltpu.PrefetchScalarGridSpec(
            num_scalar_prefetch=7, grid=(n_steps,),
            in_specs=tok_specs, out_specs=any_spec,
            scratch_shapes=perm_scratch + [
                pltpu.VMEM((STEP_TOKENS, D_MODEL), BF16),
                row_buffers,
                pltpu.VMEM((ffn_tile, D_MODEL), BF16),
                pltpu.SemaphoreType.DMA((2 * STEP_BLOCKS,)), pltpu.SemaphoreType.DMA(())]),
        out_shape=jax.ShapeDtypeStruct((rows_alloc, D_MODEL), BF16),
        compiler_params=params,
        name="moe_dispatch",
    )(*block_tables, tail_off, tail_len, g_a, g_b, h2_a, h2_b)

    last_used = lambda i, nu: jnp.maximum(jnp.minimum(i, (nu[0] - 1) // 2), 0)
    row_map = lambda i, te, nu: (last_used(i, nu), 0)
    w_first = lambda i, te, nu: (te[2 * last_used(i, nu)], 0, 0)
    w_second = lambda i, te, nu: (te[2 * last_used(i, nu) + 1], 0, 0)
    gate_up, down = (1, D_MODEL, EXPERT_FF), (1, EXPERT_FF, D_MODEL)
    y_sorted = pl.pallas_call(
        _ffn_kernel,
        grid_spec=pltpu.PrefetchScalarGridSpec(
            num_scalar_prefetch=2, grid=(n_tiles_max // 2,),
            in_specs=[pl.BlockSpec((2 * ffn_tile, D_MODEL), row_map),
                      pl.BlockSpec(gate_up, w_first), pl.BlockSpec(gate_up, w_first),
                      pl.BlockSpec(down, w_first),
                      pl.BlockSpec(gate_up, w_second), pl.BlockSpec(gate_up, w_second),
                      pl.BlockSpec(down, w_second)],
            out_specs=pl.BlockSpec((2 * ffn_tile, D_MODEL), row_map),
            scratch_shapes=[pltpu.VMEM((2, D_MODEL, 2 * EXPERT_FF), BF16),
                            pltpu.VMEM((2, EXPERT_FF, D_MODEL), BF16),
                            pltpu.SMEM((2,), jnp.int32)]),
        out_shape=jax.ShapeDtypeStruct((rows_alloc, D_MODEL), BF16),
        compiler_params=params,
        name="moe_ffn",
    )(tile_expert, n_used, x_sorted, w["w_e_gate"], w["w_e_up"], w["w_e_down"],
      w["w_e_gate"], w["w_e_up"], w["w_e_down"])

    full2 = lambda b, *_: (0, 0)
    x_spec_a, mod_spec_a = _stream_specs(x1_a, in_a)
    x_spec_b, mod_spec_b = _stream_specs(x1_b, in_b)
    return pl.pallas_call(
        functools.partial(_combine_kernel, n_steps=n_steps, n_steps_a=steps_a),
        grid_spec=pltpu.PrefetchScalarGridSpec(
            num_scalar_prefetch=5, grid=(n_steps,),
            in_specs=tok_specs + [
                x_spec_a, x_spec_b, mod_spec_a, mod_spec_b,
                pl.BlockSpec((D_MODEL, 2 * SHARED_FF), full2),
                pl.BlockSpec((SHARED_FF, D_MODEL), full2),
                pl.BlockSpec((1, D_MODEL), full2),
                any_spec],
            out_specs=[x_spec_a, x_spec_b],
            scratch_shapes=perm_scratch + [
                row_buffers,
                pltpu.VMEM((STEP_TOKENS, D_MODEL), F32),
                pltpu.SemaphoreType.DMA((2 * STEP_BLOCKS,))]),
        out_shape=[jax.ShapeDtypeStruct(x1_a.shape, F32), jax.ShapeDtypeStruct(x1_b.shape, F32)],
        compiler_params=params,
        name="moe_combine",
    )(*block_tables, g_a, g_b, h2_a, h2_b, x1_a, x1_b, mod_a, mod_b,
      w["w_sh_gu"], w["w_sh_down"], w["g_post2"], y_sorted)


def _prep_weights(g_pre1, g_post1, w_in, w_gk_up, b_gk, g_gmlp_v, w_s, b_s, g_gla_o, w_out, g_pre2,
                  g_post2, w_router, b_router, w_e_gate, w_e_up, w_e_down, w_sh_gate, w_sh_up,
                  w_sh_down):
    row = lambda v: v.reshape(1, -1)
    wr = jnp.pad(w_router, ((0, 0), (0, LANES - N_EXPERTS)))
    wr_top = lax.bitcast_convert_type(
        lax.bitcast_convert_type(wr, jnp.uint32) & jnp.uint32(0xFFFF0000), F32)
    wr_hi = wr_top.astype(BF16)
    wr_lo = (wr - wr_top).astype(BF16)
    wr_split = jnp.concatenate([jnp.concatenate([wr_hi, wr_lo], axis=1),
                                jnp.concatenate([wr_hi, jnp.zeros_like(wr_lo)], axis=1)], axis=0)
    return {
        "g_pre1": row(g_pre1), "g_post1": row(g_post1), "g_pre2": row(g_pre2), "g_post2": row(g_post2),
        "w_in": jnp.concatenate([w_in[:, :OFF_Z], _fold_gate(w_in[:, OFF_Z:], w_gk_up)],
                                axis=1).astype(BF16),
        "b_gk": row(b_gk),
        "g_gmlp_v": row(g_gmlp_v),
        "w_s": w_s,
        "b_s_full": jnp.repeat(b_s.T, GMLP_HEAD, axis=1),
        "g_gla_o": g_gla_o,
        "w_out": w_out.astype(BF16),
        "w_router": wr_split,
        "b_router": b_router.reshape(N_EXPERTS, 1),
        "w_e_gate": w_e_gate, "w_e_up": w_e_up, "w_e_down": w_e_down,
        "w_sh_gu": jnp.concatenate([w_sh_gate, w_sh_up], axis=-1).astype(BF16),
        "w_sh_down": w_sh_down.astype(BF16),
    }


PROMPT_TILE = 1024
SAMPLE_TILE = 256


def _mix(x, mod, s0, w, *, emit_v):
    n_seq, seq_len, _ = x.shape
    s0_t = jnp.swapaxes(s0, -1, -2)
    if seq_len >= GMLP_CHUNK:
        outs = _mixer(x, mod, s0_t, w, seq_tile=PROMPT_TILE, chunk=GMLP_CHUNK, emit_v=emit_v)
    else:
        outs = _mixer(x, mod, s0_t, w, seq_tile=SAMPLE_TILE, chunk=seq_len, emit_v=emit_v)
    x1, h2, gates_t, st, cnt = outs[:5]
    per_tile = (n_seq * seq_len // cnt.shape[0]) // MOE_BLOCK
    cnt = jnp.swapaxes(cnt[:, :, :per_tile], 1, 2).reshape(-1, N_EXPERTS).astype(jnp.int32)
    v = outs[5].reshape(n_seq, seq_len, GMLP_GROUPS, GMLP_HEAD) if emit_v else None
    return (h2, gates_t, x1, mod, cnt), jnp.swapaxes(st, -1, -2), v


def kernel(x_prompt, x_sample, state_gla, c_prompt, c_sample, w_ada, b_ada, g_pre1, g_post1, w_in, w_gk_up, b_gk, g_gmlp_v, w_s, b_s, g_gla_o, w_out, g_pre2, g_post2, w_router, b_router, w_e_gate, w_e_up, w_e_down, w_sh_gate, w_sh_up, w_sh_down):
    depth = w_ada.shape[0]
    n_p, n_s = x_prompt.shape[0], x_sample.shape[0]
    c_all = jnp.concatenate([c_prompt, c_sample], axis=0)
    x_p, x_s = x_prompt, x_sample
    sp_list, ss_list, vs_list = [], [], []
    for l in range(depth):
        w = _prep_weights(g_pre1[l], g_post1[l], w_in[l], w_gk_up[l], b_gk[l], g_gmlp_v[l], w_s[l],
                          b_s[l], g_gla_o[l], w_out[l], g_pre2[l], g_post2[l], w_router[l],
                          b_router[l], w_e_gate[l], w_e_up[l], w_e_down[l], w_sh_gate[l], w_sh_up[l],
                          w_sh_down[l])
        mod = _ada(c_all, w_ada[l], b_ada[l])
        mod = jnp.pad(jnp.swapaxes(mod, 0, 1), ((0, 0), (0, MOD_ROWS - 6), (0, 0)))
        s0_p = jnp.zeros((n_p, GLA_HEADS, GLA_DK, GLA_DV), F32)
        moe_p, s_p, _ = _mix(x_p, mod[:n_p], s0_p, w, emit_v=False)
        moe_s, s_s, v_s = _mix(x_s, mod[n_p:], state_gla[l], w, emit_v=True)
        x_p, x_s = _moe(moe_p, moe_s, w)
        sp_list.append(s_p)
        ss_list.append(s_s)
        vs_list.append(v_s)
    return (x_p, x_s, jnp.stack(sp_list), jnp.stack(ss_list), jnp.stack(vs_list))
```

```python
import functools

import jax
import jax.numpy as jnp
from jax import lax
from jax.experimental import pallas as pl
from jax.experimental.pallas import tpu as pltpu

D_MODEL = 1024
GMLP_WIDTH = 512
GMLP_GROUPS = 4
GMLP_HEAD = 128
GMLP_CHUNK = 128
CAUSAL_BLOCK = 64
GLA_WIDTH = 512
GLA_HEADS = 4
GLA_DV = 128
GLA_DK = 64
GLA_KEY_WIDTH = 256
GLA_GATE_RANK = 16
GLA_GATE_NORMALIZER = 16.0
N_EXPERTS = 64
TOP_K = 8
N_EXPERT_GROUPS = 8
GROUP_SIZE = N_EXPERTS // N_EXPERT_GROUPS
TOPK_GROUPS = 4
EXPERT_FF = 256
SHARED_FF = 256
ROUTED_SCALE = 2.5
EPS = 1e-6

LANES = 128
GK_PAD = LANES
IN_WIDTH_PAD = 2 * GMLP_WIDTH + 2 * GLA_KEY_WIDTH + 2 * GLA_WIDTH + GLA_KEY_WIDTH
OFF_U = 0
OFF_VG = GMLP_WIDTH
OFF_Q = 2 * GMLP_WIDTH
OFF_K = OFF_Q + GLA_KEY_WIDTH
OFF_VL = OFF_K + GLA_KEY_WIDTH
OFF_R = OFF_VL + GLA_WIDTH
OFF_Z = OFF_R + GLA_WIDTH
MOD_ROWS = 8
VMEM_LIMIT = 56 * 1024 * 1024

F32 = jnp.float32
BF16 = jnp.bfloat16
NT_DIMS = (((1,), (1,)), ((), ()))
TN_DIMS = (((0,), (0,)), ((), ()))


def _rms(x, g):
    return x * lax.rsqrt(jnp.mean(x * x, axis=-1, keepdims=True) + EPS) * g


def _gelu(x):
    return 0.5 * x * (1.0 + jnp.tanh(0.7978845608028654 * (x + 0.044715 * (x * x * x))))


def _sigmoid(x):
    return 1.0 / (1.0 + jnp.exp(-x))


def _silu(x):
    return x * _sigmoid(x)


def _ada_kernel(c_ref, w_ref, b_ref, o_ref):
    a = _silu(c_ref[...])
    o_ref[0] = jnp.dot(a, w_ref[...], precision=lax.Precision.HIGHEST,
                       preferred_element_type=F32) + b_ref[0]


def _ada(c_all, w_ada, b_ada):
    n = c_all.shape[0]
    return pl.pallas_call(
        _ada_kernel,
        grid=(6,),
        in_specs=[pl.BlockSpec((n, D_MODEL), lambda j: (0, 0)),
                  pl.BlockSpec((D_MODEL, D_MODEL), lambda j: (0, j)),
                  pl.BlockSpec((1, 1, D_MODEL), lambda j: (j, 0, 0))],
        out_specs=pl.BlockSpec((1, n, D_MODEL), lambda j: (j, 0, 0)),
        out_shape=jax.ShapeDtypeStruct((6, n, D_MODEL), F32),
        compiler_params=pltpu.CompilerParams(vmem_limit_bytes=VMEM_LIMIT),
        name="ada",
    )(c_all, w_ada, b_ada.reshape(6, 1, D_MODEL))


def _fold_gate_kernel(a_ref, b_ref, o_ref):
    o_ref[...] = jnp.dot(a_ref[...], b_ref[...], precision=lax.Precision.HIGHEST,
                         preferred_element_type=F32)


def _fold_gate(w_in_gate, w_gk_up):
    a = jnp.pad(w_in_gate, ((0, 0), (0, GK_PAD - GLA_GATE_RANK)))
    b = jnp.pad(w_gk_up, ((0, GK_PAD - GLA_GATE_RANK), (0, 0)))
    return pl.pallas_call(
        _fold_gate_kernel,
        out_shape=jax.ShapeDtypeStruct((D_MODEL, GLA_KEY_WIDTH), F32),
        name="fold_gate",
    )(a, b)


def _route(logits_t, bias_t):
    t = logits_t.shape[1]
    scores = _sigmoid(logits_t)
    sel = scores + bias_t
    sub = lax.broadcasted_iota(jnp.int32, (GROUP_SIZE, t), 0)
    gscore = []
    for g in range(N_EXPERT_GROUPS):
        blk = sel[g * GROUP_SIZE:(g + 1) * GROUP_SIZE]
        m1 = jnp.max(blk, axis=0, keepdims=True)
        first = jnp.min(jnp.where(blk == m1, sub, GROUP_SIZE), axis=0, keepdims=True)
        m2 = jnp.max(jnp.where(sub == first, -jnp.inf, blk), axis=0, keepdims=True)
        gscore.append(m1 + m2)
    neg = jnp.full((GROUP_SIZE, t), -jnp.inf, F32)
    masked = []
    for g in range(N_EXPERT_GROUPS):
        rank = jnp.zeros((1, t), jnp.int32)
        for o in range(N_EXPERT_GROUPS):
            if o == g:
                continue
            ahead = (gscore[o] >= gscore[g]) if o < g else (gscore[o] > gscore[g])
            rank = rank + jnp.where(ahead, 1, 0)
        keep = jnp.broadcast_to(rank < TOPK_GROUPS, (GROUP_SIZE, t))
        masked.append(jnp.where(keep, sel[g * GROUP_SIZE:(g + 1) * GROUP_SIZE], neg))
    selm = jnp.concatenate(masked, axis=0)
    eidx = lax.broadcasted_iota(jnp.int32, (N_EXPERTS, t), 0)
    picked = jnp.zeros((N_EXPERTS, t), F32)
    for _ in range(TOP_K):
        best = jnp.max(selm, axis=0, keepdims=True)
        first = jnp.min(jnp.where(selm == best, eidx, N_EXPERTS), axis=0, keepdims=True)
        hit = eidx == first
        picked = jnp.where(hit, 1.0, picked)
        selm = jnp.where(hit, -jnp.inf, selm)
    chosen = jnp.where(picked > 0.0, scores, 0.0)
    denom = jnp.sum(chosen, axis=0, keepdims=True)
    return chosen * (ROUTED_SCALE / denom)


def _mixer_kernel(x_ref, mod_ref, s0_ref, gpre1_ref, win_ref, bgk_ref, ggv_ref,
                  ws_ref, bsf_ref, ggo_ref, wout_ref, gpost1_ref, gpre2_ref, wr_ref, br_ref,
                  x1_ref, h2_ref, gates_ref, st_ref, *rest,
                  chunk, n_chunks, chunk_is_seq, emit_v):
    if emit_v:
        v_ref, *rest = rest
    proj, mix_scr = rest
    ns, ls, _ = x_ref.shape
    tm = ns * ls

    mod_row = lambda i: mod_ref[:, i:i + 1, :]

    h = _rms(x_ref[...], gpre1_ref[...]) * (1.0 + mod_row(1)) + mod_row(0)
    proj[...] = jnp.dot(h.reshape(tm, D_MODEL).astype(BF16), win_ref[...],
                        preferred_element_type=F32)

    if chunk_is_seq:
        st_ref[...] = s0_ref[...]
    else:
        @pl.when(pl.program_id(1) == 0)
        def _():
            st_ref[...] = s0_ref[...]

    row = lax.broadcasted_iota(jnp.int32, (chunk, chunk), 0)
    col = lax.broadcasted_iota(jnp.int32, (chunk, chunk), 1)
    tri = jnp.where(row >= col, 1.0, 0.0).astype(BF16)
    tri2 = jnp.concatenate([tri, tri], axis=1)
    causal = row >= col
    block_causal = (row // CAUSAL_BLOCK) >= (col // CAUSAL_BLOCK)
    wmix = [jnp.where(block_causal, ws_ref[g, 0:chunk, 0:chunk], 0.0).astype(BF16)
            for g in range(GMLP_GROUPS)]
    mid = chunk // 2

    for c in range(n_chunks):
        rows = slice(c * chunk, (c + 1) * chunk)
        z = proj[rows, OFF_Z:OFF_Z + GLA_KEY_WIDTH] + bgk_ref[...]
        la = (jnp.minimum(z, 0.0) - jnp.log(1.0 + jnp.exp(-jnp.abs(z)))) * (1.0 / GLA_GATE_NORMALIZER)
        la_hi = la.astype(BF16)
        la_lo = (la - la_hi.astype(F32)).astype(BF16)
        proj[rows, OFF_Z:OFF_Z + GLA_KEY_WIDTH] = jnp.dot(
            tri2, jnp.concatenate([la_hi, la_lo], axis=0), preferred_element_type=F32)

    def chunk_body(c, carry):
        rows = pl.ds(pl.multiple_of(c * chunk, chunk), chunk)
        sidx = c if chunk_is_seq else 0

        vg = _gelu(proj[rows, OFF_VG:OFF_VG + GMLP_WIDTH])
        vn = _rms(vg, ggv_ref[...])
        if emit_v:
            v_ref[rows, :] = vn
        vnb = vn.astype(BF16)
        mixed = jnp.concatenate(
            [jnp.dot(wmix[g], vnb[:, g * GMLP_HEAD:(g + 1) * GMLP_HEAD], preferred_element_type=F32)
             for g in range(GMLP_GROUPS)], axis=-1) + bsf_ref[0:chunk, :]
        u = _gelu(proj[rows, OFF_U:OFF_U + GMLP_WIDTH])
        mix_scr[rows, 0:GMLP_WIDTH] = (u * mixed).astype(BF16)

        q = proj[rows, OFF_Q:OFF_Q + GLA_KEY_WIDTH] * (GLA_DK ** -0.5)
        k = proj[rows, OFF_K:OFF_K + GLA_KEY_WIDTH]
        b = proj[rows, OFF_Z:OFF_Z + GLA_KEY_WIDTH]
        b_mid = b[mid - 1:mid]
        b_end = b[chunk - 1:chunk]
        qt = (q * jnp.exp(b - b_mid)).astype(BF16)
        kt = (k * jnp.exp(b_mid - b)).astype(BF16)
        qs = (q * jnp.exp(b)).astype(BF16)
        kd = (k * jnp.exp(b_end - b)).astype(BF16)
        decay = jnp.exp(b_end)
        for hd in range(GLA_HEADS):
            ksl = slice(hd * GLA_DK, (hd + 1) * GLA_DK)
            vsl = slice(OFF_VL + hd * GLA_DV, OFF_VL + (hd + 1) * GLA_DV)
            rsl = slice(OFF_R + hd * GLA_DV, OFF_R + (hd + 1) * GLA_DV)
            att = lax.dot_general(qt[:, ksl], kt[:, ksl], NT_DIMS, preferred_element_type=F32)
            att = jnp.where(causal, att, 0.0).astype(BF16)
            vh = proj[rows, vsl].astype(BF16)
            st = st_ref[sidx, hd]
            o = jnp.dot(att, vh, preferred_element_type=F32) + lax.dot_general(
                qs[:, ksl], st.astype(BF16), NT_DIMS, preferred_element_type=F32)
            on = _rms(o, ggo_ref[hd:hd + 1, :])
            mix_scr[rows, GMLP_WIDTH + hd * GLA_DV:GMLP_WIDTH + (hd + 1) * GLA_DV] = (
                on * _silu(proj[rows, rsl])).astype(BF16)
            st_ref[sidx, hd] = st * decay[:, ksl] + lax.dot_general(
                vh, kd[:, ksl], TN_DIMS, preferred_element_type=F32)
        return carry

    lax.fori_loop(0, n_chunks, chunk_body, 0, unroll=4)

    mixo = jnp.dot(mix_scr[...], wout_ref[...], preferred_element_type=F32).reshape(ns, ls, D_MODEL)
    x1 = x_ref[...] + mod_row(2) * _rms(mixo, gpost1_ref[...])
    x1_ref[...] = x1
    h2 = (_rms(x1, gpre2_ref[...]) * (1.0 + mod_row(4)) + mod_row(3)).reshape(tm, D_MODEL)
    h2_hi = h2.astype(BF16)
    h2_ref[...] = h2_hi

    h2_lo = (h2 - h2_hi.astype(F32)).astype(BF16)
    parts = jnp.dot(jnp.concatenate([h2_hi, h2_lo], axis=-1), wr_ref[...], preferred_element_type=F32)
    logits = parts[:, 0:LANES] + parts[:, LANES:2 * LANES]
    gates_ref[...] = _route(logits.T[0:N_EXPERTS], br_ref[...])


def _mixer(x, mod, s0_t, w, *, seq_tile, chunk, emit_v):
    n_seq, seq_len, _ = x.shape
    chunk_is_seq = seq_len == chunk
    if chunk_is_seq:
        ns, ls = seq_tile // chunk, chunk
        grid = (1, n_seq // ns)
        x_map = lambda b, t: (t, 0, 0)
        seq_map3 = lambda b, t: (t, 0, 0)
        seq_map4 = lambda b, t: (t, 0, 0, 0)
    else:
        ns, ls = 1, seq_tile
        grid = (n_seq, seq_len // seq_tile)
        x_map = lambda b, t: (b, t, 0)
        seq_map3 = lambda b, t: (b, 0, 0)
        seq_map4 = lambda b, t: (b, 0, 0, 0)
    tm = ns * ls
    n_tok = n_seq * seq_len
    tok_block = lambda b, t: b * grid[1] + t
    tok_map = lambda b, t: (tok_block(b, t), 0)
    full2 = lambda b, t: (0, 0)
    full3 = lambda b, t: (0, 0, 0)

    in_specs = [
        pl.BlockSpec((ns, ls, D_MODEL), x_map),
        pl.BlockSpec((ns, MOD_ROWS, D_MODEL), seq_map3),
        pl.BlockSpec((ns, GLA_HEADS, GLA_DV, GLA_DK), seq_map4),
        pl.BlockSpec((1, D_MODEL), full2),
        pl.BlockSpec((D_MODEL, IN_WIDTH_PAD), full2),
        pl.BlockSpec((1, GLA_KEY_WIDTH), full2),
        pl.BlockSpec((1, GMLP_WIDTH), full2),
        pl.BlockSpec((GMLP_GROUPS, GMLP_CHUNK, GMLP_CHUNK), full3),
        pl.BlockSpec((GMLP_CHUNK, GMLP_WIDTH), full2),
        pl.BlockSpec((GLA_HEADS, GLA_DV), full2),
        pl.BlockSpec((D_MODEL, D_MODEL), full2),
        pl.BlockSpec((1, D_MODEL), full2),
        pl.BlockSpec((1, D_MODEL), full2),
        pl.BlockSpec((2 * D_MODEL, 2 * LANES), full2),
        pl.BlockSpec((N_EXPERTS, 1), full2),
    ]
    out_specs = [
        pl.BlockSpec((ns, ls, D_MODEL), x_map),
        pl.BlockSpec((tm, D_MODEL), tok_map),
        pl.BlockSpec((N_EXPERTS, tm), lambda b, t: (0, tok_block(b, t))),
        pl.BlockSpec((ns, GLA_HEADS, GLA_DV, GLA_DK), seq_map4),
    ]
    out_shape = [
        jax.ShapeDtypeStruct((n_seq, seq_len, D_MODEL), F32),
        jax.ShapeDtypeStruct((n_tok, D_MODEL), BF16),
        jax.ShapeDtypeStruct((N_EXPERTS, n_tok), F32),
        jax.ShapeDtypeStruct((n_seq, GLA_HEADS, GLA_DV, GLA_DK), F32),
    ]
    if emit_v:
        out_specs.append(pl.BlockSpec((tm, GMLP_WIDTH), tok_map))
        out_shape.append(jax.ShapeDtypeStruct((n_tok, GMLP_WIDTH), F32))

    kern = functools.partial(_mixer_kernel, chunk=chunk, n_chunks=tm // chunk,
                             chunk_is_seq=chunk_is_seq, emit_v=emit_v)
    return pl.pallas_call(
        kern,
        grid=grid,
        in_specs=in_specs,
        out_specs=out_specs,
        out_shape=out_shape,
        scratch_shapes=[pltpu.VMEM((tm, IN_WIDTH_PAD), F32), pltpu.VMEM((tm, D_MODEL), BF16)],
        compiler_params=pltpu.CompilerParams(
            dimension_semantics=("arbitrary", "arbitrary"), vmem_limit_bytes=VMEM_LIMIT),
        name="mixer_sample" if chunk_is_seq else "mixer_prompt",
    )(x, mod, s0_t, w["g_pre1"], w["w_in"], w["b_gk"], w["g_gmlp_v"], w["w_s"],
      w["b_s_full"], w["g_gla_o"], w["w_out"], w["g_post1"], w["g_pre2"], w["w_router"],
      w["b_router"])


MOE_BLOCK = 256
SEG_ALIGN = 16
WIN = 64
FFN_TILE = 1280
BLOCK_ROWS = MOE_BLOCK * TOP_K + SEG_ALIGN * N_EXPERTS + WIN
MAIN_ROWS = 2688
STEP_BLOCKS = 2
STEP_TOKENS = STEP_BLOCKS * MOE_BLOCK
COL_CHUNK = 256
COUNT_LANES = LANES


def _count_kernel(g_ref, o_ref):
    n = g_ref.shape[1]
    selb = jnp.where(g_ref[...] > 0.0, 1.0, 0.0).astype(BF16)
    tok = lax.broadcasted_iota(jnp.int32, (n, COUNT_LANES), 0)
    lane = lax.broadcasted_iota(jnp.int32, (n, COUNT_LANES), 1)
    ind = jnp.where(tok // MOE_BLOCK == lane, 1.0, 0.0).astype(BF16)
    o_ref[0] = jnp.dot(selb, ind, preferred_element_type=F32)


def _fill_permutation(p_scr, rk_scr, g_ref, cntp_ref, loff_ref, long_ref, blk, weighted):
    tb = g_ref.shape[1]
    g = g_ref[...]
    sel = g > 0.0
    r_i = lax.broadcasted_iota(jnp.int32, (tb, tb), 0)
    c_i = lax.broadcasted_iota(jnp.int32, (tb, tb), 1)
    earlier = jnp.where(r_i < c_i, 1.0, 0.0).astype(BF16)
    rank = jnp.dot(jnp.where(sel, 1.0, 0.0).astype(BF16), earlier, preferred_element_type=F32)
    rk = jnp.where(sel, rank, -1.0)
    p_scr[...] = jnp.zeros_like(p_scr)
    jwin = lax.broadcasted_iota(jnp.int32, (WIN, tb), 0).astype(F32)

    for e in range(N_EXPERTS):
        off = pl.multiple_of(loff_ref[blk * N_EXPERTS + e], SEG_ALIGN)
        val = g[e:e + 1] if weighted else 1.0
        p_scr[pl.ds(off, WIN), :] = jnp.where(rk[e:e + 1] == jwin, val, 0.0).astype(BF16)

    @pl.when(long_ref[blk] > 0)
    def _():
        rk_scr[...] = rk

        def expert_body(e, c):
            n = cntp_ref[blk * N_EXPERTS + e]
            off = loff_ref[blk * N_EXPERTS + e]
            rke = rk_scr[pl.ds(e, 1), :]
            val = g_ref[pl.ds(e, 1), :] if weighted else 1.0

            def win_body(wi, c2):
                rows = pl.ds(pl.multiple_of(off + wi * WIN, SEG_ALIGN), WIN)
                j = jwin + (wi * WIN).astype(F32)
                new = jnp.where(rke == j, val, 0.0)
                p_scr[rows, :] = jnp.where(j < n.astype(F32), new, p_scr[rows, :].astype(F32)).astype(BF16)
                return c2

            lax.fori_loop(1, (n + WIN - 1) // WIN, win_body, 0)
            return c

        lax.fori_loop(0, N_EXPERTS, expert_body, 0)


def _start_segments(cntp_ref, loff_ref, blk, make_copy):
    for e in range(N_EXPERTS):
        make_copy(e, pl.multiple_of(loff_ref[blk * N_EXPERTS + e], SEG_ALIGN),
                  pl.multiple_of(cntp_ref[blk * N_EXPERTS + e], SEG_ALIGN)).start()


def _dispatch_kernel(cntp_ref, loff_ref, goff_ref, rows_ref, long_ref, toff_ref, tlen_ref,
                     ga_ref, gb_ref, h2a_ref, h2b_ref, xs_hbm,
                     p_scr, rk_scr, g_scr, h2_scr, xs_scr, zero_scr, sems, tail_sem,
                     *, n_steps, n_steps_a):
    step = pl.program_id(0)
    parity = lax.rem(step, 2)
    from_a = step < n_steps_a
    g_scr[...] = jnp.where(from_a, ga_ref[...], gb_ref[...])
    h2_scr[...] = jnp.where(from_a, h2a_ref[...], h2b_ref[...])

    def seg_copy(blk, s):
        def make(e, off, n):
            dst = pl.multiple_of(goff_ref[blk * N_EXPERTS + e], SEG_ALIGN)
            return pltpu.make_async_copy(xs_scr.at[s, pl.ds(off, n)], xs_hbm.at[pl.ds(dst, n)], sems.at[s])
        return make

    def wait_block(blk, s):
        n = pl.multiple_of(rows_ref[blk], SEG_ALIGN)
        pltpu.make_async_copy(xs_scr.at[s, pl.ds(0, n)], xs_hbm.at[pl.ds(0, n)], sems.at[s]).wait()

    def for_tails(action):
        def body(e, c):
            n = tlen_ref[e]

            @pl.when(n > 0)
            def _():
                dst = pl.multiple_of(toff_ref[e], SEG_ALIGN)
                nn = pl.multiple_of(n, SEG_ALIGN)
                action(pltpu.make_async_copy(zero_scr.at[pl.ds(0, nn)], xs_hbm.at[pl.ds(dst, nn)], tail_sem))

            return c

        lax.fori_loop(0, N_EXPERTS, body, 0)

    @pl.when(step == 0)
    def _():
        zero_scr[...] = jnp.zeros_like(zero_scr)
        for_tails(lambda cp: cp.start())

    for j in range(STEP_BLOCKS):
        blk = step * STEP_BLOCKS + j
        slot = parity * STEP_BLOCKS + j
        tok = slice(j * MOE_BLOCK, (j + 1) * MOE_BLOCK)

        @pl.when(step >= 2)
        def _():
            wait_block(blk - 2 * STEP_BLOCKS, slot)

        _fill_permutation(p_scr, rk_scr, g_scr.at[:, tok], cntp_ref, loff_ref, long_ref, blk,
                          weighted=False)

        def sort_rows(rows):
            for c in range(D_MODEL // COL_CHUNK):
                cols = slice(c * COL_CHUNK, (c + 1) * COL_CHUNK)
                xs = jnp.dot(p_scr[rows, :], h2_scr[tok, cols], preferred_element_type=F32)
                xs_scr[slot, rows, cols] = xs.astype(BF16)

        sort_rows(slice(0, MAIN_ROWS))

        @pl.when(rows_ref[blk] > MAIN_ROWS)
        def _():
            sort_rows(slice(MAIN_ROWS, BLOCK_ROWS))

        _start_segments(cntp_ref, loff_ref, blk, seg_copy(blk, slot))

    @pl.when(step == n_steps - 1)
    def _():
        for_tails(lambda cp: cp.wait())
        for j in range(STEP_BLOCKS):
            blk = step * STEP_BLOCKS + j
            if n_steps >= 2:
                wait_block(blk - STEP_BLOCKS, (1 - parity) * STEP_BLOCKS + j)
            wait_block(blk, parity * STEP_BLOCKS + j)


def _ffn_kernel(te_ref, nu_ref, x_ref, wga_ref, wua_ref, wda_ref, wgb_ref, wub_ref, wdb_ref, y_ref,
                wgu_scr, wd_scr, held_ref):
    i = pl.program_id(0)
    tile = x_ref.shape[0] // 2
    first, second = 2 * i, 2 * i + 1
    e_first, e_second = te_ref[first], te_ref[second]
    one_expert = jnp.logical_and(second < nu_ref[0], e_second == e_first)
    two_experts = jnp.logical_and(second < nu_ref[0], e_second != e_first)

    @pl.when(i == 0)
    def _():
        held_ref[0] = -1
        held_ref[1] = -1

    def hold(k, expert, wg_ref, wu_ref, wd_ref):
        @pl.when(held_ref[k] != expert)
        def _():
            wgu_scr[k, :, 0:EXPERT_FF] = wg_ref[0].astype(BF16)
            wgu_scr[k, :, EXPERT_FF:2 * EXPERT_FF] = wu_ref[0].astype(BF16)
            wd_scr[k] = wd_ref[0].astype(BF16)
            held_ref[k] = expert

    def ffn(rows, k):
        gu = jnp.dot(x_ref[rows, :], wgu_scr[k], preferred_element_type=F32)
        hid = _silu(gu[:, 0:EXPERT_FF]) * gu[:, EXPERT_FF:2 * EXPERT_FF]
        y_ref[rows, :] = jnp.dot(hid.astype(BF16), wd_scr[k], preferred_element_type=F32).astype(BF16)

    @pl.when(first < nu_ref[0])
    def _():
        hold(0, e_first, wga_ref, wua_ref, wda_ref)

        @pl.when(one_expert)
        def _():
            ffn(slice(0, 2 * tile), 0)

        @pl.when(jnp.logical_not(one_expert))
        def _():
            ffn(slice(0, tile), 0)

        @pl.when(two_experts)
        def _():
            hold(1, e_second, wgb_ref, wub_ref, wdb_ref)
            ffn(slice(tile, 2 * tile), 1)


def _combine_kernel(cntp_ref, loff_ref, goff_ref, rows_ref, long_ref,
                    ga_ref, gb_ref, h2a_ref, h2b_ref, x1a_ref, x1b_ref, moda_ref, modb_ref,
                    wsgu_ref, wsd_ref, gpost2_ref, ys_hbm, outa_ref, outb_ref,
                    p_scr, rk_scr, g_scr, ys_scr, routed_scr, sems, *, n_steps, n_steps_a):
    step = pl.program_id(0)
    parity = lax.rem(step, 2)
    from_a = step < n_steps_a
    g_scr[...] = jnp.where(from_a, ga_ref[...], gb_ref[...])

    def seg_copy(blk, s):
        def make(e, off, n):
            src = pl.multiple_of(goff_ref[blk * N_EXPERTS + e], SEG_ALIGN)
            return pltpu.make_async_copy(ys_hbm.at[pl.ds(src, n)], ys_scr.at[s, pl.ds(off, n)], sems.at[s])
        return make

    def start_step(st, par):
        for j in range(STEP_BLOCKS):
            _start_segments(cntp_ref, loff_ref, st * STEP_BLOCKS + j,
                            seg_copy(st * STEP_BLOCKS + j, par * STEP_BLOCKS + j))

    @pl.when(step == 0)
    def _():
        ys_scr[...] = jnp.zeros_like(ys_scr)
        start_step(0, 0)

    @pl.when(step + 1 < n_steps)
    def _():
        start_step(step + 1, 1 - parity)

    for j in range(STEP_BLOCKS):
        blk = step * STEP_BLOCKS + j
        slot = parity * STEP_BLOCKS + j
        tok = slice(j * MOE_BLOCK, (j + 1) * MOE_BLOCK)
        _fill_permutation(p_scr, rk_scr, g_scr.at[:, tok], cntp_ref, loff_ref, long_ref, blk,
                          weighted=True)
        n_rows = pl.multiple_of(rows_ref[blk], SEG_ALIGN)
        pltpu.make_async_copy(ys_hbm.at[pl.ds(0, n_rows)], ys_scr.at[slot, pl.ds(0, n_rows)],
                              sems.at[slot]).wait()

        def unsort_rows(rows):
            return lax.dot_general(p_scr[rows, :], ys_scr[slot, rows, :], TN_DIMS,
                                   preferred_element_type=F32)

        routed_scr[tok, :] = unsort_rows(slice(0, MAIN_ROWS))

        @pl.when(rows_ref[blk] > MAIN_ROWS)
        def _():
            routed_scr[tok, :] += unsort_rows(slice(MAIN_ROWS, BLOCK_ROWS))

    hx = jnp.where(from_a, h2a_ref[...], h2b_ref[...])
    sgu = jnp.dot(hx, wsgu_ref[...], preferred_element_type=F32)
    shid = _silu(sgu[:, 0:SHARED_FF]) * sgu[:, SHARED_FF:2 * SHARED_FF]
    f = routed_scr[...] + jnp.dot(shid.astype(BF16), wsd_ref[...], preferred_element_type=F32)
    routed_scr[...] = _rms(f, gpost2_ref[...])

    def finish(x1_ref, mod_ref, out_ref):
        gate2 = mod_ref[:, 5:6, :]
        out_ref[...] = x1_ref[...] + gate2 * routed_scr[...].reshape(x1_ref.shape)

    @pl.when(from_a)
    def _():
        finish(x1a_ref, moda_ref, outa_ref)

    @pl.when(jnp.logical_not(from_a))
    def _():
        finish(x1b_ref, modb_ref, outb_ref)


def _segment_plan(cnt, n_tiles_max, ffn_tile):
    cntp = jnp.maximum((cnt + SEG_ALIGN - 1) // SEG_ALIGN, 1) * SEG_ALIGN
    loff = jnp.cumsum(cntp, axis=1) - cntp
    block_rows = jnp.sum(cntp, axis=1)
    has_long = jnp.max(cntp, axis=1) > WIN
    tot = jnp.sum(cntp, axis=0)
    reg = (tot + ffn_tile - 1) // ffn_tile * ffn_tile
    base = jnp.cumsum(reg) - reg
    goff = base[None, :] + jnp.cumsum(cntp, axis=0) - cntp
    tile_end = jnp.cumsum(reg // ffn_tile)
    tile = jnp.arange(n_tiles_max, dtype=jnp.int32)
    tile_expert = jnp.minimum(jnp.sum(tile_end[None, :] <= tile[:, None], axis=1), N_EXPERTS - 1)
    i32 = lambda a: a.astype(jnp.int32)
    block_tables = (i32(cntp.reshape(-1)), i32(loff.reshape(-1)), i32(goff.reshape(-1)),
                    i32(block_rows), i32(has_long))
    return block_tables, i32(base + tot), i32(reg - tot), i32(tile_expert), i32(tile_end[-1:])


def _count_blocks(gates_t, params):
    n_tok = gates_t.shape[1]
    chunk = min(n_tok, COUNT_LANES * MOE_BLOCK // 16)
    per_chunk = chunk // MOE_BLOCK
    cnt = pl.pallas_call(
        _count_kernel,
        grid=(n_tok // chunk,),
        in_specs=[pl.BlockSpec((N_EXPERTS, chunk), lambda i: (0, i))],
        out_specs=pl.BlockSpec((1, N_EXPERTS, COUNT_LANES), lambda i: (i, 0, 0)),
        out_shape=jax.ShapeDtypeStruct((n_tok // chunk, N_EXPERTS, COUNT_LANES), F32),
        compiler_params=params,
        name="moe_count",
    )(gates_t)
    return jnp.swapaxes(cnt[:, :, :per_chunk], 1, 2).reshape(n_tok // MOE_BLOCK, N_EXPERTS).astype(jnp.int32)


def _stream_specs(x1, block_of):
    n_seq, seq_len, _ = x1.shape
    if seq_len < STEP_TOKENS:
        ns, ls = STEP_TOKENS // seq_len, seq_len
        x_map = lambda b, *_: (block_of(b), 0, 0)
        mod_map = x_map
    else:
        ns, ls = 1, STEP_TOKENS
        per_seq = seq_len // STEP_TOKENS
        x_map = lambda b, *_: (block_of(b) // per_seq, block_of(b) % per_seq, 0)
        mod_map = lambda b, *_: (block_of(b) // per_seq, 0, 0)
    return pl.BlockSpec((ns, ls, D_MODEL), x_map), pl.BlockSpec((ns, MOD_ROWS, D_MODEL), mod_map)


def _moe(stream_a, stream_b, w):
    (h2_a, g_a, x1_a, mod_a), (h2_b, g_b, x1_b, mod_b) = stream_a, stream_b
    nb_a, nb_b = h2_a.shape[0] // MOE_BLOCK, h2_b.shape[0] // MOE_BLOCK
    n_blocks = nb_a + nb_b
    n_tok = n_blocks * MOE_BLOCK
    ffn_tile = FFN_TILE
    rows_max = (n_tok * TOP_K + n_blocks * N_EXPERTS * SEG_ALIGN
                + N_EXPERTS * (ffn_tile - SEG_ALIGN))
    n_tiles_max = 2 * -(-rows_max // (2 * ffn_tile))
    rows_alloc = n_tiles_max * ffn_tile
    params = pltpu.CompilerParams(dimension_semantics=("arbitrary",), vmem_limit_bytes=VMEM_LIMIT)

    cnt = jnp.concatenate([_count_blocks(g_a, params), _count_blocks(g_b, params)], axis=0)
    block_tables, tail_off, tail_len, tile_expert, n_used = _segment_plan(cnt, n_tiles_max, ffn_tile)

    assert nb_a % STEP_BLOCKS == 0 and nb_b % STEP_BLOCKS == 0
    steps_a, n_steps = nb_a // STEP_BLOCKS, n_blocks // STEP_BLOCKS
    in_a = lambda s: jnp.minimum(s, steps_a - 1)
    in_b = lambda s: jnp.maximum(s - steps_a, 0)
    tok_specs = [pl.BlockSpec((N_EXPERTS, STEP_TOKENS), lambda s, *_: (0, in_a(s))),
                 pl.BlockSpec((N_EXPERTS, STEP_TOKENS), lambda s, *_: (0, in_b(s))),
                 pl.BlockSpec((STEP_TOKENS, D_MODEL), lambda s, *_: (in_a(s), 0)),
                 pl.BlockSpec((STEP_TOKENS, D_MODEL), lambda s, *_: (in_b(s), 0))]
    any_spec = pl.BlockSpec(memory_space=pl.ANY)
    perm_scratch = [pltpu.VMEM((BLOCK_ROWS, MOE_BLOCK), BF16), pltpu.VMEM((N_EXPERTS, MOE_BLOCK), F32),
                    pltpu.VMEM((N_EXPERTS, STEP_TOKENS), F32)]
    row_buffers = pltpu.VMEM((2 * STEP_BLOCKS, BLOCK_ROWS, D_MODEL), BF16)

    x_sorted = pl.pallas_call(
        functools.partial(_dispatch_kernel, n_steps=n_steps, n_steps_a=steps_a),
        grid_spec=pltpu.PrefetchScalarGridSpec(
            num_scalar_prefetch=7, grid=(n_steps,),
            in_specs=tok_specs, out_specs=any_spec,
            scratch_shapes=perm_scratch + [
                pltpu.VMEM((STEP_TOKENS, D_MODEL), BF16),
                row_buffers,
                pltpu.VMEM((ffn_tile, D_MODEL), BF16),
                pltpu.SemaphoreType.DMA((2 * STEP_BLOCKS,)), pltpu.SemaphoreType.DMA(())]),
        out_shape=jax.ShapeDtypeStruct((rows_alloc, D_MODEL), BF16),
        compiler_params=params,
        name="moe_dispatch",
    )(*block_tables, tail_off, tail_len, g_a, g_b, h2_a, h2_b)

    last_used = lambda i, nu: jnp.maximum(jnp.minimum(i, (nu[0] - 1) // 2), 0)
    row_map = lambda i, te, nu: (last_used(i, nu), 0)
    w_first = lambda i, te, nu: (te[2 * last_used(i, nu)], 0, 0)
    w_second = lambda i, te, nu: (te[2 * last_used(i, nu) + 1], 0, 0)
    gate_up, down = (1, D_MODEL, EXPERT_FF), (1, EXPERT_FF, D_MODEL)
    y_sorted = pl.pallas_call(
        _ffn_kernel,
        grid_spec=pltpu.PrefetchScalarGridSpec(
            num_scalar_prefetch=2, grid=(n_tiles_max // 2,),
            in_specs=[pl.BlockSpec((2 * ffn_tile, D_MODEL), row_map),
                      pl.BlockSpec(gate_up, w_first), pl.BlockSpec(gate_up, w_first),
                      pl.BlockSpec(down, w_first),
                      pl.BlockSpec(gate_up, w_second), pl.BlockSpec(gate_up, w_second),
                      pl.BlockSpec(down, w_second)],
            out_specs=pl.BlockSpec((2 * ffn_tile, D_MODEL), row_map),
            scratch_shapes=[pltpu.VMEM((2, D_MODEL, 2 * EXPERT_FF), BF16),
                            pltpu.VMEM((2, EXPERT_FF, D_MODEL), BF16),
                            pltpu.SMEM((2,), jnp.int32)]),
        out_shape=jax.ShapeDtypeStruct((rows_alloc, D_MODEL), BF16),
        compiler_params=params,
        name="moe_ffn",
    )(tile_expert, n_used, x_sorted, w["w_e_gate"], w["w_e_up"], w["w_e_down"],
      w["w_e_gate"], w["w_e_up"], w["w_e_down"])

    full2 = lambda b, *_: (0, 0)
    x_spec_a, mod_spec_a = _stream_specs(x1_a, in_a)
    x_spec_b, mod_spec_b = _stream_specs(x1_b, in_b)
    return pl.pallas_call(
        functools.partial(_combine_kernel, n_steps=n_steps, n_steps_a=steps_a),
        grid_spec=pltpu.PrefetchScalarGridSpec(
            num_scalar_prefetch=5, grid=(n_steps,),
            in_specs=tok_specs + [
                x_spec_a, x_spec_b, mod_spec_a, mod_spec_b,
                pl.BlockSpec((D_MODEL, 2 * SHARED_FF), full2),
                pl.BlockSpec((SHARED_FF, D_MODEL), full2),
                pl.BlockSpec((1, D_MODEL), full2),
                any_spec],
            out_specs=[x_spec_a, x_spec_b],
            scratch_shapes=perm_scratch + [
                row_buffers,
                pltpu.VMEM((STEP_TOKENS, D_MODEL), F32),
                pltpu.SemaphoreType.DMA((2 * STEP_BLOCKS,))]),
        out_shape=[jax.ShapeDtypeStruct(x1_a.shape, F32), jax.ShapeDtypeStruct(x1_b.shape, F32)],
        compiler_params=params,
        name="moe_combine",
    )(*block_tables, g_a, g_b, h2_a, h2_b, x1_a, x1_b, mod_a, mod_b,
      w["w_sh_gu"], w["w_sh_down"], w["g_post2"], y_sorted)


def _prep_weights(g_pre1, g_post1, w_in, w_gk_up, b_gk, g_gmlp_v, w_s, b_s, g_gla_o, w_out, g_pre2,
                  g_post2, w_router, b_router, w_e_gate, w_e_up, w_e_down, w_sh_gate, w_sh_up,
                  w_sh_down):
    row = lambda v: v.reshape(1, -1)
    wr = jnp.pad(w_router, ((0, 0), (0, LANES - N_EXPERTS)))
    wr_top = lax.bitcast_convert_type(
        lax.bitcast_convert_type(wr, jnp.uint32) & jnp.uint32(0xFFFF0000), F32)
    wr_hi = wr_top.astype(BF16)
    wr_lo = (wr - wr_top).astype(BF16)
    wr_split = jnp.concatenate([jnp.concatenate([wr_hi, wr_lo], axis=1),
                                jnp.concatenate([wr_hi, jnp.zeros_like(wr_lo)], axis=1)], axis=0)
    return {
        "g_pre1": row(g_pre1), "g_post1": row(g_post1), "g_pre2": row(g_pre2), "g_post2": row(g_post2),
        "w_in": jnp.concatenate([w_in[:, :OFF_Z], _fold_gate(w_in[:, OFF_Z:], w_gk_up)],
                                axis=1).astype(BF16),
        "b_gk": row(b_gk),
        "g_gmlp_v": row(g_gmlp_v),
        "w_s": w_s,
        "b_s_full": jnp.repeat(b_s.T, GMLP_HEAD, axis=1),
        "g_gla_o": g_gla_o,
        "w_out": w_out.astype(BF16),
        "w_router": wr_split,
        "b_router": b_router.reshape(N_EXPERTS, 1),
        "w_e_gate": w_e_gate, "w_e_up": w_e_up, "w_e_down": w_e_down,
        "w_sh_gu": jnp.concatenate([w_sh_gate, w_sh_up], axis=-1).astype(BF16),
        "w_sh_down": w_sh_down.astype(BF16),
    }


PROMPT_TILE = 1024
SAMPLE_TILE = 256


def _mix(x, mod, s0, w, *, emit_v):
    n_seq, seq_len, _ = x.shape
    s0_t = jnp.swapaxes(s0, -1, -2)
    if seq_len >= GMLP_CHUNK:
        outs = _mixer(x, mod, s0_t, w, seq_tile=PROMPT_TILE, chunk=GMLP_CHUNK, emit_v=emit_v)
    else:
        outs = _mixer(x, mod, s0_t, w, seq_tile=SAMPLE_TILE, chunk=seq_len, emit_v=emit_v)
    x1, h2, gates_t, st = outs[:4]
    v = outs[4].reshape(n_seq, seq_len, GMLP_GROUPS, GMLP_HEAD) if emit_v else None
    return (h2, gates_t, x1, mod), jnp.swapaxes(st, -1, -2), v


def kernel(x_prompt, x_sample, state_gla, c_prompt, c_sample, w_ada, b_ada, g_pre1, g_post1, w_in, w_gk_up, b_gk, g_gmlp_v, w_s, b_s, g_gla_o, w_out, g_pre2, g_post2, w_router, b_router, w_e_gate, w_e_up, w_e_down, w_sh_gate, w_sh_up, w_sh_down):
    depth = w_ada.shape[0]
    n_p, n_s = x_prompt.shape[0], x_sample.shape[0]
    c_all = jnp.concatenate([c_prompt, c_sample], axis=0)
    x_p, x_s = x_prompt, x_sample
    sp_list, ss_list, vs_list = [], [], []
    for l in range(depth):
        w = _prep_weights(g_pre1[l], g_post1[l], w_in[l], w_gk_up[l], b_gk[l], g_gmlp_v[l], w_s[l],
                          b_s[l], g_gla_o[l], w_out[l], g_pre2[l], g_post2[l], w_router[l],
                          b_router[l], w_e_gate[l], w_e_up[l], w_e_down[l], w_sh_gate[l], w_sh_up[l],
                          w_sh_down[l])
        mod = _ada(c_all, w_ada[l], b_ada[l])
        mod = jnp.pad(jnp.swapaxes(mod, 0, 1), ((0, 0), (0, MOD_ROWS - 6), (0, 0)))
        s0_p = jnp.zeros((n_p, GLA_HEADS, GLA_DK, GLA_DV), F32)
        moe_p, s_p, _ = _mix(x_p, mod[:n_p], s0_p, w, emit_v=False)
        moe_s, s_s, v_s = _mix(x_s, mod[n_p:], state_gla[l], w, emit_v=True)
        x_p, x_s = _moe(moe_p, moe_s, w)
        sp_list.append(s_p)
        ss_list.append(s_s)
        vs_list.append(v_s)
    return (x_p, x_s, jnp.stack(sp_list), jnp.stack(ss_list), jnp.stack(vs_list))
```

```python
import functools

import jax
import jax.numpy as jnp
from jax import lax
from jax.experimental import pallas as pl
from jax.experimental.pallas import tpu as pltpu

D_MODEL = 1024
GMLP_WIDTH = 512
GMLP_GROUPS = 4
GMLP_HEAD = 128
GMLP_CHUNK = 128
CAUSAL_BLOCK = 64
GLA_WIDTH = 512
GLA_HEADS = 4
GLA_DV = 128
GLA_DK = 64
GLA_KEY_WIDTH = 256
GLA_GATE_RANK = 16
GLA_GATE_NORMALIZER = 16.0
N_EXPERTS = 64
TOP_K = 8
N_EXPERT_GROUPS = 8
GROUP_SIZE = N_EXPERTS // N_EXPERT_GROUPS
TOPK_GROUPS = 4
EXPERT_FF = 256
SHARED_FF = 256
ROUTED_SCALE = 2.5
EPS = 1e-6

LANES = 128
GK_PAD = LANES
IN_WIDTH_PAD = 2 * GMLP_WIDTH + 2 * GLA_KEY_WIDTH + 2 * GLA_WIDTH + GLA_KEY_WIDTH
OFF_U = 0
OFF_VG = GMLP_WIDTH
OFF_Q = 2 * GMLP_WIDTH
OFF_K = OFF_Q + GLA_KEY_WIDTH
OFF_VL = OFF_K + GLA_KEY_WIDTH
OFF_R = OFF_VL + GLA_WIDTH
OFF_Z = OFF_R + GLA_WIDTH
MOD_ROWS = 8
VMEM_LIMIT = 56 * 1024 * 1024

F32 = jnp.float32
BF16 = jnp.bfloat16
NT_DIMS = (((1,), (1,)), ((), ()))
TN_DIMS = (((0,), (0,)), ((), ()))


def _rms(x, g):
    return x * lax.rsqrt(jnp.mean(x * x, axis=-1, keepdims=True) + EPS) * g


def _gelu(x):
    return 0.5 * x * (1.0 + jnp.tanh(0.7978845608028654 * (x + 0.044715 * (x * x * x))))


def _sigmoid(x):
    return 1.0 / (1.0 + jnp.exp(-x))


def _silu(x):
    return x * _sigmoid(x)


def _ada_kernel(c_ref, w_ref, b_ref, o_ref):
    a = _silu(c_ref[...])
    o_ref[0] = jnp.dot(a, w_ref[...], precision=lax.Precision.HIGHEST,
                       preferred_element_type=F32) + b_ref[0]


def _ada(c_all, w_ada, b_ada):
    n = c_all.shape[0]
    return pl.pallas_call(
        _ada_kernel,
        grid=(6,),
        in_specs=[pl.BlockSpec((n, D_MODEL), lambda j: (0, 0)),
                  pl.BlockSpec((D_MODEL, D_MODEL), lambda j: (0, j)),
                  pl.BlockSpec((1, 1, D_MODEL), lambda j: (j, 0, 0))],
        out_specs=pl.BlockSpec((1, n, D_MODEL), lambda j: (j, 0, 0)),
        out_shape=jax.ShapeDtypeStruct((6, n, D_MODEL), F32),
        compiler_params=pltpu.CompilerParams(vmem_limit_bytes=VMEM_LIMIT),
        name="ada",
    )(c_all, w_ada, b_ada.reshape(6, 1, D_MODEL))


def _fold_gate_kernel(a_ref, b_ref, o_ref):
    o_ref[...] = jnp.dot(a_ref[...], b_ref[...], precision=lax.Precision.HIGHEST,
                         preferred_element_type=F32)


def _fold_gate(w_in_gate, w_gk_up):
    a = jnp.pad(w_in_gate, ((0, 0), (0, GK_PAD - GLA_GATE_RANK)))
    b = jnp.pad(w_gk_up, ((0, GK_PAD - GLA_GATE_RANK), (0, 0)))
    return pl.pallas_call(
        _fold_gate_kernel,
        out_shape=jax.ShapeDtypeStruct((D_MODEL, GLA_KEY_WIDTH), F32),
        name="fold_gate",
    )(a, b)


def _route(logits_t, bias_t):
    t = logits_t.shape[1]
    scores = _sigmoid(logits_t)
    sel = scores + bias_t
    sub = lax.broadcasted_iota(jnp.int32, (GROUP_SIZE, t), 0)
    gscore = []
    for g in range(N_EXPERT_GROUPS):
        blk = sel[g * GROUP_SIZE:(g + 1) * GROUP_SIZE]
        m1 = jnp.max(blk, axis=0, keepdims=True)
        first = jnp.min(jnp.where(blk == m1, sub, GROUP_SIZE), axis=0, keepdims=True)
        m2 = jnp.max(jnp.where(sub == first, -jnp.inf, blk), axis=0, keepdims=True)
        gscore.append(m1 + m2)
    neg = jnp.full((GROUP_SIZE, t), -jnp.inf, F32)
    masked = []
    for g in range(N_EXPERT_GROUPS):
        rank = jnp.zeros((1, t), jnp.int32)
        for o in range(N_EXPERT_GROUPS):
            if o == g:
                continue
            ahead = (gscore[o] >= gscore[g]) if o < g else (gscore[o] > gscore[g])
            rank = rank + jnp.where(ahead, 1, 0)
        keep = jnp.broadcast_to(rank < TOPK_GROUPS, (GROUP_SIZE, t))
        masked.append(jnp.where(keep, sel[g * GROUP_SIZE:(g + 1) * GROUP_SIZE], neg))
    selm = jnp.concatenate(masked, axis=0)
    eidx = lax.broadcasted_iota(jnp.int32, (N_EXPERTS, t), 0)
    picked = jnp.zeros((N_EXPERTS, t), F32)
    for _ in range(TOP_K):
        best = jnp.max(selm, axis=0, keepdims=True)
        first = jnp.min(jnp.where(selm == best, eidx, N_EXPERTS), axis=0, keepdims=True)
        hit = eidx == first
        picked = jnp.where(hit, 1.0, picked)
        selm = jnp.where(hit, -jnp.inf, selm)
    chosen = jnp.where(picked > 0.0, scores, 0.0)
    denom = jnp.sum(chosen, axis=0, keepdims=True)
    return chosen * (ROUTED_SCALE / denom)


def _mixer_kernel(x_ref, mod_ref, s0_ref, gpre1_ref, win_ref, bgk_ref, ggv_ref,
                  ws_ref, bsf_ref, ggo_ref, wout_ref, gpost1_ref, gpre2_ref, wr_ref, br_ref,
                  x1_ref, h2_ref, gates_ref, st_ref, *rest,
                  chunk, n_chunks, chunk_is_seq, emit_v):
    if emit_v:
        v_ref, *rest = rest
    proj, mix_scr = rest
    ns, ls, _ = x_ref.shape
    tm = ns * ls

    mod_row = lambda i: mod_ref[:, i:i + 1, :]

    h = _rms(x_ref[...], gpre1_ref[...]) * (1.0 + mod_row(1)) + mod_row(0)
    proj[...] = jnp.dot(h.reshape(tm, D_MODEL).astype(BF16), win_ref[...],
                        preferred_element_type=F32)

    if chunk_is_seq:
        st_ref[...] = s0_ref[...]
    else:
        @pl.when(pl.program_id(1) == 0)
        def _():
            st_ref[...] = s0_ref[...]

    row = lax.broadcasted_iota(jnp.int32, (chunk, chunk), 0)
    col = lax.broadcasted_iota(jnp.int32, (chunk, chunk), 1)
    tri = jnp.where(row >= col, 1.0, 0.0).astype(BF16)
    tri2 = jnp.concatenate([tri, tri], axis=1)
    causal = row >= col
    block_causal = (row // CAUSAL_BLOCK) >= (col // CAUSAL_BLOCK)
    wmix = [jnp.where(block_causal, ws_ref[g, 0:chunk, 0:chunk], 0.0).astype(BF16)
            for g in range(GMLP_GROUPS)]
    mid = chunk // 2

    for c in range(n_chunks):
        rows = slice(c * chunk, (c + 1) * chunk)
        z = proj[rows, OFF_Z:OFF_Z + GLA_KEY_WIDTH] + bgk_ref[...]
        la = (jnp.minimum(z, 0.0) - jnp.log(1.0 + jnp.exp(-jnp.abs(z)))) * (1.0 / GLA_GATE_NORMALIZER)
        la_hi = la.astype(BF16)
        la_lo = (la - la_hi.astype(F32)).astype(BF16)
        proj[rows, OFF_Z:OFF_Z + GLA_KEY_WIDTH] = jnp.dot(
            tri2, jnp.concatenate([la_hi, la_lo], axis=0), preferred_element_type=F32)

    def chunk_body(c, carry):
        rows = pl.ds(pl.multiple_of(c * chunk, chunk), chunk)
        sidx = c if chunk_is_seq else 0

        vg = _gelu(proj[rows, OFF_VG:OFF_VG + GMLP_WIDTH])
        vn = _rms(vg, ggv_ref[...])
        if emit_v:
            v_ref[rows, :] = vn
        vnb = vn.astype(BF16)
        mixed = jnp.concatenate(
            [jnp.dot(wmix[g], vnb[:, g * GMLP_HEAD:(g + 1) * GMLP_HEAD], preferred_element_type=F32)
             for g in range(GMLP_GROUPS)], axis=-1) + bsf_ref[0:chunk, :]
        u = _gelu(proj[rows, OFF_U:OFF_U + GMLP_WIDTH])
        mix_scr[rows, 0:GMLP_WIDTH] = (u * mixed).astype(BF16)

        q = proj[rows, OFF_Q:OFF_Q + GLA_KEY_WIDTH] * (GLA_DK ** -0.5)
        k = proj[rows, OFF_K:OFF_K + GLA_KEY_WIDTH]
        b = proj[rows, OFF_Z:OFF_Z + GLA_KEY_WIDTH]
        b_mid = b[mid - 1:mid]
        b_end = b[chunk - 1:chunk]
        qt = (q * jnp.exp(b - b_mid)).astype(BF16)
        kt = (k * jnp.exp(b_mid - b)).astype(BF16)
        qs = (q * jnp.exp(b)).astype(BF16)
        kd = (k * jnp.exp(b_end - b)).astype(BF16)
        decay = jnp.exp(b_end)
        for hd in range(GLA_HEADS):
            ksl = slice(hd * GLA_DK, (hd + 1) * GLA_DK)
            vsl = slice(OFF_VL + hd * GLA_DV, OFF_VL + (hd + 1) * GLA_DV)
            rsl = slice(OFF_R + hd * GLA_DV, OFF_R + (hd + 1) * GLA_DV)
            att = lax.dot_general(qt[:, ksl], kt[:, ksl], NT_DIMS, preferred_element_type=F32)
            att = jnp.where(causal, att, 0.0).astype(BF16)
            vh = proj[rows, vsl].astype(BF16)
            st = st_ref[sidx, hd]
            o = jnp.dot(att, vh, preferred_element_type=F32) + lax.dot_general(
                qs[:, ksl], st.astype(BF16), NT_DIMS, preferred_element_type=F32)
            on = _rms(o, ggo_ref[hd:hd + 1, :])
            mix_scr[rows, GMLP_WIDTH + hd * GLA_DV:GMLP_WIDTH + (hd + 1) * GLA_DV] = (
                on * _silu(proj[rows, rsl])).astype(BF16)
            st_ref[sidx, hd] = st * decay[:, ksl] + lax.dot_general(
                vh, kd[:, ksl], TN_DIMS, preferred_element_type=F32)
        return carry

    lax.fori_loop(0, n_chunks, chunk_body, 0, unroll=4)

    mixo = jnp.dot(mix_scr[...], wout_ref[...], preferred_element_type=F32).reshape(ns, ls, D_MODEL)
    x1 = x_ref[...] + mod_row(2) * _rms(mixo, gpost1_ref[...])
    x1_ref[...] = x1
    h2 = (_rms(x1, gpre2_ref[...]) * (1.0 + mod_row(4)) + mod_row(3)).reshape(tm, D_MODEL)
    h2_hi = h2.astype(BF16)
    h2_ref[...] = h2_hi

    h2_lo = (h2 - h2_hi.astype(F32)).astype(BF16)
    parts = jnp.dot(jnp.concatenate([h2_hi, h2_lo], axis=-1), wr_ref[...], preferred_element_type=F32)
    logits = parts[:, 0:LANES] + parts[:, LANES:2 * LANES]
    gates_ref[...] = _route(logits.T[0:N_EXPERTS], br_ref[...])


def _mixer(x, mod, s0_t, w, *, seq_tile, chunk, emit_v):
    n_seq, seq_len, _ = x.shape
    chunk_is_seq = seq_len == chunk
    if chunk_is_seq:
        ns, ls = seq_tile // chunk, chunk
        grid = (1, n_seq // ns)
        x_map = lambda b, t: (t, 0, 0)
        seq_map3 = lambda b, t: (t, 0, 0)
        seq_map4 = lambda b, t: (t, 0, 0, 0)
    else:
        ns, ls = 1, seq_tile
        grid = (n_seq, seq_len // seq_tile)
        x_map = lambda b, t: (b, t, 0)
        seq_map3 = lambda b, t: (b, 0, 0)
        seq_map4 = lambda b, t: (b, 0, 0, 0)
    tm = ns * ls
    n_tok = n_seq * seq_len
    tok_block = lambda b, t: b * grid[1] + t
    tok_map = lambda b, t: (tok_block(b, t), 0)
    full2 = lambda b, t: (0, 0)
    full3 = lambda b, t: (0, 0, 0)

    in_specs = [
        pl.BlockSpec((ns, ls, D_MODEL), x_map),
        pl.BlockSpec((ns, MOD_ROWS, D_MODEL), seq_map3),
        pl.BlockSpec((ns, GLA_HEADS, GLA_DV, GLA_DK), seq_map4),
        pl.BlockSpec((1, D_MODEL), full2),
        pl.BlockSpec((D_MODEL, IN_WIDTH_PAD), full2),
        pl.BlockSpec((1, GLA_KEY_WIDTH), full2),
        pl.BlockSpec((1, GMLP_WIDTH), full2),
        pl.BlockSpec((GMLP_GROUPS, GMLP_CHUNK, GMLP_CHUNK), full3),
        pl.BlockSpec((GMLP_CHUNK, GMLP_WIDTH), full2),
        pl.BlockSpec((GLA_HEADS, GLA_DV), full2),
        pl.BlockSpec((D_MODEL, D_MODEL), full2),
        pl.BlockSpec((1, D_MODEL), full2),
        pl.BlockSpec((1, D_MODEL), full2),
        pl.BlockSpec((2 * D_MODEL, 2 * LANES), full2),
        pl.BlockSpec((N_EXPERTS, 1), full2),
    ]
    out_specs = [
        pl.BlockSpec((ns, ls, D_MODEL), x_map),
        pl.BlockSpec((tm, D_MODEL), tok_map),
        pl.BlockSpec((N_EXPERTS, tm), lambda b, t: (0, tok_block(b, t))),
        pl.BlockSpec((ns, GLA_HEADS, GLA_DV, GLA_DK), seq_map4),
    ]
    out_shape = [
        jax.ShapeDtypeStruct((n_seq, seq_len, D_MODEL), F32),
        jax.ShapeDtypeStruct((n_tok, D_MODEL), BF16),
        jax.ShapeDtypeStruct((N_EXPERTS, n_tok), F32),
        jax.ShapeDtypeStruct((n_seq, GLA_HEADS, GLA_DV, GLA_DK), F32),
    ]
    if emit_v:
        out_specs.append(pl.BlockSpec((tm, GMLP_WIDTH), tok_map))
        out_shape.append(jax.ShapeDtypeStruct((n_tok, GMLP_WIDTH), F32))

    kern = functools.partial(_mixer_kernel, chunk=chunk, n_chunks=tm // chunk,
                             chunk_is_seq=chunk_is_seq, emit_v=emit_v)
    return pl.pallas_call(
        kern,
        grid=grid,
        in_specs=in_specs,
        out_specs=out_specs,
        out_shape=out_shape,
        scratch_shapes=[pltpu.VMEM((tm, IN_WIDTH_PAD), F32), pltpu.VMEM((tm, D_MODEL), BF16)],
        compiler_params=pltpu.CompilerParams(
            dimension_semantics=("arbitrary", "arbitrary"), vmem_limit_bytes=VMEM_LIMIT),
        name="mixer_sample" if chunk_is_seq else "mixer_prompt",
    )(x, mod, s0_t, w["g_pre1"], w["w_in"], w["b_gk"], w["g_gmlp_v"], w["w_s"],
      w["b_s_full"], w["g_gla_o"], w["w_out"], w["g_post1"], w["g_pre2"], w["w_router"],
      w["b_router"])


MOE_BLOCK = 256
SEG_ALIGN = 16
WIN = 64
FFN_TILE = 1280
BLOCK_ROWS = MOE_BLOCK * TOP_K + SEG_ALIGN * N_EXPERTS + WIN
MAIN_ROWS = 2688
STEP_BLOCKS = 2
STEP_TOKENS = STEP_BLOCKS * MOE_BLOCK
COL_CHUNK = 256
COUNT_LANES = LANES


def _count_kernel(g_ref, o_ref):
    n = g_ref.shape[1]
    selb = jnp.where(g_ref[...] > 0.0, 1.0, 0.0).astype(BF16)
    tok = lax.broadcasted_iota(jnp.int32, (n, COUNT_LANES), 0)
    lane = lax.broadcasted_iota(jnp.int32, (n, COUNT_LANES), 1)
    ind = jnp.where(tok // MOE_BLOCK == lane, 1.0, 0.0).astype(BF16)
    o_ref[0] = jnp.dot(selb, ind, preferred_element_type=F32)


def _fill_permutation(p_scr, rk_scr, g_ref, cntp_ref, loff_ref, long_ref, blk, weighted):
    tb = g_ref.shape[1]
    g = g_ref[...]
    sel = g > 0.0
    r_i = lax.broadcasted_iota(jnp.int32, (tb, tb), 0)
    c_i = lax.broadcasted_iota(jnp.int32, (tb, tb), 1)
    earlier = jnp.where(r_i < c_i, 1.0, 0.0).astype(BF16)
    rank = jnp.dot(jnp.where(sel, 1.0, 0.0).astype(BF16), earlier, preferred_element_type=F32)
    rk = jnp.where(sel, rank, -1.0)
    p_scr[...] = jnp.zeros_like(p_scr)
    jwin = lax.broadcasted_iota(jnp.int32, (WIN, tb), 0).astype(F32)

    for e in range(N_EXPERTS):
        off = pl.multiple_of(loff_ref[blk * N_EXPERTS + e], SEG_ALIGN)
        val = g[e:e + 1] if weighted else 1.0
        p_scr[pl.ds(off, WIN), :] = jnp.where(rk[e:e + 1] == jwin, val, 0.0).astype(BF16)

    @pl.when(long_ref[blk] > 0)
    def _():
        rk_scr[...] = rk

        def expert_body(e, c):
            n = cntp_ref[blk * N_EXPERTS + e]
            off = loff_ref[blk * N_EXPERTS + e]
            rke = rk_scr[pl.ds(e, 1), :]
            val = g_ref[pl.ds(e, 1), :] if weighted else 1.0

            def win_body(wi, c2):
                rows = pl.ds(pl.multiple_of(off + wi * WIN, SEG_ALIGN), WIN)
                j = jwin + (wi * WIN).astype(F32)
                new = jnp.where(rke == j, val, 0.0)
                p_scr[rows, :] = jnp.where(j < n.astype(F32), new, p_scr[rows, :].astype(F32)).astype(BF16)
                return c2

            lax.fori_loop(1, (n + WIN - 1) // WIN, win_body, 0)
            return c

        lax.fori_loop(0, N_EXPERTS, expert_body, 0)


def _start_segments(cntp_ref, loff_ref, blk, make_copy):
    for e in range(N_EXPERTS):
        make_copy(e, pl.multiple_of(loff_ref[blk * N_EXPERTS + e], SEG_ALIGN),
                  pl.multiple_of(cntp_ref[blk * N_EXPERTS + e], SEG_ALIGN)).start(priority=e % 2)


def _dispatch_kernel(cntp_ref, loff_ref, goff_ref, rows_ref, long_ref, toff_ref, tlen_ref,
                     ga_ref, gb_ref, h2a_ref, h2b_ref, xs_hbm,
                     p_scr, rk_scr, g_scr, h2_scr, xs_scr, zero_scr, sems, tail_sem,
                     *, n_steps, n_steps_a):
    step = pl.program_id(0)
    parity = lax.rem(step, 2)
    from_a = step < n_steps_a
    g_scr[...] = jnp.where(from_a, ga_ref[...], gb_ref[...])
    h2_scr[...] = jnp.where(from_a, h2a_ref[...], h2b_ref[...])

    def seg_copy(blk, s):
        def make(e, off, n):
            dst = pl.multiple_of(goff_ref[blk * N_EXPERTS + e], SEG_ALIGN)
            return pltpu.make_async_copy(xs_scr.at[s, pl.ds(off, n)], xs_hbm.at[pl.ds(dst, n)], sems.at[s])
        return make

    def wait_block(blk, s):
        n = pl.multiple_of(rows_ref[blk], SEG_ALIGN)
        pltpu.make_async_copy(xs_scr.at[s, pl.ds(0, n)], xs_hbm.at[pl.ds(0, n)], sems.at[s]).wait()

    def for_tails(action):
        def body(e, c):
            n = tlen_ref[e]

            @pl.when(n > 0)
            def _():
                dst = pl.multiple_of(toff_ref[e], SEG_ALIGN)
                nn = pl.multiple_of(n, SEG_ALIGN)
                action(pltpu.make_async_copy(zero_scr.at[pl.ds(0, nn)], xs_hbm.at[pl.ds(dst, nn)], tail_sem))

            return c

        lax.fori_loop(0, N_EXPERTS, body, 0)

    @pl.when(step == 0)
    def _():
        zero_scr[...] = jnp.zeros_like(zero_scr)
        for_tails(lambda cp: cp.start())

    for j in range(STEP_BLOCKS):
        blk = step * STEP_BLOCKS + j
        slot = parity * STEP_BLOCKS + j
        tok = slice(j * MOE_BLOCK, (j + 1) * MOE_BLOCK)

        @pl.when(step >= 2)
        def _():
            wait_block(blk - 2 * STEP_BLOCKS, slot)

        _fill_permutation(p_scr, rk_scr, g_scr.at[:, tok], cntp_ref, loff_ref, long_ref, blk,
                          weighted=False)

        def sort_rows(rows):
            for c in range(D_MODEL // COL_CHUNK):
                cols = slice(c * COL_CHUNK, (c + 1) * COL_CHUNK)
                xs = jnp.dot(p_scr[rows, :], h2_scr[tok, cols], preferred_element_type=F32)
                xs_scr[slot, rows, cols] = xs.astype(BF16)

        sort_rows(slice(0, MAIN_ROWS))

        @pl.when(rows_ref[blk] > MAIN_ROWS)
        def _():
            sort_rows(slice(MAIN_ROWS, BLOCK_ROWS))

        _start_segments(cntp_ref, loff_ref, blk, seg_copy(blk, slot))

    @pl.when(step == n_steps - 1)
    def _():
        for_tails(lambda cp: cp.wait())
        for j in range(STEP_BLOCKS):
            blk = step * STEP_BLOCKS + j
            if n_steps >= 2:
                wait_block(blk - STEP_BLOCKS, (1 - parity) * STEP_BLOCKS + j)
            wait_block(blk, parity * STEP_BLOCKS + j)


def _ffn_kernel(te_ref, nu_ref, x_ref, wga_ref, wua_ref, wda_ref, wgb_ref, wub_ref, wdb_ref, y_ref,
                wgu_scr, wd_scr, held_ref):
    i = pl.program_id(0)
    tile = x_ref.shape[0] // 2
    first, second = 2 * i, 2 * i + 1
    e_first, e_second = te_ref[first], te_ref[second]
    one_expert = jnp.logical_and(second < nu_ref[0], e_second == e_first)
    two_experts = jnp.logical_and(second < nu_ref[0], e_second != e_first)

    @pl.when(i == 0)
    def _():
        held_ref[0] = -1
        held_ref[1] = -1

    def hold(k, expert, wg_ref, wu_ref, wd_ref):
        @pl.when(held_ref[k] != expert)
        def _():
            wgu_scr[k, :, 0:EXPERT_FF] = wg_ref[0].astype(BF16)
            wgu_scr[k, :, EXPERT_FF:2 * EXPERT_FF] = wu_ref[0].astype(BF16)
            wd_scr[k] = wd_ref[0].astype(BF16)
            held_ref[k] = expert

    def ffn(rows, k):
        gu = jnp.dot(x_ref[rows, :], wgu_scr[k], preferred_element_type=F32)
        hid = _silu(gu[:, 0:EXPERT_FF]) * gu[:, EXPERT_FF:2 * EXPERT_FF]
        y_ref[rows, :] = jnp.dot(hid.astype(BF16), wd_scr[k], preferred_element_type=F32).astype(BF16)

    @pl.when(first < nu_ref[0])
    def _():
        hold(0, e_first, wga_ref, wua_ref, wda_ref)

        @pl.when(one_expert)
        def _():
            ffn(slice(0, 2 * tile), 0)

        @pl.when(jnp.logical_not(one_expert))
        def _():
            ffn(slice(0, tile), 0)

        @pl.when(two_experts)
        def _():
            hold(1, e_second, wgb_ref, wub_ref, wdb_ref)
            ffn(slice(tile, 2 * tile), 1)


def _combine_kernel(cntp_ref, loff_ref, goff_ref, rows_ref, long_ref,
                    ga_ref, gb_ref, h2a_ref, h2b_ref, x1a_ref, x1b_ref, moda_ref, modb_ref,
                    wsgu_ref, wsd_ref, gpost2_ref, ys_hbm, outa_ref, outb_ref,
                    p_scr, rk_scr, g_scr, ys_scr, routed_scr, sems, *, n_steps, n_steps_a):
    step = pl.program_id(0)
    parity = lax.rem(step, 2)
    from_a = step < n_steps_a
    g_scr[...] = jnp.where(from_a, ga_ref[...], gb_ref[...])

    def seg_copy(blk, s):
        def make(e, off, n):
            src = pl.multiple_of(goff_ref[blk * N_EXPERTS + e], SEG_ALIGN)
            return pltpu.make_async_copy(ys_hbm.at[pl.ds(src, n)], ys_scr.at[s, pl.ds(off, n)], sems.at[s])
        return make

    def start_step(st, par):
        for j in range(STEP_BLOCKS):
            _start_segments(cntp_ref, loff_ref, st * STEP_BLOCKS + j,
                            seg_copy(st * STEP_BLOCKS + j, par * STEP_BLOCKS + j))

    @pl.when(step == 0)
    def _():
        ys_scr[...] = jnp.zeros_like(ys_scr)
        start_step(0, 0)

    @pl.when(step + 1 < n_steps)
    def _():
        start_step(step + 1, 1 - parity)

    for j in range(STEP_BLOCKS):
        blk = step * STEP_BLOCKS + j
        slot = parity * STEP_BLOCKS + j
        tok = slice(j * MOE_BLOCK, (j + 1) * MOE_BLOCK)
        _fill_permutation(p_scr, rk_scr, g_scr.at[:, tok], cntp_ref, loff_ref, long_ref, blk,
                          weighted=True)
        n_rows = pl.multiple_of(rows_ref[blk], SEG_ALIGN)
        pltpu.make_async_copy(ys_hbm.at[pl.ds(0, n_rows)], ys_scr.at[slot, pl.ds(0, n_rows)],
                              sems.at[slot]).wait()

        def unsort_rows(rows):
            return lax.dot_general(p_scr[rows, :], ys_scr[slot, rows, :], TN_DIMS,
                                   preferred_element_type=F32)

        routed_scr[tok, :] = unsort_rows(slice(0, MAIN_ROWS))

        @pl.when(rows_ref[blk] > MAIN_ROWS)
        def _():
            routed_scr[tok, :] += unsort_rows(slice(MAIN_ROWS, BLOCK_ROWS))

    hx = jnp.where(from_a, h2a_ref[...], h2b_ref[...])
    sgu = jnp.dot(hx, wsgu_ref[...], preferred_element_type=F32)
    shid = _silu(sgu[:, 0:SHARED_FF]) * sgu[:, SHARED_FF:2 * SHARED_FF]
    f = routed_scr[...] + jnp.dot(shid.astype(BF16), wsd_ref[...], preferred_element_type=F32)
    routed_scr[...] = _rms(f, gpost2_ref[...])

    def finish(x1_ref, mod_ref, out_ref):
        gate2 = mod_ref[:, 5:6, :]
        out_ref[...] = x1_ref[...] + gate2 * routed_scr[...].reshape(x1_ref.shape)

    @pl.when(from_a)
    def _():
        finish(x1a_ref, moda_ref, outa_ref)

    @pl.when(jnp.logical_not(from_a))
    def _():
        finish(x1b_ref, modb_ref, outb_ref)


def _segment_plan(cnt, n_tiles_max, ffn_tile):
    cntp = jnp.maximum((cnt + SEG_ALIGN - 1) // SEG_ALIGN, 1) * SEG_ALIGN
    loff = jnp.cumsum(cntp, axis=1) - cntp
    block_rows = jnp.sum(cntp, axis=1)
    has_long = jnp.max(cntp, axis=1) > WIN
    tot = jnp.sum(cntp, axis=0)
    reg = (tot + ffn_tile - 1) // ffn_tile * ffn_tile
    base = jnp.cumsum(reg) - reg
    goff = base[None, :] + jnp.cumsum(cntp, axis=0) - cntp
    tile_end = jnp.cumsum(reg // ffn_tile)
    tile = jnp.arange(n_tiles_max, dtype=jnp.int32)
    tile_expert = jnp.minimum(jnp.sum(tile_end[None, :] <= tile[:, None], axis=1), N_EXPERTS - 1)
    i32 = lambda a: a.astype(jnp.int32)
    block_tables = (i32(cntp.reshape(-1)), i32(loff.reshape(-1)), i32(goff.reshape(-1)),
                    i32(block_rows), i32(has_long))
    return block_tables, i32(base + tot), i32(reg - tot), i32(tile_expert), i32(tile_end[-1:])


def _count_blocks(gates_t, params):
    n_tok = gates_t.shape[1]
    chunk = min(n_tok, COUNT_LANES * MOE_BLOCK // 16)
    per_chunk = chunk // MOE_BLOCK
    cnt = pl.pallas_call(
        _count_kernel,
        grid=(n_tok // chunk,),
        in_specs=[pl.BlockSpec((N_EXPERTS, chunk), lambda i: (0, i))],
        out_specs=pl.BlockSpec((1, N_EXPERTS, COUNT_LANES), lambda i: (i, 0, 0)),
        out_shape=jax.ShapeDtypeStruct((n_tok // chunk, N_EXPERTS, COUNT_LANES), F32),
        compiler_params=params,
        name="moe_count",
    )(gates_t)
    return jnp.swapaxes(cnt[:, :, :per_chunk], 1, 2).reshape(n_tok // MOE_BLOCK, N_EXPERTS).astype(jnp.int32)


def _stream_specs(x1, block_of):
    n_seq, seq_len, _ = x1.shape
    if seq_len < STEP_TOKENS:
        ns, ls = STEP_TOKENS // seq_len, seq_len
        x_map = lambda b, *_: (block_of(b), 0, 0)
        mod_map = x_map
    else:
        ns, ls = 1, STEP_TOKENS
        per_seq = seq_len // STEP_TOKENS
        x_map = lambda b, *_: (block_of(b) // per_seq, block_of(b) % per_seq, 0)
        mod_map = lambda b, *_: (block_of(b) // per_seq, 0, 0)
    return pl.BlockSpec((ns, ls, D_MODEL), x_map), pl.BlockSpec((ns, MOD_ROWS, D_MODEL), mod_map)


def _moe(stream_a, stream_b, w):
    (h2_a, g_a, x1_a, mod_a), (h2_b, g_b, x1_b, mod_b) = stream_a, stream_b
    nb_a, nb_b = h2_a.shape[0] // MOE_BLOCK, h2_b.shape[0] // MOE_BLOCK
    n_blocks = nb_a + nb_b
    n_tok = n_blocks * MOE_BLOCK
    ffn_tile = FFN_TILE
    rows_max = (n_tok * TOP_K + n_blocks * N_EXPERTS * SEG_ALIGN
                + N_EXPERTS * (ffn_tile - SEG_ALIGN))
    n_tiles_max = 2 * -(-rows_max // (2 * ffn_tile))
    rows_alloc = n_tiles_max * ffn_tile
    params = pltpu.CompilerParams(dimension_semantics=("arbitrary",), vmem_limit_bytes=VMEM_LIMIT)

    cnt = jnp.concatenate([_count_blocks(g_a, params), _count_blocks(g_b, params)], axis=0)
    block_tables, tail_off, tail_len, tile_expert, n_used = _segment_plan(cnt, n_tiles_max, ffn_tile)

    assert nb_a % STEP_BLOCKS == 0 and nb_b % STEP_BLOCKS == 0
    steps_a, n_steps = nb_a // STEP_BLOCKS, n_blocks // STEP_BLOCKS
    in_a = lambda s: jnp.minimum(s, steps_a - 1)
    in_b = lambda s: jnp.maximum(s - steps_a, 0)
    tok_specs = [pl.BlockSpec((N_EXPERTS, STEP_TOKENS), lambda s, *_: (0, in_a(s))),
                 pl.BlockSpec((N_EXPERTS, STEP_TOKENS), lambda s, *_: (0, in_b(s))),
                 pl.BlockSpec((STEP_TOKENS, D_MODEL), lambda s, *_: (in_a(s), 0)),
                 pl.BlockSpec((STEP_TOKENS, D_MODEL), lambda s, *_: (in_b(s), 0))]
    any_spec = pl.BlockSpec(memory_space=pl.ANY)
    perm_scratch = [pltpu.VMEM((BLOCK_ROWS, MOE_BLOCK), BF16), pltpu.VMEM((N_EXPERTS, MOE_BLOCK), F32),
                    pltpu.VMEM((N_EXPERTS, STEP_TOKENS), F32)]
    row_buffers = pltpu.VMEM((2 * STEP_BLOCKS, BLOCK_ROWS, D_MODEL), BF16)

    x_sorted = pl.pallas_call(
        functools.partial(_dispatch_kernel, n_steps=n_steps, n_steps_a=steps_a),
        grid_spec=pltpu.PrefetchScalarGridSpec(
            num_scalar_prefetch=7, grid=(n_steps,),
            in_specs=tok_specs, out_specs=any_spec,
            scratch_shapes=perm_scratch + [
                pltpu.VMEM((STEP_TOKENS, D_MODEL), BF16),
                row_buffers,
                pltpu.VMEM((ffn_tile, D_MODEL), BF16),
                pltpu.SemaphoreType.DMA((2 * STEP_BLOCKS,)), pltpu.SemaphoreType.DMA(())]),
        out_shape=jax.ShapeDtypeStruct((rows_alloc, D_MODEL), BF16),
        compiler_params=params,
        name="moe_dispatch",
    )(*block_tables, tail_off, tail_len, g_a, g_b, h2_a, h2_b)

    last_used = lambda i, nu: jnp.maximum(jnp.minimum(i, (nu[0] - 1) // 2), 0)
    row_map = lambda i, te, nu: (last_used(i, nu), 0)
    w_first = lambda i, te, nu: (te[2 * last_used(i, nu)], 0, 0)
    w_second = lambda i, te, nu: (te[2 * last_used(i, nu) + 1], 0, 0)
    gate_up, down = (1, D_MODEL, EXPERT_FF), (1, EXPERT_FF, D_MODEL)
    y_sorted = pl.pallas_call(
        _ffn_kernel,
        grid_spec=pltpu.PrefetchScalarGridSpec(
            num_scalar_prefetch=2, grid=(n_tiles_max // 2,),
            in_specs=[pl.BlockSpec((2 * ffn_tile, D_MODEL), row_map),
                      pl.BlockSpec(gate_up, w_first), pl.BlockSpec(gate_up, w_first),
                      pl.BlockSpec(down, w_first),
                      pl.BlockSpec(gate_up, w_second), pl.BlockSpec(gate_up, w_second),
                      pl.BlockSpec(down, w_second)],
            out_specs=pl.BlockSpec((2 * ffn_tile, D_MODEL), row_map),
            scratch_shapes=[pltpu.VMEM((2, D_MODEL, 2 * EXPERT_FF), BF16),
                            pltpu.VMEM((2, EXPERT_FF, D_MODEL), BF16),
                            pltpu.SMEM((2,), jnp.int32)]),
        out_shape=jax.ShapeDtypeStruct((rows_alloc, D_MODEL), BF16),
        compiler_params=params,
        name="moe_ffn",
    )(tile_expert, n_used, x_sorted, w["w_e_gate"], w["w_e_up"], w["w_e_down"],
      w["w_e_gate"], w["w_e_up"], w["w_e_down"])

    full2 = lambda b, *_: (0, 0)
    x_spec_a, mod_spec_a = _stream_specs(x1_a, in_a)
    x_spec_b, mod_spec_b = _stream_specs(x1_b, in_b)
    return pl.pallas_call(
        functools.partial(_combine_kernel, n_steps=n_steps, n_steps_a=steps_a),
        grid_spec=pltpu.PrefetchScalarGridSpec(
            num_scalar_prefetch=5, grid=(n_steps,),
            in_specs=tok_specs + [
                x_spec_a, x_spec_b, mod_spec_a, mod_spec_b,
                pl.BlockSpec((D_MODEL, 2 * SHARED_FF), full2),
                pl.BlockSpec((SHARED_FF, D_MODEL), full2),
                pl.BlockSpec((1, D_MODEL), full2),
                any_spec],
            out_specs=[x_spec_a, x_spec_b],
            scratch_shapes=perm_scratch + [
                row_buffers,
                pltpu.VMEM((STEP_TOKENS, D_MODEL), F32),
                pltpu.SemaphoreType.DMA((2 * STEP_BLOCKS,))]),
        out_shape=[jax.ShapeDtypeStruct(x1_a.shape, F32), jax.ShapeDtypeStruct(x1_b.shape, F32)],
        compiler_params=params,
        name="moe_combine",
    )(*block_tables, g_a, g_b, h2_a, h2_b, x1_a, x1_b, mod_a, mod_b,
      w["w_sh_gu"], w["w_sh_down"], w["g_post2"], y_sorted)


def _prep_weights(g_pre1, g_post1, w_in, w_gk_up, b_gk, g_gmlp_v, w_s, b_s, g_gla_o, w_out, g_pre2,
                  g_post2, w_router, b_router, w_e_gate, w_e_up, w_e_down, w_sh_gate, w_sh_up,
                  w_sh_down):
    row = lambda v: v.reshape(1, -1)
    wr = jnp.pad(w_router, ((0, 0), (0, LANES - N_EXPERTS)))
    wr_top = lax.bitcast_convert_type(
        lax.bitcast_convert_type(wr, jnp.uint32) & jnp.uint32(0xFFFF0000), F32)
    wr_hi = wr_top.astype(BF16)
    wr_lo = (wr - wr_top).astype(BF16)
    wr_split = jnp.concatenate([jnp.concatenate([wr_hi, wr_lo], axis=1),
                                jnp.concatenate([wr_hi, jnp.zeros_like(wr_lo)], axis=1)], axis=0)
    return {
        "g_pre1": row(g_pre1), "g_post1": row(g_post1), "g_pre2": row(g_pre2), "g_post2": row(g_post2),
        "w_in": jnp.concatenate([w_in[:, :OFF_Z], _fold_gate(w_in[:, OFF_Z:], w_gk_up)],
                                axis=1).astype(BF16),
        "b_gk": row(b_gk),
        "g_gmlp_v": row(g_gmlp_v),
        "w_s": w_s,
        "b_s_full": jnp.repeat(b_s.T, GMLP_HEAD, axis=1),
        "g_gla_o": g_gla_o,
        "w_out": w_out.astype(BF16),
        "w_router": wr_split,
        "b_router": b_router.reshape(N_EXPERTS, 1),
        "w_e_gate": w_e_gate, "w_e_up": w_e_up, "w_e_down": w_e_down,
        "w_sh_gu": jnp.concatenate([w_sh_gate, w_sh_up], axis=-1).astype(BF16),
        "w_sh_down": w_sh_down.astype(BF16),
    }


PROMPT_TILE = 1024
SAMPLE_TILE = 256


def _mix(x, mod, s0, w, *, emit_v):
    n_seq, seq_len, _ = x.shape
    s0_t = jnp.swapaxes(s0, -1, -2)
    if seq_len >= GMLP_CHUNK:
        outs = _mixer(x, mod, s0_t, w, seq_tile=PROMPT_TILE, chunk=GMLP_CHUNK, emit_v=emit_v)
    else:
        outs = _mixer(x, mod, s0_t, w, seq_tile=SAMPLE_TILE, chunk=seq_len, emit_v=emit_v)
    x1, h2, gates_t, st = outs[:4]
    v = outs[4].reshape(n_seq, seq_len, GMLP_GROUPS, GMLP_HEAD) if emit_v else None
    return (h2, gates_t, x1, mod), jnp.swapaxes(st, -1, -2), v


def kernel(x_prompt, x_sample, state_gla, c_prompt, c_sample, w_ada, b_ada, g_pre1, g_post1, w_in, w_gk_up, b_gk, g_gmlp_v, w_s, b_s, g_gla_o, w_out, g_pre2, g_post2, w_router, b_router, w_e_gate, w_e_up, w_e_down, w_sh_gate, w_sh_up, w_sh_down):
    depth = w_ada.shape[0]
    n_p, n_s = x_prompt.shape[0], x_sample.shape[0]
    c_all = jnp.concatenate([c_prompt, c_sample], axis=0)
    x_p, x_s = x_prompt, x_sample
    sp_list, ss_list, vs_list = [], [], []
    for l in range(depth):
        w = _prep_weights(g_pre1[l], g_post1[l], w_in[l], w_gk_up[l], b_gk[l], g_gmlp_v[l], w_s[l],
                          b_s[l], g_gla_o[l], w_out[l], g_pre2[l], g_post2[l], w_router[l],
                          b_router[l], w_e_gate[l], w_e_up[l], w_e_down[l], w_sh_gate[l], w_sh_up[l],
                          w_sh_down[l])
        mod = _ada(c_all, w_ada[l], b_ada[l])
        mod = jnp.pad(jnp.swapaxes(mod, 0, 1), ((0, 0), (0, MOD_ROWS - 6), (0, 0)))
        s0_p = jnp.zeros((n_p, GLA_HEADS, GLA_DK, GLA_DV), F32)
        moe_p, s_p, _ = _mix(x_p, mod[:n_p], s0_p, w, emit_v=False)
        moe_s, s_s, v_s = _mix(x_s, mod[n_p:], state_gla[l], w, emit_v=True)
        x_p, x_s = _moe(moe_p, moe_s, w)
        sp_list.append(s_p)
        ss_list.append(s_s)
        vs_list.append(v_s)
    return (x_p, x_s, jnp.stack(sp_list), jnp.stack(ss_list), jnp.stack(vs_list))
```
